```python
import math
import jax, jax.numpy as jnp
from jax import lax
import numpy as np

D_MODEL = 1024
BATCH = 4
SEQ = 4096
DEPTH = 4

MIX_WIDTH = D_MODEL
DA_WIDTH = MIX_WIDTH // 2
SC_WIDTH = MIX_WIDTH // 4
LRU_WIDTH = MIX_WIDTH - DA_WIDTH - SC_WIDTH

DA_HEAD_DIM = 64
DA_HEADS = DA_WIDTH // (2 * DA_HEAD_DIM)
Q_BLOCK = 128
NUM_BUCKETS = 32
MAX_DISTANCE = 128
SUBLN_EPS = 1e-5

SC_KERNEL = 3

LRU_BLOCKS = 4
LRU_BLOCK = LRU_WIDTH // LRU_BLOCKS
LRU_CONV = 4
LRU_C = 8.0

D_FF = 2816
RMS_EPS = 1e-6
NEG_INF = -1e30

IN_SPLIT_WIDTHS = (DA_WIDTH, DA_WIDTH, DA_WIDTH,
                   SC_WIDTH, SC_WIDTH, SC_WIDTH,
                   LRU_WIDTH, LRU_WIDTH)
IN_WIDTH = sum(IN_SPLIT_WIDTHS)
IN_SPLIT_POINTS = tuple(int(v) for v in np.cumsum(IN_SPLIT_WIDTHS)[:-1])

kernel_name = "hymba_diffattn_shortconv_rglru_macaron"


def rms_norm(x, g, eps=RMS_EPS):
    xf = x.astype(jnp.float32)
    y = xf * lax.rsqrt(jnp.mean(xf * xf, axis=-1, keepdims=True) + eps)
    return (y * g.astype(jnp.float32)).astype(x.dtype)


def swiglu(x, w_gate, w_up, w_down):
    return (jax.nn.silu(x @ w_gate) * (x @ w_up)) @ w_down


def causal_dwconv(x, w):
    K = w.shape[0]
    S = x.shape[1]
    xp = jnp.pad(x, ((0, 0), (K - 1, 0), (0, 0)))
    return sum(xp[:, k:k + S] * w[k] for k in range(K))


def t5_bucket(n):
    max_exact = NUM_BUCKETS // 2
    nf = jnp.maximum(n, 1).astype(jnp.float32)
    large = max_exact + (jnp.log(nf / max_exact) / math.log(MAX_DISTANCE / max_exact)
                         * (NUM_BUCKETS - max_exact)).astype(jnp.int32)
    large = jnp.minimum(large, NUM_BUCKETS - 1)
    return jnp.where(n < max_exact, n, large)


def diff_attention(q, k, v, lq1, lk1, lq2, lk2, subln_g, rel_bias, lam_init):
    B, S, _ = q.shape
    nblk = S // Q_BLOCK
    q = q.reshape(B, S, DA_HEADS, 2, DA_HEAD_DIM) * (DA_HEAD_DIM ** -0.5)
    k = k.reshape(B, S, DA_HEADS, 2, DA_HEAD_DIM)
    v = v.reshape(B, S, DA_HEADS, 2 * DA_HEAD_DIM)
    f32 = jnp.float32
    lam = (jnp.exp(jnp.sum(lq1.astype(f32) * lk1.astype(f32)))
           - jnp.exp(jnp.sum(lq2.astype(f32) * lk2.astype(f32))) + lam_init)
    k_pos = jnp.arange(S)
    q_blocks = jnp.moveaxis(q.reshape(B, nblk, Q_BLOCK, DA_HEADS, 2, DA_HEAD_DIM), 1, 0)

    def block(args):
        qb, start = args
        q_pos = start + jnp.arange(Q_BLOCK)
        dist = q_pos[:, None] - k_pos[None, :]
        bias = rel_bias[t5_bucket(jnp.maximum(dist, 0))]
        bias = bias.reshape(Q_BLOCK, S, DA_HEADS, 2).transpose(2, 3, 0, 1).astype(f32)
        logits = jnp.einsum('bqhmd,bkhmd->bhmqk', qb, k).astype(f32) + bias
        logits = jnp.where(dist >= 0, logits, NEG_INF)
        p = jax.nn.softmax(logits, axis=-1)
        attn = p[:, :, 0] - lam * p[:, :, 1]
        return jnp.einsum('bhqk,bkhe->bqhe', attn.astype(v.dtype), v)

    starts = jnp.arange(nblk) * Q_BLOCK
    o = lax.map(block, (q_blocks, starts))
    o = jnp.moveaxis(o, 0, 1).reshape(B, S, DA_HEADS, 2 * DA_HEAD_DIM)
    o = rms_norm(o, subln_g, SUBLN_EPS) * (1.0 - lam_init)
    return o.reshape(B, S, DA_WIDTH)


def block_diag_linear(x, w, b):
    B, S, _ = x.shape
    xb = x.reshape(B, S, LRU_BLOCKS, LRU_BLOCK)
    y = jnp.einsum('bsni,nij->bsnj', xb, w) + b
    return y.reshape(B, S, LRU_WIDTH)


def rg_lru(x, wa, ba, wx, bx, lam):
    f32 = jnp.float32
    r = jax.nn.sigmoid(block_diag_linear(x, wa, ba)).astype(f32)
    i = jax.nn.sigmoid(block_diag_linear(x, wx, bx)).astype(f32)
    log_a = -LRU_C * r * jax.nn.softplus(-lam.astype(f32))
    a = jnp.exp(log_a)
    mult = jnp.sqrt(-jnp.expm1(2.0 * log_a))
    b = mult * (i * x.astype(f32))

    def combine(e, l):
        return (e[0] * l[0], l[0] * e[1] + l[1])

    _, h = lax.associative_scan(combine, (a, b), axis=1)
    return h.astype(x.dtype)


def setup_inputs(seed: int = 0) -> dict:
    key = jax.random.key(seed)
    ks = iter(jax.random.split(key, 40))
    f32 = jnp.float32

    def nrm(shape, fan_in):
        return jax.random.normal(next(ks), shape, f32) * fan_in ** -0.5

    def gain(shape):
        return 1.0 + 0.02 * jax.random.normal(next(ks), shape, f32)

    def small(shape, s):
        return s * jax.random.normal(next(ks), shape, f32)

    u = jax.random.uniform(next(ks), (DEPTH, LRU_WIDTH), f32, minval=0.9, maxval=0.999)
    a0 = u ** (1.0 / LRU_C)
    lru_lambda = jnp.log(a0) - jnp.log1p(-a0)

    return {
        "x": jax.random.normal(next(ks), (BATCH, SEQ, D_MODEL), f32),
        "rel_bias": small((NUM_BUCKETS, 2 * DA_HEADS), 0.5),
        "ffn1_norm": gain((DEPTH, D_MODEL)),
        "ffn1_gate": nrm((DEPTH, D_MODEL, D_FF), D_MODEL),
        "ffn1_up": nrm((DEPTH, D_MODEL, D_FF), D_MODEL),
        "ffn1_down": nrm((DEPTH, D_FF, D_MODEL), D_FF),
        "mix_norm": gain((DEPTH, D_MODEL)),
        "w_in": nrm((DEPTH, D_MODEL, IN_WIDTH), D_MODEL),
        "w_out": nrm((DEPTH, MIX_WIDTH, D_MODEL), MIX_WIDTH),
        "lam_q1": small((DEPTH, DA_HEAD_DIM), 0.1),
        "lam_k1": small((DEPTH, DA_HEAD_DIM), 0.1),
        "lam_q2": small((DEPTH, DA_HEAD_DIM), 0.1),
        "lam_k2": small((DEPTH, DA_HEAD_DIM), 0.1),
        "subln_gain": gain((DEPTH, 2 * DA_HEAD_DIM)),
        "sc_conv_w": nrm((DEPTH, SC_KERNEL, SC_WIDTH), SC_KERNEL),
        "lru_conv_w": nrm((DEPTH, LRU_CONV, LRU_WIDTH), LRU_CONV),
        "lru_conv_b": small((DEPTH, LRU_WIDTH), 0.01),
        "lru_wa": nrm((DEPTH, LRU_BLOCKS, LRU_BLOCK, LRU_BLOCK), LRU_BLOCK),
        "lru_ba": small((DEPTH, LRU_BLOCKS, LRU_BLOCK), 0.01),
        "lru_wx": nrm((DEPTH, LRU_BLOCKS, LRU_BLOCK, LRU_BLOCK), LRU_BLOCK),
        "lru_bx": small((DEPTH, LRU_BLOCKS, LRU_BLOCK), 0.01),
        "lru_lambda": lru_lambda,
        "ffn2_norm": gain((DEPTH, D_MODEL)),
        "ffn2_gate": nrm((DEPTH, D_MODEL, D_FF), D_MODEL),
        "ffn2_up": nrm((DEPTH, D_MODEL, D_FF), D_MODEL),
        "ffn2_down": nrm((DEPTH, D_FF, D_MODEL), D_FF),
        "final_norm": gain((D_MODEL,)),
    }


def reference(x, rel_bias, ffn1_norm, ffn1_gate, ffn1_up, ffn1_down, mix_norm, w_in, w_out,
              lam_q1, lam_k1, lam_q2, lam_k2, subln_gain, sc_conv_w, lru_conv_w, lru_conv_b,
              lru_wa, lru_ba, lru_wx, lru_bx, lru_lambda, ffn2_norm, ffn2_gate, ffn2_up,
              ffn2_down, final_norm):
    for l in range(DEPTH):
        h = x + 0.5 * swiglu(rms_norm(x, ffn1_norm[l]), ffn1_gate[l], ffn1_up[l], ffn1_down[l])

        u = rms_norm(h, mix_norm[l])
        z = u @ w_in[l]
        q, k, v, sc_b, sc_c, sc_x, lru_x, lru_g = jnp.split(z, IN_SPLIT_POINTS, axis=-1)

        lam_init = 0.8 - 0.6 * math.exp(-0.3 * l)
        y_da = diff_attention(q, k, v, lam_q1[l], lam_k1[l], lam_q2[l], lam_k2[l],
                              subln_gain[l], rel_bias, lam_init)
        y_sc = sc_b * causal_dwconv(sc_c * sc_x, sc_conv_w[l])
        xr = causal_dwconv(lru_x, lru_conv_w[l]) + lru_conv_b[l]
        y_lru = jax.nn.gelu(lru_g) * rg_lru(xr, lru_wa[l], lru_ba[l], lru_wx[l], lru_bx[l],
                                            lru_lambda[l])

        h = h + jnp.concatenate([y_da, y_sc, y_lru], axis=-1) @ w_out[l]

        x = h + 0.5 * swiglu(rms_norm(h, ffn2_norm[l]), ffn2_gate[l], ffn2_up[l], ffn2_down[l])
    return rms_norm(x, final_norm)
```

```python
import functools
import math

import jax
import jax.numpy as jnp
from jax import lax
from jax.experimental import pallas as pl
from jax.experimental.pallas import tpu as pltpu

F32 = jnp.float32
BF16 = jnp.bfloat16

D_MODEL = 1024
D_FF = 2816
DA_WIDTH = 512
SC_WIDTH = 256
LRU_WIDTH = 256
DA_HEAD_DIM = 64
DA_HEADS = 4
DA_VDIM = 2 * DA_HEAD_DIM
NUM_BUCKETS = 32
MAX_DISTANCE = 128
SUBLN_EPS = 1e-5
SC_KERNEL = 3
LRU_BLOCKS = 4
LRU_BLOCK = 64
LRU_CONV = 4
LRU_C = 8.0
RMS_EPS = 1e-6
NEG_INF = -1e30
QKV_WIDTH = 3 * DA_WIDTH
REST_WIDTH = 3 * SC_WIDTH + 2 * LRU_WIDTH
IN_WIDTH = QKV_WIDTH + REST_WIDTH

LANES = 128
SUBLANES = 8
VMEM_LIMIT_BYTES = 56 * 1024 * 1024
ROW_TILE = 256
ATTN_TILE = 256
MIX_TILE = 256


def _rms(x, g, eps):
    return x * lax.rsqrt(jnp.mean(x * x, axis=-1, keepdims=True) + eps) * g


def _swiglu(xn, wg_ref, wu_ref, wd_ref):
    g = jnp.dot(xn, wg_ref[...], preferred_element_type=F32)
    u = jnp.dot(xn, wu_ref[...], preferred_element_type=F32)
    a = (g * jax.nn.sigmoid(g) * u).astype(BF16)
    return jnp.dot(a, wd_ref[...], preferred_element_type=F32)


def _ffn_in_kernel(x_ref, g1_ref, wg_ref, wu_ref, wd_ref, gm_ref, wq_ref, wr_ref,
                   h_ref, zq_ref, zr_ref):
    x = x_ref[...]
    xn = _rms(x, g1_ref[...], RMS_EPS).astype(BF16)
    h = x + 0.5 * _swiglu(xn, wg_ref, wu_ref, wd_ref)
    h_ref[...] = h
    u = _rms(h, gm_ref[...], RMS_EPS).astype(BF16)
    zq_ref[...] = jnp.dot(u, wq_ref[...], preferred_element_type=F32).astype(BF16)
    zr_ref[...] = jnp.dot(u, wr_ref[...], preferred_element_type=F32)


def _resident(shape, index):
    return pl.BlockSpec(shape, index, pipeline_mode=pl.Buffered(1))


def _ffn_in(x2, g1, wg, wu, wd, gm, wq, wr, l):
    n = x2.shape[0]
    row = lambda w: pl.BlockSpec((ROW_TILE, w), lambda i: (i, 0))
    vec = pl.BlockSpec((None, 1, D_MODEL), lambda i: (l, 0, 0))
    return pl.pallas_call(
        _ffn_in_kernel,
        grid=(n // ROW_TILE,),
        in_specs=[
            row(D_MODEL), vec,
            _resident((None, D_MODEL, D_FF), lambda i: (l, 0, 0)),
            _resident((None, D_MODEL, D_FF), lambda i: (l, 0, 0)),
            _resident((None, D_FF, D_MODEL), lambda i: (l, 0, 0)),
            vec,
            _resident((None, D_MODEL, QKV_WIDTH), lambda i: (l, 0, 0)),
            _resident((None, D_MODEL, REST_WIDTH), lambda i: (l, 0, 0)),
        ],
        out_specs=[row(D_MODEL), row(QKV_WIDTH), row(REST_WIDTH)],
        out_shape=[
            jax.ShapeDtypeStruct((n, D_MODEL), F32),
            jax.ShapeDtypeStruct((n, QKV_WIDTH), BF16),
            jax.ShapeDtypeStruct((n, REST_WIDTH), F32),
        ],
        compiler_params=pltpu.CompilerParams(
            dimension_semantics=("arbitrary",), vmem_limit_bytes=VMEM_LIMIT_BYTES),
        name="ffn_in",
    )(x2, g1, wg, wu, wd, gm, wq, wr)


def _out_ffn_kernel(h_ref, ya_ref, ym_ref, woa_ref, wom_ref, g2_ref, wg_ref, wu_ref, wd_ref,
                    gf_ref, o_ref, *, final_norm):
    h = (h_ref[...]
         + jnp.dot(ya_ref[...], woa_ref[...], preferred_element_type=F32)
         + jnp.dot(ym_ref[...], wom_ref[...], preferred_element_type=F32))
    hn = _rms(h, g2_ref[...], RMS_EPS).astype(BF16)
    x = h + 0.5 * _swiglu(hn, wg_ref, wu_ref, wd_ref)
    if final_norm:
        x = _rms(x, gf_ref[...], RMS_EPS)
    o_ref[...] = x


def _out_ffn(h2, ya, ym, woa, wom, g2, wg, wu, wd, gf, l, final_norm):
    n = h2.shape[0]
    row = lambda w: pl.BlockSpec((ROW_TILE, w), lambda i: (i, 0))
    mix_half = SC_WIDTH + LRU_WIDTH
    return pl.pallas_call(
        functools.partial(_out_ffn_kernel, final_norm=final_norm),
        grid=(n // ROW_TILE,),
        in_specs=[
            row(D_MODEL), row(DA_WIDTH), row(mix_half),
            _resident((None, DA_WIDTH, D_MODEL), lambda i: (l, 0, 0)),
            _resident((None, mix_half, D_MODEL), lambda i: (l, 0, 0)),
            pl.BlockSpec((None, 1, D_MODEL), lambda i: (l, 0, 0)),
            _resident((None, D_MODEL, D_FF), lambda i: (l, 0, 0)),
            _resident((None, D_MODEL, D_FF), lambda i: (l, 0, 0)),
            _resident((None, D_FF, D_MODEL), lambda i: (l, 0, 0)),
            pl.BlockSpec((1, D_MODEL), lambda i: (0, 0)),
        ],
        out_specs=row(D_MODEL),
        out_shape=jax.ShapeDtypeStruct((n, D_MODEL), F32),
        compiler_params=pltpu.CompilerParams(
            dimension_semantics=("arbitrary",), vmem_limit_bytes=VMEM_LIMIT_BYTES),
        name="out_ffn",
    )(h2, ya, ym, woa, wom, g2, wg, wu, wd, gf)


def _bias_kernel(rb_ref, o_ref):
    hm = pl.program_id(0)
    t = ATTN_TILE
    max_exact = NUM_BUCKETS // 2
    far = rb_ref[NUM_BUCKETS - 1, hm]
    r = lax.broadcasted_iota(jnp.int32, (t, t), 0)
    c = lax.broadcasted_iota(jnp.int32, (t, t), 1)
    for off in range(2):
        dist = r - c + off * t
        n = jnp.maximum(dist, 0)
        nf = jnp.maximum(n, 1).astype(F32)
        large = max_exact + (jnp.log(nf / max_exact) / math.log(MAX_DISTANCE / max_exact)
                             * (NUM_BUCKETS - max_exact)).astype(jnp.int32)
        large = jnp.minimum(large, NUM_BUCKETS - 1)
        bucket = jnp.where(n < max_exact, n, large)
        val = jnp.zeros((t, t), F32)
        for j in range(NUM_BUCKETS):
            val = jnp.where(bucket == j, rb_ref[j, hm], val)
        o_ref[off] = jnp.where(dist >= 0, val - far, NEG_INF)


def _bias_tiles(rel_bias):
    t = ATTN_TILE
    return pl.pallas_call(
        _bias_kernel,
        grid=(2 * DA_HEADS,),
        in_specs=[pl.BlockSpec(memory_space=pltpu.SMEM)],
        out_specs=pl.BlockSpec((None, 2, t, t), lambda i: (i, 0, 0, 0)),
        out_shape=jax.ShapeDtypeStruct((2 * DA_HEADS, 2, t, t), F32),
        compiler_params=pltpu.CompilerParams(dimension_semantics=("arbitrary",)),
        name="bias_tiles",
    )(rel_bias)


def _attn_kernel(lam_ref, g_ref, bias_ref, q_ref, k_ref, v_ref, o_ref, m_ref, l_ref, acc_ref,
                 *, lam_init):
    t = ATTN_TILE
    qi = pl.program_id(2)
    lane = lax.broadcasted_iota(jnp.int32, (t, LANES), 1)
    q = q_ref[...] * (DA_HEAD_DIM ** -0.5)
    qm = (jnp.where(lane < DA_HEAD_DIM, q, 0).astype(BF16),
          jnp.where(lane >= DA_HEAD_DIM, q, 0).astype(BF16))

    m_ref[...] = jnp.full(m_ref.shape, NEG_INF, F32)
    l_ref[...] = jnp.zeros(l_ref.shape, F32)
    acc_ref[...] = jnp.zeros(acc_ref.shape, F32)

    def step(kj, off):
        start = pl.multiple_of(kj * t, t)
        k = k_ref[pl.ds(start, t), :]
        v = v_ref[pl.ds(start, t), :]
        for mp in range(2):
            s = lax.dot_general(qm[mp], k, (((1,), (1,)), ((), ())),
                                preferred_element_type=F32)
            if off is not None:
                s = s + bias_ref[mp, off]
            m_old = m_ref[mp]
            m_new = jnp.maximum(m_old, jnp.max(s, axis=1, keepdims=True))
            alpha = jnp.exp(m_old - m_new)
            p = jnp.exp(s - m_new)
            l_ref[mp] = alpha * l_ref[mp] + jnp.sum(p, axis=1, keepdims=True)
            acc_ref[mp] = alpha * acc_ref[mp] + jnp.dot(p.astype(BF16), v,
                                                        preferred_element_type=F32)
            m_ref[mp] = m_new

    def far_step(kj, carry):
        step(kj, None)
        return carry

    lax.fori_loop(0, jnp.maximum(qi - 1, 0), far_step, 0)

    @pl.when(qi >= 1)
    def _():
        step(qi - 1, 1)

    step(qi, 0)

    lp = lam_ref[...]
    lam = (jnp.exp(jnp.sum(lp[0:1] * lp[1:2], axis=1, keepdims=True))
           - jnp.exp(jnp.sum(lp[2:3] * lp[3:4], axis=1, keepdims=True)) + lam_init)
    o = acc_ref[0] / l_ref[0] - lam * (acc_ref[1] / l_ref[1])
    o_ref[...] = (_rms(o, g_ref[...], SUBLN_EPS) * (1.0 - lam_init)).astype(o_ref.dtype)


def _attention(zq3, lam_params, subln_g, bias, l, lam_init):
    b, s, _ = zq3.shape
    t = ATTN_TILE
    return pl.pallas_call(
        functools.partial(_attn_kernel, lam_init=lam_init),
        grid=(b, DA_HEADS, s // t),
        in_specs=[
            pl.BlockSpec((None, 4, DA_HEAD_DIM), lambda bi, h, qi: (l, 0, 0)),
            pl.BlockSpec((None, 1, DA_VDIM), lambda bi, h, qi: (l, 0, 0)),
            pl.BlockSpec((2, 2, t, t), lambda bi, h, qi: (h, 0, 0, 0)),
            pl.BlockSpec((None, t, LANES), lambda bi, h, qi: (bi, qi, h)),
            pl.BlockSpec((None, s, LANES), lambda bi, h, qi: (bi, 0, DA_HEADS + h)),
            pl.BlockSpec((None, s, LANES), lambda bi, h, qi: (bi, 0, 2 * DA_HEADS + h)),
        ],
        out_specs=pl.BlockSpec((None, t, LANES), lambda bi, h, qi: (bi, qi, h)),
        out_shape=jax.ShapeDtypeStruct((b, s, DA_WIDTH), BF16),
        scratch_shapes=[
            pltpu.VMEM((2, t, 1), F32),
            pltpu.VMEM((2, t, 1), F32),
            pltpu.VMEM((2, t, DA_VDIM), F32),
        ],
        compiler_params=pltpu.CompilerParams(
            dimension_semantics=("arbitrary", "arbitrary", "arbitrary"),
            vmem_limit_bytes=VMEM_LIMIT_BYTES),
        name="diff_attn",
    )(lam_params, subln_g, bias, zq3, zq3, zq3)


def _shift_rows(x, halo, j):
    r = pltpu.roll(x, j, 0)
    head_row = lax.broadcasted_iota(jnp.int32, halo.shape, 0)
    head = jnp.where(head_row < j, pltpu.roll(halo, j, 0), r[:SUBLANES])
    return jnp.concatenate([head, r[SUBLANES:]], axis=0)


def _gelu_tanh(x):
    return 0.5 * x * (1.0 + jnp.tanh(math.sqrt(2.0 / math.pi) * (x + 0.044715 * (x * x * x))))


def _mix_kernel(z_ref, scw_ref, lcw_ref, lcb_ref, wa_ref, ba_ref, wx_ref, bx_ref, lam_ref,
                o_ref, hc_ref, halo_sc_ref, halo_lx_ref):
    tc = MIX_TILE
    w = SC_WIDTH

    @pl.when(pl.program_id(1) == 0)
    def _():
        hc_ref[...] = jnp.zeros(hc_ref.shape, F32)
        halo_sc_ref[...] = jnp.zeros(halo_sc_ref.shape, F32)
        halo_lx_ref[...] = jnp.zeros(halo_lx_ref.shape, F32)

    row = lax.broadcasted_iota(jnp.int32, (tc, w), 0)

    sc_b = z_ref[:, 0:w]
    cx = z_ref[:, w:2 * w] * z_ref[:, 2 * w:3 * w]
    halo = halo_sc_ref[...]
    conv = scw_ref[SC_KERNEL - 1:SC_KERNEL, :] * cx
    for j in range(1, SC_KERNEL):
        conv = conv + scw_ref[SC_KERNEL - 1 - j:SC_KERNEL - j, :] * _shift_rows(cx, halo, j)
    halo_sc_ref[...] = cx[tc - SUBLANES:, :]
    o_ref[:, 0:w] = (sc_b * conv).astype(o_ref.dtype)

    lx = z_ref[:, 3 * w:4 * w]
    lg = z_ref[:, 4 * w:5 * w]
    halo = halo_lx_ref[...]
    xr = lcw_ref[LRU_CONV - 1:LRU_CONV, :] * lx + lcb_ref[...]
    for j in range(1, LRU_CONV):
        xr = xr + lcw_ref[LRU_CONV - 1 - j:LRU_CONV - j, :] * _shift_rows(lx, halo, j)
    halo_lx_ref[...] = lx[tc - SUBLANES:, :]

    xb = xr.astype(BF16)
    r = jax.nn.sigmoid(jnp.dot(xb, wa_ref[...], preferred_element_type=F32) + ba_ref[...])
    i = jax.nn.sigmoid(jnp.dot(xb, wx_ref[...], preferred_element_type=F32) + bx_ref[...])
    nl = -lam_ref[...]
    softplus = jnp.maximum(nl, 0.0) + jnp.log1p(jnp.exp(-jnp.abs(nl)))
    log_a = (-LRU_C) * r * softplus
    a = jnp.exp(log_a)
    bv = jnp.sqrt(-jnp.tanh(log_a) * (a * a + 1.0)) * (i * xr)

    s = 1
    while s < tc:
        keep = row >= s
        a_sh = jnp.where(keep, pltpu.roll(a, s, 0), 1.0)
        b_sh = jnp.where(keep, pltpu.roll(bv, s, 0), 0.0)
        bv = a * b_sh + bv
        a = a * a_sh
        s *= 2
    h = bv + a * hc_ref[0:1, :]
    hc_ref[0:1, :] = h[tc - 1:tc, :]
    o_ref[:, w:2 * w] = (_gelu_tanh(lg) * h).astype(o_ref.dtype)


def _mix(zr3, scw, lcw, lcb, wa, ba, wx, bx, lam, l):
    b, s, _ = zr3.shape
    w = SC_WIDTH
    par = lambda rows: pl.BlockSpec((None, rows, w), lambda bi, ti: (l, 0, 0))
    return pl.pallas_call(
        _mix_kernel,
        grid=(b, s // MIX_TILE),
        in_specs=[
            pl.BlockSpec((None, MIX_TILE, REST_WIDTH), lambda bi, ti: (bi, ti, 0)),
            par(SC_KERNEL), par(LRU_CONV), par(1), par(w), par(1), par(w), par(1), par(1),
        ],
        out_specs=pl.BlockSpec((None, MIX_TILE, 2 * w), lambda bi, ti: (bi, ti, 0)),
        out_shape=jax.ShapeDtypeStruct((b, s, 2 * w), BF16),
        scratch_shapes=[pltpu.VMEM((SUBLANES, w), F32)] * 3,
        compiler_params=pltpu.CompilerParams(
            dimension_semantics=("arbitrary", "arbitrary"),
            vmem_limit_bytes=VMEM_LIMIT_BYTES),
        name="conv_lru",
    )(zr3, scw, lcw, lcb, wa, ba, wx, bx, lam)


def _block_diag(w):
    depth, nb, blk, _ = w.shape
    eye = jnp.eye(nb, dtype=w.dtype)
    return jnp.einsum('lnij,nm->lnimj', w, eye).reshape(depth, nb * blk, nb * blk)


def kernel(x, rel_bias, ffn1_norm, ffn1_gate, ffn1_up, ffn1_down, mix_norm, w_in, w_out, lam_q1, lam_k1, lam_q2, lam_k2, subln_gain, sc_conv_w, lru_conv_w, lru_conv_b, lru_wa, lru_ba, lru_wx, lru_bx, lru_lambda, ffn2_norm, ffn2_gate, ffn2_up, ffn2_down, final_norm):
    b, s, d = x.shape
    depth = w_in.shape[0]
    assert d == D_MODEL and s % ATTN_TILE == 0 and s % MIX_TILE == 0 and (b * s) % ROW_TILE == 0

    bf = lambda w: w.astype(BF16)
    vec = lambda v: v.reshape(depth, 1, v.shape[-1])
    wg1, wu1, wd1 = bf(ffn1_gate), bf(ffn1_up), bf(ffn1_down)
    wg2, wu2, wd2 = bf(ffn2_gate), bf(ffn2_up), bf(ffn2_down)
    wq, wr = bf(w_in[:, :, :QKV_WIDTH]), bf(w_in[:, :, QKV_WIDTH:])
    woa, wom = bf(w_out[:, :DA_WIDTH]), bf(w_out[:, DA_WIDTH:])
    wa, wx = bf(_block_diag(lru_wa)), bf(_block_diag(lru_wx))
    lam_params = jnp.stack([lam_q1, lam_k1, lam_q2, lam_k2], axis=1)
    g1, gm, g2 = vec(ffn1_norm), vec(mix_norm), vec(ffn2_norm)
    gs, lcb, lam = vec(subln_gain), vec(lru_conv_b), vec(lru_lambda)
    ba, bx = vec(lru_ba.reshape(depth, -1)), vec(lru_bx.reshape(depth, -1))
    gf = final_norm.reshape(1, d)

    bias = _bias_tiles(rel_bias)

    x2 = x.reshape(b * s, d)
    for l in range(depth):
        lam_init = 0.8 - 0.6 * math.exp(-0.3 * l)
        h2, zq, zr = _ffn_in(x2, g1, wg1, wu1, wd1, gm, wq, wr, l)
        ya = _attention(zq.reshape(b, s, QKV_WIDTH), lam_params, gs, bias, l, lam_init)
        ym = _mix(zr.reshape(b, s, REST_WIDTH), sc_conv_w, lru_conv_w, lcb, wa, ba, wx, bx, lam, l)
        x2 = _out_ffn(h2, ya.reshape(b * s, DA_WIDTH), ym.reshape(b * s, 2 * SC_WIDTH),
                      woa, wom, g2, wg2, wu2, wd2, gf, l, l == depth - 1)
    return x2.reshape(b, s, d)
```

```python
import functools
import math

import jax
import jax.numpy as jnp
from jax import lax
from jax.experimental import pallas as pl
from jax.experimental.pallas import tpu as pltpu

F32 = jnp.float32
BF16 = jnp.bfloat16

D_MODEL = 1024
D_FF = 2816
DA_WIDTH = 512
SC_WIDTH = 256
LRU_WIDTH = 256
DA_HEAD_DIM = 64
DA_HEADS = 4
DA_VDIM = 2 * DA_HEAD_DIM
NUM_BUCKETS = 32
MAX_DISTANCE = 128
SUBLN_EPS = 1e-5
SC_KERNEL = 3
LRU_BLOCKS = 4
LRU_BLOCK = 64
LRU_CONV = 4
LRU_C = 8.0
RMS_EPS = 1e-6
NEG_INF = -1e30
QKV_WIDTH = 3 * DA_WIDTH
REST_WIDTH = 3 * SC_WIDTH + 2 * LRU_WIDTH
IN_WIDTH = QKV_WIDTH + REST_WIDTH

LANES = 128
SUBLANES = 8
VMEM_LIMIT_BYTES = 56 * 1024 * 1024
ROW_TILE = 256
ATTN_TILE = 512
MIX_TILE = 256


def _rms(x, g, eps):
    return x * lax.rsqrt(jnp.mean(x * x, axis=-1, keepdims=True) + eps) * g


def _swiglu(xn, wg_ref, wu_ref, wd_ref):
    g = jnp.dot(xn, wg_ref[...], preferred_element_type=F32)
    u = jnp.dot(xn, wu_ref[...], preferred_element_type=F32)
    a = (g * jax.nn.sigmoid(g) * u).astype(BF16)
    return jnp.dot(a, wd_ref[...], preferred_element_type=F32)


def _ffn_in_kernel(x_ref, g1_ref, wg_ref, wu_ref, wd_ref, gm_ref, wq_ref, wr_ref,
                   h_ref, zq_ref, zr_ref):
    x = x_ref[...]
    xn = _rms(x, g1_ref[...], RMS_EPS).astype(BF16)
    h = x + 0.5 * _swiglu(xn, wg_ref, wu_ref, wd_ref)
    h_ref[...] = h
    u = _rms(h, gm_ref[...], RMS_EPS).astype(BF16)
    zq_ref[...] = jnp.dot(u, wq_ref[...], preferred_element_type=F32).astype(BF16)
    zr_ref[...] = jnp.dot(u, wr_ref[...], preferred_element_type=F32)


def _resident(shape, index):
    return pl.BlockSpec(shape, index, pipeline_mode=pl.Buffered(1))


def _ffn_in(x2, g1, wg, wu, wd, gm, wq, wr, l):
    n = x2.shape[0]
    row = lambda w: pl.BlockSpec((ROW_TILE, w), lambda i: (i, 0))
    vec = pl.BlockSpec((None, 1, D_MODEL), lambda i: (l, 0, 0))
    return pl.pallas_call(
        _ffn_in_kernel,
        grid=(n // ROW_TILE,),
        in_specs=[
            row(D_MODEL), vec,
            _resident((None, D_MODEL, D_FF), lambda i: (l, 0, 0)),
            _resident((None, D_MODEL, D_FF), lambda i: (l, 0, 0)),
            _resident((None, D_FF, D_MODEL), lambda i: (l, 0, 0)),
            vec,
            _resident((None, D_MODEL, QKV_WIDTH), lambda i: (l, 0, 0)),
            _resident((None, D_MODEL, REST_WIDTH), lambda i: (l, 0, 0)),
        ],
        out_specs=[row(D_MODEL), row(QKV_WIDTH), row(REST_WIDTH)],
        out_shape=[
            jax.ShapeDtypeStruct((n, D_MODEL), F32),
            jax.ShapeDtypeStruct((n, QKV_WIDTH), BF16),
            jax.ShapeDtypeStruct((n, REST_WIDTH), F32),
        ],
        compiler_params=pltpu.CompilerParams(
            dimension_semantics=("arbitrary",), vmem_limit_bytes=VMEM_LIMIT_BYTES),
        name="ffn_in",
    )(x2, g1, wg, wu, wd, gm, wq, wr)


def _out_ffn_kernel(h_ref, ya_ref, ym_ref, woa_ref, wom_ref, g2_ref, wg_ref, wu_ref, wd_ref,
                    gf_ref, o_ref, *, final_norm):
    h = (h_ref[...]
         + jnp.dot(ya_ref[...], woa_ref[...], preferred_element_type=F32)
         + jnp.dot(ym_ref[...], wom_ref[...], preferred_element_type=F32))
    hn = _rms(h, g2_ref[...], RMS_EPS).astype(BF16)
    x = h + 0.5 * _swiglu(hn, wg_ref, wu_ref, wd_ref)
    if final_norm:
        x = _rms(x, gf_ref[...], RMS_EPS)
    o_ref[...] = x


def _out_ffn(h2, ya, ym, woa, wom, g2, wg, wu, wd, gf, l, final_norm):
    n = h2.shape[0]
    row = lambda w: pl.BlockSpec((ROW_TILE, w), lambda i: (i, 0))
    mix_half = SC_WIDTH + LRU_WIDTH
    return pl.pallas_call(
        functools.partial(_out_ffn_kernel, final_norm=final_norm),
        grid=(n // ROW_TILE,),
        in_specs=[
            row(D_MODEL), row(DA_WIDTH), row(mix_half),
            _resident((None, DA_WIDTH, D_MODEL), lambda i: (l, 0, 0)),
            _resident((None, mix_half, D_MODEL), lambda i: (l, 0, 0)),
            pl.BlockSpec((None, 1, D_MODEL), lambda i: (l, 0, 0)),
            _resident((None, D_MODEL, D_FF), lambda i: (l, 0, 0)),
            _resident((None, D_MODEL, D_FF), lambda i: (l, 0, 0)),
            _resident((None, D_FF, D_MODEL), lambda i: (l, 0, 0)),
            pl.BlockSpec((1, D_MODEL), lambda i: (0, 0)),
        ],
        out_specs=row(D_MODEL),
        out_shape=jax.ShapeDtypeStruct((n, D_MODEL), F32),
        compiler_params=pltpu.CompilerParams(
            dimension_semantics=("arbitrary",), vmem_limit_bytes=VMEM_LIMIT_BYTES),
        name="out_ffn",
    )(h2, ya, ym, woa, wom, g2, wg, wu, wd, gf)


def _bias_kernel(rb_ref, o_ref):
    h = pl.program_id(0)
    t = ATTN_TILE
    max_exact = NUM_BUCKETS // 2
    key = lax.broadcasted_iota(jnp.int32, (t, t), 0)
    qry = lax.broadcasted_iota(jnp.int32, (t, t), 1)
    for off in range(2):
        dist = qry - key + off * t
        n = jnp.maximum(dist, 0)
        nf = jnp.maximum(n, 1).astype(F32)
        large = max_exact + (jnp.log(nf / max_exact) / math.log(MAX_DISTANCE / max_exact)
                             * (NUM_BUCKETS - max_exact)).astype(jnp.int32)
        large = jnp.minimum(large, NUM_BUCKETS - 1)
        bucket = jnp.where(n < max_exact, n, large)
        for mp in range(2):
            hm = 2 * h + mp
            val = jnp.zeros((t, t), F32)
            for j in range(NUM_BUCKETS):
                val = jnp.where(bucket == j, rb_ref[j, hm], val)
            far = rb_ref[NUM_BUCKETS - 1, hm]
            o_ref[off, :, mp * t:(mp + 1) * t] = jnp.where(dist >= 0, val - far, NEG_INF)


def _bias_tiles(rel_bias):
    t = ATTN_TILE
    return pl.pallas_call(
        _bias_kernel,
        grid=(DA_HEADS,),
        in_specs=[pl.BlockSpec(memory_space=pltpu.SMEM)],
        out_specs=pl.BlockSpec((None, 2, t, 2 * t), lambda i: (i, 0, 0, 0)),
        out_shape=jax.ShapeDtypeStruct((DA_HEADS, 2, t, 2 * t), F32),
        compiler_params=pltpu.CompilerParams(dimension_semantics=("arbitrary",)),
        name="bias_tiles",
    )(rel_bias)


def _attn_kernel(lam_ref, g_ref, bias_ref, q_ref, k_ref, v_ref, o_ref,
                 qq_ref, m_ref, l_ref, acc_ref, *, lam_init):
    t = ATTN_TILE
    qi = pl.program_id(2)
    lane = lax.broadcasted_iota(jnp.int32, (t, LANES), 1)
    q = q_ref[...] * (DA_HEAD_DIM ** -0.5)
    zero = jnp.zeros_like(q)
    qq_ref[0:t, :] = jnp.where(lane < DA_HEAD_DIM, q, zero)
    qq_ref[t:2 * t, :] = jnp.where(lane >= DA_HEAD_DIM, q, zero)

    m_ref[...] = jnp.full(m_ref.shape, NEG_INF, F32)
    l_ref[...] = jnp.zeros(l_ref.shape, F32)
    acc_ref[...] = jnp.zeros(acc_ref.shape, F32)

    def step(kj, off):
        start = pl.multiple_of(kj * t, t)
        k = k_ref[pl.ds(start, t), :]
        v = v_ref[pl.ds(start, t), :]
        s = lax.dot_general(k, qq_ref[...], (((1,), (1,)), ((), ())),
                            preferred_element_type=F32)
        if off is not None:
            s = s + bias_ref[off]
        m_old = m_ref[...]
        m_new = jnp.maximum(m_old, jnp.max(s, axis=0, keepdims=True))
        alpha = jnp.exp(m_old - m_new)
        p = jnp.exp(s - m_new)
        l_ref[...] = alpha * l_ref[...] + jnp.sum(p, axis=0, keepdims=True)
        pv = lax.dot_general(v, p.astype(BF16), (((0,), (0,)), ((), ())),
                             preferred_element_type=F32)
        acc_ref[...] = alpha * acc_ref[...] + pv
        m_ref[...] = m_new

    def far_step(kj, carry):
        step(kj, None)
        return carry

    lax.fori_loop(0, jnp.maximum(qi - 1, 0), far_step, 0)

    @pl.when(qi >= 1)
    def _():
        step(qi - 1, 1)

    step(qi, 0)

    lp = lam_ref[...]
    lam = (jnp.exp(jnp.sum(lp[0:1] * lp[1:2], axis=1, keepdims=True))
           - jnp.exp(jnp.sum(lp[2:3] * lp[3:4], axis=1, keepdims=True)) + lam_init)
    on = acc_ref[...] / l_ref[...]
    o = (on[:, 0:t] - lam * on[:, t:2 * t]).T
    o_ref[...] = (_rms(o, g_ref[...], SUBLN_EPS) * (1.0 - lam_init)).astype(o_ref.dtype)


def _attention(zq3, lam_params, subln_g, bias, l, lam_init):
    b, s, _ = zq3.shape
    t = ATTN_TILE
    return pl.pallas_call(
        functools.partial(_attn_kernel, lam_init=lam_init),
        grid=(b, DA_HEADS, s // t),
        in_specs=[
            pl.BlockSpec((None, 4, DA_HEAD_DIM), lambda bi, h, qi: (l, 0, 0)),
            pl.BlockSpec((None, 1, DA_VDIM), lambda bi, h, qi: (l, 0, 0)),
            pl.BlockSpec((None, 2, t, 2 * t), lambda bi, h, qi: (h, 0, 0, 0)),
            pl.BlockSpec((None, t, LANES), lambda bi, h, qi: (bi, qi, h)),
            pl.BlockSpec((None, s, LANES), lambda bi, h, qi: (bi, 0, DA_HEADS + h)),
            pl.BlockSpec((None, s, LANES), lambda bi, h, qi: (bi, 0, 2 * DA_HEADS + h)),
        ],
        out_specs=pl.BlockSpec((None, t, LANES), lambda bi, h, qi: (bi, qi, h)),
        out_shape=jax.ShapeDtypeStruct((b, s, DA_WIDTH), BF16),
        scratch_shapes=[
            pltpu.VMEM((2 * t, LANES), BF16),
            pltpu.VMEM((1, 2 * t), F32),
            pltpu.VMEM((1, 2 * t), F32),
            pltpu.VMEM((DA_VDIM, 2 * t), F32),
        ],
        compiler_params=pltpu.CompilerParams(
            dimension_semantics=("arbitrary", "arbitrary", "arbitrary"),
            vmem_limit_bytes=VMEM_LIMIT_BYTES),
        name="diff_attn",
    )(lam_params, subln_g, bias, zq3, zq3, zq3)


def _shift_rows(x, halo, j):
    r = pltpu.roll(x, j, 0)
    head_row = lax.broadcasted_iota(jnp.int32, halo.shape, 0)
    head = jnp.where(head_row < j, pltpu.roll(halo, j, 0), r[:SUBLANES])
    return jnp.concatenate([head, r[SUBLANES:]], axis=0)


def _gelu_tanh(x):
    return 0.5 * x * (1.0 + jnp.tanh(math.sqrt(2.0 / math.pi) * (x + 0.044715 * (x * x * x))))


def _mix_kernel(z_ref, scw_ref, lcw_ref, lcb_ref, wa_ref, ba_ref, wx_ref, bx_ref, lam_ref,
                o_ref, hc_ref, halo_sc_ref, halo_lx_ref):
    tc = MIX_TILE
    w = SC_WIDTH

    @pl.when(pl.program_id(1) == 0)
    def _():
        hc_ref[...] = jnp.zeros(hc_ref.shape, F32)
        halo_sc_ref[...] = jnp.zeros(halo_sc_ref.shape, F32)
        halo_lx_ref[...] = jnp.zeros(halo_lx_ref.shape, F32)

    row = lax.broadcasted_iota(jnp.int32, (tc, w), 0)

    sc_b = z_ref[:, 0:w]
    cx = z_ref[:, w:2 * w] * z_ref[:, 2 * w:3 * w]
    halo = halo_sc_ref[...]
    conv = scw_ref[SC_KERNEL - 1:SC_KERNEL, :] * cx
    for j in range(1, SC_KERNEL):
        conv = conv + scw_ref[SC_KERNEL - 1 - j:SC_KERNEL - j, :] * _shift_rows(cx, halo, j)
    halo_sc_ref[...] = cx[tc - SUBLANES:, :]
    o_ref[:, 0:w] = (sc_b * conv).astype(o_ref.dtype)

    lx = z_ref[:, 3 * w:4 * w]
    lg = z_ref[:, 4 * w:5 * w]
    halo = halo_lx_ref[...]
    xr = lcw_ref[LRU_CONV - 1:LRU_CONV, :] * lx + lcb_ref[...]
    for j in range(1, LRU_CONV):
        xr = xr + lcw_ref[LRU_CONV - 1 - j:LRU_CONV - j, :] * _shift_rows(lx, halo, j)
    halo_lx_ref[...] = lx[tc - SUBLANES:, :]

    xb = xr.astype(BF16)
    r = jax.nn.sigmoid(jnp.dot(xb, wa_ref[...], preferred_element_type=F32) + ba_ref[...])
    i = jax.nn.sigmoid(jnp.dot(xb, wx_ref[...], preferred_element_type=F32) + bx_ref[...])
    nl = -lam_ref[...]
    softplus = jnp.maximum(nl, 0.0) + jnp.log1p(jnp.exp(-jnp.abs(nl)))
    log_a = (-LRU_C) * r * softplus
    a = jnp.exp(log_a)
    bv = jnp.sqrt(-jnp.tanh(log_a) * (a * a + 1.0)) * (i * xr)

    s = 1
    while s < tc:
        keep = row >= s
        a_sh = jnp.where(keep, pltpu.roll(a, s, 0), 1.0)
        b_sh = jnp.where(keep, pltpu.roll(bv, s, 0), 0.0)
        bv = a * b_sh + bv
        a = a * a_sh
        s *= 2
    h = bv + a * hc_ref[0:1, :]
    hc_ref[0:1, :] = h[tc - 1:tc, :]
    o_ref[:, w:2 * w] = (_gelu_tanh(lg) * h).astype(o_ref.dtype)


def _mix(zr3, scw, lcw, lcb, wa, ba, wx, bx, lam, l):
    b, s, _ = zr3.shape
    w = SC_WIDTH
    par = lambda rows: pl.BlockSpec((None, rows, w), lambda bi, ti: (l, 0, 0))
    return pl.pallas_call(
        _mix_kernel,
        grid=(b, s // MIX_TILE),
        in_specs=[
            pl.BlockSpec((None, MIX_TILE, REST_WIDTH), lambda bi, ti: (bi, ti, 0)),
            par(SC_KERNEL), par(LRU_CONV), par(1), par(w), par(1), par(w), par(1), par(1),
        ],
        out_specs=pl.BlockSpec((None, MIX_TILE, 2 * w), lambda bi, ti: (bi, ti, 0)),
        out_shape=jax.ShapeDtypeStruct((b, s, 2 * w), BF16),
        scratch_shapes=[pltpu.VMEM((SUBLANES, w), F32)] * 3,
        compiler_params=pltpu.CompilerParams(
            dimension_semantics=("arbitrary", "arbitrary"),
            vmem_limit_bytes=VMEM_LIMIT_BYTES),
        name="conv_lru",
    )(zr3, scw, lcw, lcb, wa, ba, wx, bx, lam)


def _block_diag(w):
    depth, nb, blk, _ = w.shape
    eye = jnp.eye(nb, dtype=w.dtype)
    return jnp.einsum('lnij,nm->lnimj', w, eye).reshape(depth, nb * blk, nb * blk)


def kernel(x, rel_bias, ffn1_norm, ffn1_gate, ffn1_up, ffn1_down, mix_norm, w_in, w_out, lam_q1, lam_k1, lam_q2, lam_k2, subln_gain, sc_conv_w, lru_conv_w, lru_conv_b, lru_wa, lru_ba, lru_wx, lru_bx, lru_lambda, ffn2_norm, ffn2_gate, ffn2_up, ffn2_down, final_norm):
    b, s, d = x.shape
    depth = w_in.shape[0]
    assert d == D_MODEL and s % ATTN_TILE == 0 and s % MIX_TILE == 0 and (b * s) % ROW_TILE == 0

    bf = lambda w: w.astype(BF16)
    vec = lambda v: v.reshape(depth, 1, v.shape[-1])
    wg1, wu1, wd1 = bf(ffn1_gate), bf(ffn1_up), bf(ffn1_down)
    wg2, wu2, wd2 = bf(ffn2_gate), bf(ffn2_up), bf(ffn2_down)
    wq, wr = bf(w_in[:, :, :QKV_WIDTH]), bf(w_in[:, :, QKV_WIDTH:])
    woa, wom = bf(w_out[:, :DA_WIDTH]), bf(w_out[:, DA_WIDTH:])
    wa, wx = bf(_block_diag(lru_wa)), bf(_block_diag(lru_wx))
    lam_params = jnp.stack([lam_q1, lam_k1, lam_q2, lam_k2], axis=1)
    g1, gm, g2 = vec(ffn1_norm), vec(mix_norm), vec(ffn2_norm)
    gs, lcb, lam = vec(subln_gain), vec(lru_conv_b), vec(lru_lambda)
    ba, bx = vec(lru_ba.reshape(depth, -1)), vec(lru_bx.reshape(depth, -1))
    gf = final_norm.reshape(1, d)

    bias = _bias_tiles(rel_bias)

    x2 = x.reshape(b * s, d)
    for l in range(depth):
        lam_init = 0.8 - 0.6 * math.exp(-0.3 * l)
        h2, zq, zr = _ffn_in(x2, g1, wg1, wu1, wd1, gm, wq, wr, l)
        ya = _attention(zq.reshape(b, s, QKV_WIDTH), lam_params, gs, bias, l, lam_init)
        ym = _mix(zr.reshape(b, s, REST_WIDTH), sc_conv_w, lru_conv_w, lcb, wa, ba, wx, bx, lam, l)
        x2 = _out_ffn(h2, ya.reshape(b * s, DA_WIDTH), ym.reshape(b * s, 2 * SC_WIDTH),
                      woa, wom, g2, wg2, wu2, wd2, gf, l, l == depth - 1)
    return x2.reshape(b, s, d)
```

```python
import functools
import math

import jax
import jax.numpy as jnp
from jax import lax
from jax.experimental import pallas as pl
from jax.experimental.pallas import tpu as pltpu

F32 = jnp.float32
BF16 = jnp.bfloat16

D_MODEL = 1024
D_FF = 2816
DA_WIDTH = 512
SC_WIDTH = 256
LRU_WIDTH = 256
DA_HEAD_DIM = 64
DA_HEADS = 4
DA_VDIM = 2 * DA_HEAD_DIM
NUM_BUCKETS = 32
MAX_DISTANCE = 128
SUBLN_EPS = 1e-5
SC_KERNEL = 3
LRU_BLOCKS = 4
LRU_BLOCK = 64
LRU_CONV = 4
LRU_C = 8.0
RMS_EPS = 1e-6
NEG_INF = -1e30
QKV_WIDTH = 3 * DA_WIDTH
REST_WIDTH = 3 * SC_WIDTH + 2 * LRU_WIDTH
IN_WIDTH = QKV_WIDTH + REST_WIDTH

LANES = 128
SUBLANES = 8
VMEM_LIMIT_BYTES = 56 * 1024 * 1024
ROW_TILE = 256
ATTN_TILE = 512
MIX_TILE = 256


def _rms(x, g, eps):
    return x * lax.rsqrt(jnp.mean(x * x, axis=-1, keepdims=True) + eps) * g


def _swiglu(xn, wg_ref, wu_ref, wd_ref):
    g = jnp.dot(xn, wg_ref[...], preferred_element_type=F32)
    u = jnp.dot(xn, wu_ref[...], preferred_element_type=F32)
    a = (g * jax.nn.sigmoid(g) * u).astype(BF16)
    return jnp.dot(a, wd_ref[...], preferred_element_type=F32)


def _ffn_in_kernel(x_ref, g1_ref, wg_ref, wu_ref, wd_ref, gm_ref, wq_ref, wr_ref,
                   h_ref, zq_ref, zr_ref):
    x = x_ref[...]
    xn = _rms(x, g1_ref[...], RMS_EPS).astype(BF16)
    h = x + 0.5 * _swiglu(xn, wg_ref, wu_ref, wd_ref)
    h_ref[...] = h
    u = _rms(h, gm_ref[...], RMS_EPS).astype(BF16)
    zq_ref[...] = jnp.dot(u, wq_ref[...], preferred_element_type=F32).astype(BF16)
    zr_ref[...] = jnp.dot(u, wr_ref[...], preferred_element_type=F32)


def _resident(shape, index):
    return pl.BlockSpec(shape, index, pipeline_mode=pl.Buffered(1))


def _ffn_in(x2, g1, wg, wu, wd, gm, wq, wr, l):
    n = x2.shape[0]
    row = lambda w: pl.BlockSpec((ROW_TILE, w), lambda i: (i, 0))
    vec = pl.BlockSpec((None, 1, D_MODEL), lambda i: (l, 0, 0))
    return pl.pallas_call(
        _ffn_in_kernel,
        grid=(n // ROW_TILE,),
        in_specs=[
            row(D_MODEL), vec,
            _resident((None, D_MODEL, D_FF), lambda i: (l, 0, 0)),
            _resident((None, D_MODEL, D_FF), lambda i: (l, 0, 0)),
            _resident((None, D_FF, D_MODEL), lambda i: (l, 0, 0)),
            vec,
            _resident((None, D_MODEL, QKV_WIDTH), lambda i: (l, 0, 0)),
            _resident((None, D_MODEL, REST_WIDTH), lambda i: (l, 0, 0)),
        ],
        out_specs=[row(D_MODEL), row(QKV_WIDTH), row(REST_WIDTH)],
        out_shape=[
            jax.ShapeDtypeStruct((n, D_MODEL), F32),
            jax.ShapeDtypeStruct((n, QKV_WIDTH), BF16),
            jax.ShapeDtypeStruct((n, REST_WIDTH), F32),
        ],
        compiler_params=pltpu.CompilerParams(
            dimension_semantics=("arbitrary",), vmem_limit_bytes=VMEM_LIMIT_BYTES),
        name="ffn_in",
    )(x2, g1, wg, wu, wd, gm, wq, wr)


def _out_ffn_kernel(h_ref, ya_ref, ym_ref, woa_ref, wom_ref, g2_ref, wg_ref, wu_ref, wd_ref,
                    gf_ref, o_ref, *, final_norm):
    h = (h_ref[...]
         + jnp.dot(ya_ref[...], woa_ref[...], preferred_element_type=F32)
         + jnp.dot(ym_ref[...], wom_ref[...], preferred_element_type=F32))
    hn = _rms(h, g2_ref[...], RMS_EPS).astype(BF16)
    x = h + 0.5 * _swiglu(hn, wg_ref, wu_ref, wd_ref)
    if final_norm:
        x = _rms(x, gf_ref[...], RMS_EPS)
    o_ref[...] = x


def _out_ffn(h2, ya, ym, woa, wom, g2, wg, wu, wd, gf, l, final_norm):
    n = h2.shape[0]
    row = lambda w: pl.BlockSpec((ROW_TILE, w), lambda i: (i, 0))
    mix_half = SC_WIDTH + LRU_WIDTH
    return pl.pallas_call(
        functools.partial(_out_ffn_kernel, final_norm=final_norm),
        grid=(n // ROW_TILE,),
        in_specs=[
            row(D_MODEL), row(DA_WIDTH), row(mix_half),
            _resident((None, DA_WIDTH, D_MODEL), lambda i: (l, 0, 0)),
            _resident((None, mix_half, D_MODEL), lambda i: (l, 0, 0)),
            pl.BlockSpec((None, 1, D_MODEL), lambda i: (l, 0, 0)),
            _resident((None, D_MODEL, D_FF), lambda i: (l, 0, 0)),
            _resident((None, D_MODEL, D_FF), lambda i: (l, 0, 0)),
            _resident((None, D_FF, D_MODEL), lambda i: (l, 0, 0)),
            pl.BlockSpec((1, D_MODEL), lambda i: (0, 0)),
        ],
        out_specs=row(D_MODEL),
        out_shape=jax.ShapeDtypeStruct((n, D_MODEL), F32),
        compiler_params=pltpu.CompilerParams(
            dimension_semantics=("arbitrary",), vmem_limit_bytes=VMEM_LIMIT_BYTES),
        name="out_ffn",
    )(h2, ya, ym, woa, wom, g2, wg, wu, wd, gf)


def _bias_kernel(rb_ref, o_ref):
    h = pl.program_id(0)
    t = ATTN_TILE
    max_exact = NUM_BUCKETS // 2
    key = lax.broadcasted_iota(jnp.int32, (t, t), 0)
    qry = lax.broadcasted_iota(jnp.int32, (t, t), 1)
    for off in range(2):
        dist = qry - key + off * t
        n = jnp.maximum(dist, 0)
        nf = jnp.maximum(n, 1).astype(F32)
        large = max_exact + (jnp.log(nf / max_exact) / math.log(MAX_DISTANCE / max_exact)
                             * (NUM_BUCKETS - max_exact)).astype(jnp.int32)
        large = jnp.minimum(large, NUM_BUCKETS - 1)
        bucket = jnp.where(n < max_exact, n, large)
        for mp in range(2):
            hm = 2 * h + mp
            val = jnp.zeros((t, t), F32)
            for j in range(NUM_BUCKETS):
                val = jnp.where(bucket == j, rb_ref[j, hm], val)
            far = rb_ref[NUM_BUCKETS - 1, hm]
            o_ref[off, :, mp * t:(mp + 1) * t] = jnp.where(dist >= 0, val - far, NEG_INF)


def _bias_tiles(rel_bias):
    t = ATTN_TILE
    return pl.pallas_call(
        _bias_kernel,
        grid=(DA_HEADS,),
        in_specs=[pl.BlockSpec(memory_space=pltpu.SMEM)],
        out_specs=pl.BlockSpec((None, 2, t, 2 * t), lambda i: (i, 0, 0, 0)),
        out_shape=jax.ShapeDtypeStruct((DA_HEADS, 2, t, 2 * t), F32),
        compiler_params=pltpu.CompilerParams(dimension_semantics=("arbitrary",)),
        name="bias_tiles",
    )(rel_bias)


def _attn_kernel(lam_ref, g_ref, bias_ref, q_ref, k_ref, v_ref, o_ref,
                 qq_ref, sa_ref, sb_ref, mxa_ref, mxb_ref, m_ref, l_ref, acc_ref, *, lam_init):
    t = ATTN_TILE
    qi = pl.program_id(2)
    lane = lax.broadcasted_iota(jnp.int32, (t, LANES), 1)
    q = q_ref[...] * (DA_HEAD_DIM ** -0.5)
    zero = jnp.zeros_like(q)
    qq_ref[0:t, :] = jnp.where(lane < DA_HEAD_DIM, q, zero)
    qq_ref[t:2 * t, :] = jnp.where(lane >= DA_HEAD_DIM, q, zero)

    m_ref[...] = jnp.full(m_ref.shape, NEG_INF, F32)
    l_ref[...] = jnp.zeros(l_ref.shape, F32)
    acc_ref[...] = jnp.zeros(acc_ref.shape, F32)

    def scores(kj, s_ref, mx_ref, off):
        k = k_ref[pl.ds(pl.multiple_of(kj * t, t), t), :]
        s = lax.dot_general(k, qq_ref[...], (((1,), (1,)), ((), ())),
                            preferred_element_type=F32)
        if off is not None:
            s = s + bias_ref[off]
        s_ref[...] = s
        mx_ref[...] = jnp.max(s, axis=0, keepdims=True)

    def consume(kj, s_ref, mx_ref):
        v = v_ref[pl.ds(pl.multiple_of(kj * t, t), t), :]
        m_old = m_ref[...]
        m_new = jnp.maximum(m_old, mx_ref[...])
        alpha = jnp.exp(m_old - m_new)
        p = jnp.exp(s_ref[...] - m_new)
        l_ref[...] = alpha * l_ref[...] + jnp.sum(p, axis=0, keepdims=True)
        pv = lax.dot_general(v, p.astype(BF16), (((0,), (0,)), ((), ())),
                             preferred_element_type=F32)
        acc_ref[...] = alpha * acc_ref[...] + pv
        m_ref[...] = m_new

    scores(qi, sa_ref, mxa_ref, 0)

    @pl.when(qi == 0)
    def _():
        consume(qi, sa_ref, mxa_ref)

    @pl.when(qi >= 1)
    def _():
        n_far = qi - 1
        n_pairs = n_far // 2
        scores(qi - 1, sb_ref, mxb_ref, 1)
        consume(qi, sa_ref, mxa_ref)

        def pair(i, carry):
            in_b = jnp.where(i == 0, qi - 1, 2 * i - 1)
            scores(2 * i, sa_ref, mxa_ref, None)
            consume(in_b, sb_ref, mxb_ref)
            scores(2 * i + 1, sb_ref, mxb_ref, None)
            consume(2 * i, sa_ref, mxa_ref)
            return carry

        lax.fori_loop(0, n_pairs, pair, 0)
        in_b = jnp.where(n_pairs == 0, qi - 1, 2 * n_pairs - 1)

        @pl.when(n_far % 2 == 1)
        def _():
            scores(n_far - 1, sa_ref, mxa_ref, None)
            consume(in_b, sb_ref, mxb_ref)
            consume(n_far - 1, sa_ref, mxa_ref)

        @pl.when(n_far % 2 == 0)
        def _():
            consume(in_b, sb_ref, mxb_ref)

    lp = lam_ref[...]
    lam = (jnp.exp(jnp.sum(lp[0:1] * lp[1:2], axis=1, keepdims=True))
           - jnp.exp(jnp.sum(lp[2:3] * lp[3:4], axis=1, keepdims=True)) + lam_init)
    on = acc_ref[...] / l_ref[...]
    o = (on[:, 0:t] - lam * on[:, t:2 * t]).T
    o_ref[...] = (_rms(o, g_ref[...], SUBLN_EPS) * (1.0 - lam_init)).astype(o_ref.dtype)


def _attention(zq3, lam_params, subln_g, bias, l, lam_init):
    b, s, _ = zq3.shape
    t = ATTN_TILE
    return pl.pallas_call(
        functools.partial(_attn_kernel, lam_init=lam_init),
        grid=(b, DA_HEADS, s // t),
        in_specs=[
            pl.BlockSpec((None, 4, DA_HEAD_DIM), lambda bi, h, qi: (l, 0, 0)),
            pl.BlockSpec((None, 1, DA_VDIM), lambda bi, h, qi: (l, 0, 0)),
            pl.BlockSpec((None, 2, t, 2 * t), lambda bi, h, qi: (h, 0, 0, 0)),
            pl.BlockSpec((None, t, LANES), lambda bi, h, qi: (bi, qi, h)),
            pl.BlockSpec((None, s, LANES), lambda bi, h, qi: (bi, 0, DA_HEADS + h)),
            pl.BlockSpec((None, s, LANES), lambda bi, h, qi: (bi, 0, 2 * DA_HEADS + h)),
        ],
        out_specs=pl.BlockSpec((None, t, LANES), lambda bi, h, qi: (bi, qi, h)),
        out_shape=jax.ShapeDtypeStruct((b, s, DA_WIDTH), BF16),
        scratch_shapes=[
            pltpu.VMEM((2 * t, LANES), BF16),
            pltpu.VMEM((t, 2 * t), F32),
            pltpu.VMEM((t, 2 * t), F32),
            pltpu.VMEM((1, 2 * t), F32),
            pltpu.VMEM((1, 2 * t), F32),
            pltpu.VMEM((1, 2 * t), F32),
            pltpu.VMEM((1, 2 * t), F32),
            pltpu.VMEM((DA_VDIM, 2 * t), F32),
        ],
        compiler_params=pltpu.CompilerParams(
            dimension_semantics=("arbitrary", "arbitrary", "arbitrary"),
            vmem_limit_bytes=VMEM_LIMIT_BYTES),
        name="diff_attn",
    )(lam_params, subln_g, bias, zq3, zq3, zq3)


def _shift_rows(x, halo, j):
    r = pltpu.roll(x, j, 0)
    head_row = lax.broadcasted_iota(jnp.int32, halo.shape, 0)
    head = jnp.where(head_row < j, pltpu.roll(halo, j, 0), r[:SUBLANES])
    return jnp.concatenate([head, r[SUBLANES:]], axis=0)


def _gelu_tanh(x):
    return 0.5 * x * (1.0 + jnp.tanh(math.sqrt(2.0 / math.pi) * (x + 0.044715 * (x * x * x))))


def _mix_kernel(z_ref, scw_ref, lcw_ref, lcb_ref, wa_ref, ba_ref, wx_ref, bx_ref, lam_ref,
                o_ref, hc_ref, halo_sc_ref, halo_lx_ref):
    tc = MIX_TILE
    w = SC_WIDTH

    @pl.when(pl.program_id(1) == 0)
    def _():
        hc_ref[...] = jnp.zeros(hc_ref.shape, F32)
        halo_sc_ref[...] = jnp.zeros(halo_sc_ref.shape, F32)
        halo_lx_ref[...] = jnp.zeros(halo_lx_ref.shape, F32)

    row = lax.broadcasted_iota(jnp.int32, (tc, w), 0)

    sc_b = z_ref[:, 0:w]
    cx = z_ref[:, w:2 * w] * z_ref[:, 2 * w:3 * w]
    halo = halo_sc_ref[...]
    conv = scw_ref[SC_KERNEL - 1:SC_KERNEL, :] * cx
    for j in range(1, SC_KERNEL):
        conv = conv + scw_ref[SC_KERNEL - 1 - j:SC_KERNEL - j, :] * _shift_rows(cx, halo, j)
    halo_sc_ref[...] = cx[tc - SUBLANES:, :]
    o_ref[:, 0:w] = (sc_b * conv).astype(o_ref.dtype)

    lx = z_ref[:, 3 * w:4 * w]
    lg = z_ref[:, 4 * w:5 * w]
    halo = halo_lx_ref[...]
    xr = lcw_ref[LRU_CONV - 1:LRU_CONV, :] * lx + lcb_ref[...]
    for j in range(1, LRU_CONV):
        xr = xr + lcw_ref[LRU_CONV - 1 - j:LRU_CONV - j, :] * _shift_rows(lx, halo, j)
    halo_lx_ref[...] = lx[tc - SUBLANES:, :]

    xb = xr.astype(BF16)
    r = jax.nn.sigmoid(jnp.dot(xb, wa_ref[...], preferred_element_type=F32) + ba_ref[...])
    i = jax.nn.sigmoid(jnp.dot(xb, wx_ref[...], preferred_element_type=F32) + bx_ref[...])
    nl = -lam_ref[...]
    softplus = jnp.maximum(nl, 0.0) + jnp.log1p(jnp.exp(-jnp.abs(nl)))
    log_a = (-LRU_C) * r * softplus
    a = jnp.exp(log_a)
    bv = jnp.sqrt(-jnp.tanh(log_a) * (a * a + 1.0)) * (i * xr)

    s = 1
    while s < tc:
        keep = row >= s
        a_sh = jnp.where(keep, pltpu.roll(a, s, 0), 1.0)
        b_sh = jnp.where(keep, pltpu.roll(bv, s, 0), 0.0)
        bv = a * b_sh + bv
        a = a * a_sh
        s *= 2
    h = bv + a * hc_ref[0:1, :]
    hc_ref[0:1, :] = h[tc - 1:tc, :]
    o_ref[:, w:2 * w] = (_gelu_tanh(lg) * h).astype(o_ref.dtype)


def _mix(zr3, scw, lcw, lcb, wa, ba, wx, bx, lam, l):
    b, s, _ = zr3.shape
    w = SC_WIDTH
    par = lambda rows: pl.BlockSpec((None, rows, w), lambda bi, ti: (l, 0, 0))
    return pl.pallas_call(
        _mix_kernel,
        grid=(b, s // MIX_TILE),
        in_specs=[
            pl.BlockSpec((None, MIX_TILE, REST_WIDTH), lambda bi, ti: (bi, ti, 0)),
            par(SC_KERNEL), par(LRU_CONV), par(1), par(w), par(1), par(w), par(1), par(1),
        ],
        out_specs=pl.BlockSpec((None, MIX_TILE, 2 * w), lambda bi, ti: (bi, ti, 0)),
        out_shape=jax.ShapeDtypeStruct((b, s, 2 * w), BF16),
        scratch_shapes=[pltpu.VMEM((SUBLANES, w), F32)] * 3,
        compiler_params=pltpu.CompilerParams(
            dimension_semantics=("arbitrary", "arbitrary"),
            vmem_limit_bytes=VMEM_LIMIT_BYTES),
        name="conv_lru",
    )(zr3, scw, lcw, lcb, wa, ba, wx, bx, lam)


def _block_diag(w):
    depth, nb, blk, _ = w.shape
    eye = jnp.eye(nb, dtype=w.dtype)
    return jnp.einsum('lnij,nm->lnimj', w, eye).reshape(depth, nb * blk, nb * blk)


def kernel(x, rel_bias, ffn1_norm, ffn1_gate, ffn1_up, ffn1_down, mix_norm, w_in, w_out, lam_q1, lam_k1, lam_q2, lam_k2, subln_gain, sc_conv_w, lru_conv_w, lru_conv_b, lru_wa, lru_ba, lru_wx, lru_bx, lru_lambda, ffn2_norm, ffn2_gate, ffn2_up, ffn2_down, final_norm):
    b, s, d = x.shape
    depth = w_in.shape[0]
    assert d == D_MODEL and s % ATTN_TILE == 0 and s % MIX_TILE == 0 and (b * s) % ROW_TILE == 0

    bf = lambda w: w.astype(BF16)
    vec = lambda v: v.reshape(depth, 1, v.shape[-1])
    wg1, wu1, wd1 = bf(ffn1_gate), bf(ffn1_up), bf(ffn1_down)
    wg2, wu2, wd2 = bf(ffn2_gate), bf(ffn2_up), bf(ffn2_down)
    wq, wr = bf(w_in[:, :, :QKV_WIDTH]), bf(w_in[:, :, QKV_WIDTH:])
    woa, wom = bf(w_out[:, :DA_WIDTH]), bf(w_out[:, DA_WIDTH:])
    wa, wx = bf(_block_diag(lru_wa)), bf(_block_diag(lru_wx))
    lam_params = jnp.stack([lam_q1, lam_k1, lam_q2, lam_k2], axis=1)
    g1, gm, g2 = vec(ffn1_norm), vec(mix_norm), vec(ffn2_norm)
    gs, lcb, lam = vec(subln_gain), vec(lru_conv_b), vec(lru_lambda)
    ba, bx = vec(lru_ba.reshape(depth, -1)), vec(lru_bx.reshape(depth, -1))
    gf = final_norm.reshape(1, d)

    bias = _bias_tiles(rel_bias)

    x2 = x.reshape(b * s, d)
    for l in range(depth):
        lam_init = 0.8 - 0.6 * math.exp(-0.3 * l)
        h2, zq, zr = _ffn_in(x2, g1, wg1, wu1, wd1, gm, wq, wr, l)
        ya = _attention(zq.reshape(b, s, QKV_WIDTH), lam_params, gs, bias, l, lam_init)
        ym = _mix(zr.reshape(b, s, REST_WIDTH), sc_conv_w, lru_conv_w, lcb, wa, ba, wx, bx, lam, l)
        x2 = _out_ffn(h2, ya.reshape(b * s, DA_WIDTH), ym.reshape(b * s, 2 * SC_WIDTH),
                      woa, wom, g2, wg2, wu2, wd2, gf, l, l == depth - 1)
    return x2.reshape(b, s, d)
```

```python
import functools
import math

import jax
import jax.numpy as jnp
from jax import lax
from jax.experimental import pallas as pl
from jax.experimental.pallas import tpu as pltpu

F32 = jnp.float32
BF16 = jnp.bfloat16

D_MODEL = 1024
D_FF = 2816
DA_WIDTH = 512
SC_WIDTH = 256
LRU_WIDTH = 256
DA_HEAD_DIM = 64
DA_HEADS = 4
DA_VDIM = 2 * DA_HEAD_DIM
NUM_BUCKETS = 32
MAX_DISTANCE = 128
SUBLN_EPS = 1e-5
SC_KERNEL = 3
LRU_BLOCKS = 4
LRU_BLOCK = 64
LRU_CONV = 4
LRU_C = 8.0
RMS_EPS = 1e-6
NEG_INF = -1e30
QKV_WIDTH = 3 * DA_WIDTH
REST_WIDTH = 3 * SC_WIDTH + 2 * LRU_WIDTH
IN_WIDTH = QKV_WIDTH + REST_WIDTH

LANES = 128
SUBLANES = 8
VMEM_LIMIT_BYTES = 56 * 1024 * 1024
ROW_TILE = 256
ATTN_TILE = 512
MIX_TILE = 256


def _rms(x, g, eps):
    return x * lax.rsqrt(jnp.mean(x * x, axis=-1, keepdims=True) + eps) * g


def _swiglu(xn, wg_ref, wu_ref, wd_ref):
    g = jnp.dot(xn, wg_ref[...], preferred_element_type=F32)
    u = jnp.dot(xn, wu_ref[...], preferred_element_type=F32)
    a = (g * jax.nn.sigmoid(g) * u).astype(BF16)
    return jnp.dot(a, wd_ref[...], preferred_element_type=F32)


def _ffn_in_kernel(x_ref, g1_ref, wg_ref, wu_ref, wd_ref, gm_ref, wq_ref, wr_ref,
                   h_ref, zq_ref, zr_ref):
    x = x_ref[...]
    xn = _rms(x, g1_ref[...], RMS_EPS).astype(BF16)
    h = x + 0.5 * _swiglu(xn, wg_ref, wu_ref, wd_ref)
    h_ref[...] = h
    u = _rms(h, gm_ref[...], RMS_EPS).astype(BF16)
    zq_ref[...] = jnp.dot(u, wq_ref[...], preferred_element_type=F32).astype(BF16)
    zr_ref[...] = jnp.dot(u, wr_ref[...], preferred_element_type=F32)


def _resident(shape, index):
    return pl.BlockSpec(shape, index, pipeline_mode=pl.Buffered(1))


def _ffn_in(x2, g1, wg, wu, wd, gm, wq, wr, l):
    n = x2.shape[0]
    row = lambda w: pl.BlockSpec((ROW_TILE, w), lambda i: (i, 0))
    vec = pl.BlockSpec((None, 1, D_MODEL), lambda i: (l, 0, 0))
    return pl.pallas_call(
        _ffn_in_kernel,
        grid=(n // ROW_TILE,),
        in_specs=[
            row(D_MODEL), vec,
            _resident((None, D_MODEL, D_FF), lambda i: (l, 0, 0)),
            _resident((None, D_MODEL, D_FF), lambda i: (l, 0, 0)),
            _resident((None, D_FF, D_MODEL), lambda i: (l, 0, 0)),
            vec,
            _resident((None, D_MODEL, QKV_WIDTH), lambda i: (l, 0, 0)),
            _resident((None, D_MODEL, REST_WIDTH), lambda i: (l, 0, 0)),
        ],
        out_specs=[row(D_MODEL), row(QKV_WIDTH), row(REST_WIDTH)],
        out_shape=[
            jax.ShapeDtypeStruct((n, D_MODEL), F32),
            jax.ShapeDtypeStruct((n, QKV_WIDTH), BF16),
            jax.ShapeDtypeStruct((n, REST_WIDTH), F32),
        ],
        compiler_params=pltpu.CompilerParams(
            dimension_semantics=("arbitrary",), vmem_limit_bytes=VMEM_LIMIT_BYTES),
        name="ffn_in",
    )(x2, g1, wg, wu, wd, gm, wq, wr)


def _out_ffn_kernel(h_ref, ya_ref, ym_ref, woa_ref, wom_ref, g2_ref, wg_ref, wu_ref, wd_ref,
                    gf_ref, o_ref, *, final_norm):
    h = (h_ref[...]
         + lax.dot_general(ya_ref[...], woa_ref[...], (((0,), (0,)), ((), ())),
                           preferred_element_type=F32)
         + jnp.dot(ym_ref[...], wom_ref[...], preferred_element_type=F32))
    hn = _rms(h, g2_ref[...], RMS_EPS).astype(BF16)
    x = h + 0.5 * _swiglu(hn, wg_ref, wu_ref, wd_ref)
    if final_norm:
        x = _rms(x, gf_ref[...], RMS_EPS)
    o_ref[...] = x


def _out_ffn(h2, ya_t, ym, woa, wom, g2, wg, wu, wd, gf, l, final_norm):
    n = h2.shape[0]
    tiles_per_seq = ya_t.shape[2] // ROW_TILE
    row = lambda w: pl.BlockSpec((ROW_TILE, w), lambda i: (i, 0))
    mix_half = SC_WIDTH + LRU_WIDTH
    return pl.pallas_call(
        functools.partial(_out_ffn_kernel, final_norm=final_norm),
        grid=(n // ROW_TILE,),
        in_specs=[
            row(D_MODEL),
            pl.BlockSpec((None, DA_WIDTH, ROW_TILE),
                         lambda i: (i // tiles_per_seq, 0, i % tiles_per_seq)),
            row(mix_half),
            _resident((None, DA_WIDTH, D_MODEL), lambda i: (l, 0, 0)),
            _resident((None, mix_half, D_MODEL), lambda i: (l, 0, 0)),
            pl.BlockSpec((None, 1, D_MODEL), lambda i: (l, 0, 0)),
            _resident((None, D_MODEL, D_FF), lambda i: (l, 0, 0)),
            _resident((None, D_MODEL, D_FF), lambda i: (l, 0, 0)),
            _resident((None, D_FF, D_MODEL), lambda i: (l, 0, 0)),
            pl.BlockSpec((1, D_MODEL), lambda i: (0, 0)),
        ],
        out_specs=row(D_MODEL),
        out_shape=jax.ShapeDtypeStruct((n, D_MODEL), F32),
        compiler_params=pltpu.CompilerParams(
            dimension_semantics=("arbitrary",), vmem_limit_bytes=VMEM_LIMIT_BYTES),
        name="out_ffn",
    )(h2, ya_t, ym, woa, wom, g2, wg, wu, wd, gf)


def _bias_kernel(rb_ref, o_ref):
    h = pl.program_id(0)
    t = ATTN_TILE
    max_exact = NUM_BUCKETS // 2
    key = lax.broadcasted_iota(jnp.int32, (t, t), 0)
    qry = lax.broadcasted_iota(jnp.int32, (t, t), 1)
    for off in range(2):
        dist = qry - key + off * t
        n = jnp.maximum(dist, 0)
        nf = jnp.maximum(n, 1).astype(F32)
        large = max_exact + (jnp.log(nf / max_exact) / math.log(MAX_DISTANCE / max_exact)
                             * (NUM_BUCKETS - max_exact)).astype(jnp.int32)
        large = jnp.minimum(large, NUM_BUCKETS - 1)
        bucket = jnp.where(n < max_exact, n, large)
        for mp in range(2):
            hm = 2 * h + mp
            val = jnp.zeros((t, t), F32)
            for j in range(NUM_BUCKETS):
                val = jnp.where(bucket == j, rb_ref[j, hm], val)
            far = rb_ref[NUM_BUCKETS - 1, hm]
            o_ref[off, :, mp * t:(mp + 1) * t] = jnp.where(dist >= 0, val - far, NEG_INF)


def _bias_tiles(rel_bias):
    t = ATTN_TILE
    return pl.pallas_call(
        _bias_kernel,
        grid=(DA_HEADS,),
        in_specs=[pl.BlockSpec(memory_space=pltpu.SMEM)],
        out_specs=pl.BlockSpec((None, 2, t, 2 * t), lambda i: (i, 0, 0, 0)),
        out_shape=jax.ShapeDtypeStruct((DA_HEADS, 2, t, 2 * t), F32),
        compiler_params=pltpu.CompilerParams(dimension_semantics=("arbitrary",)),
        name="bias_tiles",
    )(rel_bias)


def _attn_kernel(lam_ref, g_ref, bias_ref, q_ref, k_ref, v_ref, o_ref,
                 qq_ref, s_ref, mx_ref, m_ref, l_ref, acc_ref, *, lam_init, nq):
    t = ATTN_TILE
    lane = lax.broadcasted_iota(jnp.int32, (t, LANES), 1)
    for qi in range(nq):
        q = q_ref[qi * t:(qi + 1) * t, :] * (DA_HEAD_DIM ** -0.5)
        zero = jnp.zeros_like(q)
        qq_ref[qi, 0:t, :] = jnp.where(lane < DA_HEAD_DIM, q, zero)
        qq_ref[qi, t:2 * t, :] = jnp.where(lane >= DA_HEAD_DIM, q, zero)

    lp = lam_ref[...]
    lam = (jnp.exp(jnp.sum(lp[0:1] * lp[1:2], axis=1, keepdims=True))
           - jnp.exp(jnp.sum(lp[2:3] * lp[3:4], axis=1, keepdims=True)) + lam_init)
    gain = g_ref[...] * (1.0 - lam_init)

    m_ref[...] = jnp.full(m_ref.shape, NEG_INF, F32)
    l_ref[...] = jnp.zeros(l_ref.shape, F32)
    acc_ref[...] = jnp.zeros(acc_ref.shape, F32)

    def key_rows(kj):
        return pl.ds(kj * t, t) if isinstance(kj, int) else pl.ds(pl.multiple_of(kj * t, t), t)

    def scores(qi, kj, off, buf):
        s = lax.dot_general(k_ref[key_rows(kj), :], qq_ref[qi], (((1,), (1,)), ((), ())),
                            preferred_element_type=F32)
        if off is not None:
            s = s + bias_ref[off]
        s_ref[buf] = s
        mx_ref[buf] = jnp.max(s, axis=0, keepdims=True)

    def consume(qi, kj, off, buf):
        m_old = m_ref[qi]
        m_new = jnp.maximum(m_old, mx_ref[buf])
        alpha = jnp.exp(m_old - m_new)
        p = jnp.exp(s_ref[buf] - m_new)
        l_ref[qi] = alpha * l_ref[qi] + jnp.sum(p, axis=0, keepdims=True)
        pv = lax.dot_general(v_ref[key_rows(kj), :], p.astype(BF16),
                             (((0,), (0,)), ((), ())), preferred_element_type=F32)
        acc_ref[qi] = alpha * acc_ref[qi] + pv
        m_ref[qi] = m_new

    def sweep(n_steps, coords):
        if n_steps == 0:
            return
        scores(*coords(0), 0)

        def pair(i, carry):
            scores(*coords(2 * i + 1), 1)
            consume(*coords(2 * i), 0)
            scores(*coords(2 * i + 2), 0)
            consume(*coords(2 * i + 1), 1)
            return carry

        lax.fori_loop(0, (n_steps - 1) // 2, pair, 0)
        if n_steps % 2 == 1:
            consume(*coords(n_steps - 1), 0)
        else:
            scores(*coords(n_steps - 1), 1)
            consume(*coords(n_steps - 2), 0)
            consume(*coords(n_steps - 1), 1)

    def biased_coords(n):
        if isinstance(n, int):
            qi, off = (n + 1) // 2, n % 2
        else:
            qi, off = lax.shift_right_logical(n + 1, 1), lax.bitwise_and(n, 1)
        return qi, qi - off, off

    def far_coords(n):
        first = lambda qi: (qi - 1) * (qi - 2) // 2
        if isinstance(n, int):
            qi = max(c for c in range(2, nq) if first(c) <= n)
            return qi, n - first(qi), None
        qi = 2
        for c in range(3, nq):
            qi = qi + (n >= first(c)).astype(jnp.int32)
        return qi, n - lax.shift_right_logical((qi - 1) * (qi - 2), 1), None

    sweep(2 * nq - 1, biased_coords)
    sweep((nq - 1) * (nq - 2) // 2, far_coords)

    for qi in range(nq):
        on = acc_ref[qi] / l_ref[qi]
        o = on[:, 0:t] - lam * on[:, t:2 * t]
        y = o * lax.rsqrt(jnp.mean(o * o, axis=0, keepdims=True) + SUBLN_EPS) * gain
        o_ref[:, qi * t:(qi + 1) * t] = y.astype(o_ref.dtype)


def _attention(zq3, lam_params, subln_g, bias, l, lam_init):
    b, s, _ = zq3.shape
    t = ATTN_TILE
    nq = s // t
    return pl.pallas_call(
        functools.partial(_attn_kernel, lam_init=lam_init, nq=nq),
        grid=(b, DA_HEADS),
        in_specs=[
            pl.BlockSpec((None, 4, DA_HEAD_DIM), lambda bi, h: (l, 0, 0)),
            pl.BlockSpec((None, DA_VDIM, 1), lambda bi, h: (l, 0, 0)),
            pl.BlockSpec((None, 2, t, 2 * t), lambda bi, h: (h, 0, 0, 0)),
            pl.BlockSpec((None, s, LANES), lambda bi, h: (bi, 0, h)),
            pl.BlockSpec((None, s, LANES), lambda bi, h: (bi, 0, DA_HEADS + h)),
            pl.BlockSpec((None, s, LANES), lambda bi, h: (bi, 0, 2 * DA_HEADS + h)),
        ],
        out_specs=pl.BlockSpec((None, DA_VDIM, s), lambda bi, h: (bi, h, 0)),
        out_shape=jax.ShapeDtypeStruct((b, DA_WIDTH, s), BF16),
        scratch_shapes=[
            pltpu.VMEM((nq, 2 * t, LANES), BF16),
            pltpu.VMEM((2, t, 2 * t), F32),
            pltpu.VMEM((2, 1, 2 * t), F32),
            pltpu.VMEM((nq, 1, 2 * t), F32),
            pltpu.VMEM((nq, 1, 2 * t), F32),
            pltpu.VMEM((nq, DA_VDIM, 2 * t), F32),
        ],
        compiler_params=pltpu.CompilerParams(
            dimension_semantics=("arbitrary", "arbitrary"),
            vmem_limit_bytes=VMEM_LIMIT_BYTES),
        name="diff_attn",
    )(lam_params, subln_g, bias, zq3, zq3, zq3)


def _shift_rows(x, halo, j):
    r = pltpu.roll(x, j, 0)
    head_row = lax.broadcasted_iota(jnp.int32, halo.shape, 0)
    head = jnp.where(head_row < j, pltpu.roll(halo, j, 0), r[:SUBLANES])
    return jnp.concatenate([head, r[SUBLANES:]], axis=0)


def _gelu_tanh(x):
    return 0.5 * x * (1.0 + jnp.tanh(math.sqrt(2.0 / math.pi) * (x + 0.044715 * (x * x * x))))


def _mix_kernel(z_ref, scw_ref, lcw_ref, lcb_ref, wa_ref, ba_ref, wx_ref, bx_ref, lam_ref,
                o_ref, hc_ref, halo_sc_ref, halo_lx_ref):
    tc = MIX_TILE
    w = SC_WIDTH

    @pl.when(pl.program_id(1) == 0)
    def _():
        hc_ref[...] = jnp.zeros(hc_ref.shape, F32)
        halo_sc_ref[...] = jnp.zeros(halo_sc_ref.shape, F32)
        halo_lx_ref[...] = jnp.zeros(halo_lx_ref.shape, F32)

    row = lax.broadcasted_iota(jnp.int32, (tc, w), 0)

    sc_b = z_ref[:, 0:w]
    cx = z_ref[:, w:2 * w] * z_ref[:, 2 * w:3 * w]
    halo = halo_sc_ref[...]
    conv = scw_ref[SC_KERNEL - 1:SC_KERNEL, :] * cx
    for j in range(1, SC_KERNEL):
        conv = conv + scw_ref[SC_KERNEL - 1 - j:SC_KERNEL - j, :] * _shift_rows(cx, halo, j)
    halo_sc_ref[...] = cx[tc - SUBLANES:, :]
    o_ref[:, 0:w] = (sc_b * conv).astype(o_ref.dtype)

    lx = z_ref[:, 3 * w:4 * w]
    lg = z_ref[:, 4 * w:5 * w]
    halo = halo_lx_ref[...]
    xr = lcw_ref[LRU_CONV - 1:LRU_CONV, :] * lx + lcb_ref[...]
    for j in range(1, LRU_CONV):
        xr = xr + lcw_ref[LRU_CONV - 1 - j:LRU_CONV - j, :] * _shift_rows(lx, halo, j)
    halo_lx_ref[...] = lx[tc - SUBLANES:, :]

    xb = xr.astype(BF16)
    r = jax.nn.sigmoid(jnp.dot(xb, wa_ref[...], preferred_element_type=F32) + ba_ref[...])
    i = jax.nn.sigmoid(jnp.dot(xb, wx_ref[...], preferred_element_type=F32) + bx_ref[...])
    nl = -lam_ref[...]
    softplus = jnp.maximum(nl, 0.0) + jnp.log1p(jnp.exp(-jnp.abs(nl)))
    log_a = (-LRU_C) * r * softplus
    a = jnp.exp(log_a)
    bv = jnp.sqrt(-jnp.tanh(log_a) * (a * a + 1.0)) * (i * xr)

    s = 1
    while s < tc:
        keep = row >= s
        a_sh = jnp.where(keep, pltpu.roll(a, s, 0), 1.0)
        b_sh = jnp.where(keep, pltpu.roll(bv, s, 0), 0.0)
        bv = a * b_sh + bv
        a = a * a_sh
        s *= 2
    h = bv + a * hc_ref[0:1, :]
    hc_ref[0:1, :] = h[tc - 1:tc, :]
    o_ref[:, w:2 * w] = (_gelu_tanh(lg) * h).astype(o_ref.dtype)


def _mix(zr3, scw, lcw, lcb, wa, ba, wx, bx, lam, l):
    b, s, _ = zr3.shape
    w = SC_WIDTH
    par = lambda rows: pl.BlockSpec((None, rows, w), lambda bi, ti: (l, 0, 0))
    return pl.pallas_call(
        _mix_kernel,
        grid=(b, s // MIX_TILE),
        in_specs=[
            pl.BlockSpec((None, MIX_TILE, REST_WIDTH), lambda bi, ti: (bi, ti, 0)),
            par(SC_KERNEL), par(LRU_CONV), par(1), par(w), par(1), par(w), par(1), par(1),
        ],
        out_specs=pl.BlockSpec((None, MIX_TILE, 2 * w), lambda bi, ti: (bi, ti, 0)),
        out_shape=jax.ShapeDtypeStruct((b, s, 2 * w), BF16),
        scratch_shapes=[pltpu.VMEM((SUBLANES, w), F32)] * 3,
        compiler_params=pltpu.CompilerParams(
            dimension_semantics=("arbitrary", "arbitrary"),
            vmem_limit_bytes=VMEM_LIMIT_BYTES),
        name="conv_lru",
    )(zr3, scw, lcw, lcb, wa, ba, wx, bx, lam)


def _block_diag(w):
    depth, nb, blk, _ = w.shape
    eye = jnp.eye(nb, dtype=w.dtype)
    return jnp.einsum('lnij,nm->lnimj', w, eye).reshape(depth, nb * blk, nb * blk)


def kernel(x, rel_bias, ffn1_norm, ffn1_gate, ffn1_up, ffn1_down, mix_norm, w_in, w_out, lam_q1, lam_k1, lam_q2, lam_k2, subln_gain, sc_conv_w, lru_conv_w, lru_conv_b, lru_wa, lru_ba, lru_wx, lru_bx, lru_lambda, ffn2_norm, ffn2_gate, ffn2_up, ffn2_down, final_norm):
    b, s, d = x.shape
    depth = w_in.shape[0]
    assert d == D_MODEL and s % ATTN_TILE == 0 and s % MIX_TILE == 0 and (b * s) % ROW_TILE == 0

    bf = lambda w: w.astype(BF16)
    vec = lambda v: v.reshape(depth, 1, v.shape[-1])
    wg1, wu1, wd1 = bf(ffn1_gate), bf(ffn1_up), bf(ffn1_down)
    wg2, wu2, wd2 = bf(ffn2_gate), bf(ffn2_up), bf(ffn2_down)
    wq, wr = bf(w_in[:, :, :QKV_WIDTH]), bf(w_in[:, :, QKV_WIDTH:])
    woa, wom = bf(w_out[:, :DA_WIDTH]), bf(w_out[:, DA_WIDTH:])
    wa, wx = bf(_block_diag(lru_wa)), bf(_block_diag(lru_wx))
    lam_params = jnp.stack([lam_q1, lam_k1, lam_q2, lam_k2], axis=1)
    g1, gm, g2 = vec(ffn1_norm), vec(mix_norm), vec(ffn2_norm)
    gs = subln_gain.reshape(depth, DA_VDIM, 1)
    lcb, lam = vec(lru_conv_b), vec(lru_lambda)
    ba, bx = vec(lru_ba.reshape(depth, -1)), vec(lru_bx.reshape(depth, -1))
    gf = final_norm.reshape(1, d)

    bias = _bias_tiles(rel_bias)

    x2 = x.reshape(b * s, d)
    for l in range(depth):
        lam_init = 0.8 - 0.6 * math.exp(-0.3 * l)
        h2, zq, zr = _ffn_in(x2, g1, wg1, wu1, wd1, gm, wq, wr, l)
        ya = _attention(zq.reshape(b, s, QKV_WIDTH), lam_params, gs, bias, l, lam_init)
        ym = _mix(zr.reshape(b, s, REST_WIDTH), sc_conv_w, lru_conv_w, lcb, wa, ba, wx, bx, lam, l)
        x2 = _out_ffn(h2, ya, ym.reshape(b * s, 2 * SC_WIDTH),
                      woa, wom, g2, wg2, wu2, wd2, gf, l, l == depth - 1)
    return x2.reshape(b, s, d)
```

```python
import functools
import math

import jax
import jax.numpy as jnp
from jax import lax
from jax.experimental import pallas as pl
from jax.experimental.pallas import tpu as pltpu

F32 = jnp.float32
BF16 = jnp.bfloat16

D_MODEL = 1024
D_FF = 2816
DA_WIDTH = 512
SC_WIDTH = 256
LRU_WIDTH = 256
DA_HEAD_DIM = 64
DA_HEADS = 4
DA_VDIM = 2 * DA_HEAD_DIM
NUM_BUCKETS = 32
MAX_DISTANCE = 128
SUBLN_EPS = 1e-5
SC_KERNEL = 3
LRU_BLOCKS = 4
LRU_BLOCK = 64
LRU_CONV = 4
LRU_C = 8.0
RMS_EPS = 1e-6
NEG_INF = -1e30
QKV_WIDTH = 3 * DA_WIDTH
REST_WIDTH = 3 * SC_WIDTH + 2 * LRU_WIDTH
IN_WIDTH = QKV_WIDTH + REST_WIDTH
LOG2_E = math.log2(math.e)
Q_SCALE = DA_HEAD_DIM ** -0.5 * LOG2_E

LANES = 128
SUBLANES = 8
VMEM_LIMIT_BYTES = 56 * 1024 * 1024
ROW_TILE = 256
ATTN_TILE = 512
MIX_TILE = 256


def _rms(x, g, eps):
    return x * lax.rsqrt(jnp.mean(x * x, axis=-1, keepdims=True) + eps) * g


def _swiglu(xn, wg_ref, wu_ref, wd_ref):
    g = jnp.dot(xn, wg_ref[...], preferred_element_type=F32)
    u = jnp.dot(xn, wu_ref[...], preferred_element_type=F32)
    a = (g * jax.nn.sigmoid(g) * u).astype(BF16)
    return jnp.dot(a, wd_ref[...], preferred_element_type=F32)


def _ffn_in_kernel(x_ref, g1_ref, wg_ref, wu_ref, wd_ref, gm_ref, wq_ref, wr_ref,
                   h_ref, zq_ref, zr_ref):
    x = x_ref[...]
    xn = _rms(x, g1_ref[...], RMS_EPS).astype(BF16)
    h = x + 0.5 * _swiglu(xn, wg_ref, wu_ref, wd_ref)
    h_ref[...] = h
    u = _rms(h, gm_ref[...], RMS_EPS).astype(BF16)
    zq_ref[...] = jnp.dot(u, wq_ref[...], preferred_element_type=F32).astype(BF16)
    zr_ref[...] = jnp.dot(u, wr_ref[...], preferred_element_type=F32)


def _resident(shape, index):
    return pl.BlockSpec(shape, index, pipeline_mode=pl.Buffered(1))


def _ffn_in(x2, g1, wg, wu, wd, gm, wq, wr, l):
    n = x2.shape[0]
    row = lambda w: pl.BlockSpec((ROW_TILE, w), lambda i: (i, 0))
    vec = pl.BlockSpec((None, 1, D_MODEL), lambda i: (l, 0, 0))
    return pl.pallas_call(
        _ffn_in_kernel,
        grid=(n // ROW_TILE,),
        in_specs=[
            row(D_MODEL), vec,
            _resident((None, D_MODEL, D_FF), lambda i: (l, 0, 0)),
            _resident((None, D_MODEL, D_FF), lambda i: (l, 0, 0)),
            _resident((None, D_FF, D_MODEL), lambda i: (l, 0, 0)),
            vec,
            _resident((None, D_MODEL, QKV_WIDTH), lambda i: (l, 0, 0)),
            _resident((None, D_MODEL, REST_WIDTH), lambda i: (l, 0, 0)),
        ],
        out_specs=[row(D_MODEL), row(QKV_WIDTH), row(REST_WIDTH)],
        out_shape=[
            jax.ShapeDtypeStruct((n, D_MODEL), F32),
            jax.ShapeDtypeStruct((n, QKV_WIDTH), BF16),
            jax.ShapeDtypeStruct((n, REST_WIDTH), F32),
        ],
        compiler_params=pltpu.CompilerParams(
            dimension_semantics=("arbitrary",), vmem_limit_bytes=VMEM_LIMIT_BYTES),
        name="ffn_in",
    )(x2, g1, wg, wu, wd, gm, wq, wr)


def _out_ffn_kernel(h_ref, ya_ref, ym_ref, woa_ref, wom_ref, g2_ref, wg_ref, wu_ref, wd_ref,
                    gf_ref, o_ref, *, final_norm):
    h = (h_ref[...]
         + lax.dot_general(ya_ref[...], woa_ref[...], (((0,), (0,)), ((), ())),
                           preferred_element_type=F32)
         + jnp.dot(ym_ref[...], wom_ref[...], preferred_element_type=F32))
    hn = _rms(h, g2_ref[...], RMS_EPS).astype(BF16)
    x = h + 0.5 * _swiglu(hn, wg_ref, wu_ref, wd_ref)
    if final_norm:
        x = _rms(x, gf_ref[...], RMS_EPS)
    o_ref[...] = x


def _out_ffn(h2, ya_t, ym, woa, wom, g2, wg, wu, wd, gf, l, final_norm):
    n = h2.shape[0]
    tiles_per_seq = ya_t.shape[2] // ROW_TILE
    row = lambda w: pl.BlockSpec((ROW_TILE, w), lambda i: (i, 0))
    mix_half = SC_WIDTH + LRU_WIDTH
    return pl.pallas_call(
        functools.partial(_out_ffn_kernel, final_norm=final_norm),
        grid=(n // ROW_TILE,),
        in_specs=[
            row(D_MODEL),
            pl.BlockSpec((None, DA_WIDTH, ROW_TILE),
                         lambda i: (i // tiles_per_seq, 0, i % tiles_per_seq)),
            row(mix_half),
            _resident((None, DA_WIDTH, D_MODEL), lambda i: (l, 0, 0)),
            _resident((None, mix_half, D_MODEL), lambda i: (l, 0, 0)),
            pl.BlockSpec((None, 1, D_MODEL), lambda i: (l, 0, 0)),
            _resident((None, D_MODEL, D_FF), lambda i: (l, 0, 0)),
            _resident((None, D_MODEL, D_FF), lambda i: (l, 0, 0)),
            _resident((None, D_FF, D_MODEL), lambda i: (l, 0, 0)),
            pl.BlockSpec((1, D_MODEL), lambda i: (0, 0)),
        ],
        out_specs=row(D_MODEL),
        out_shape=jax.ShapeDtypeStruct((n, D_MODEL), F32),
        compiler_params=pltpu.CompilerParams(
            dimension_semantics=("arbitrary",), vmem_limit_bytes=VMEM_LIMIT_BYTES),
        name="out_ffn",
    )(h2, ya_t, ym, woa, wom, g2, wg, wu, wd, gf)


def _bias_kernel(rb_ref, o_ref):
    h = pl.program_id(0)
    t = ATTN_TILE
    max_exact = NUM_BUCKETS // 2
    key = lax.broadcasted_iota(jnp.int32, (t, t), 0)
    qry = lax.broadcasted_iota(jnp.int32, (t, t), 1)
    for off in range(2):
        dist = qry - key + off * t
        n = jnp.maximum(dist, 0)
        nf = jnp.maximum(n, 1).astype(F32)
        large = max_exact + (jnp.log(nf / max_exact) / math.log(MAX_DISTANCE / max_exact)
                             * (NUM_BUCKETS - max_exact)).astype(jnp.int32)
        large = jnp.minimum(large, NUM_BUCKETS - 1)
        bucket = jnp.where(n < max_exact, n, large)
        for mp in range(2):
            hm = 2 * h + mp
            val = jnp.zeros((t, t), F32)
            for j in range(NUM_BUCKETS):
                val = jnp.where(bucket == j, rb_ref[j, hm], val)
            far = rb_ref[NUM_BUCKETS - 1, hm]
            o_ref[off, :, mp * t:(mp + 1) * t] = jnp.where(dist >= 0, (val - far) * LOG2_E,
                                                           NEG_INF)


def _bias_tiles(rel_bias):
    t = ATTN_TILE
    return pl.pallas_call(
        _bias_kernel,
        grid=(DA_HEADS,),
        in_specs=[pl.BlockSpec(memory_space=pltpu.SMEM)],
        out_specs=pl.BlockSpec((None, 2, t, 2 * t), lambda i: (i, 0, 0, 0)),
        out_shape=jax.ShapeDtypeStruct((DA_HEADS, 2, t, 2 * t), F32),
        compiler_params=pltpu.CompilerParams(dimension_semantics=("arbitrary",)),
        name="bias_tiles",
    )(rel_bias)


def _attn_kernel(lam_ref, g_ref, bias_ref, q_ref, k_ref, v_ref, o_ref,
                 qq_ref, s_ref, mx_ref, m_ref, l_ref, acc_ref, *, lam_init, nq):
    t = ATTN_TILE
    lane = lax.broadcasted_iota(jnp.int32, (t, LANES), 1)
    for qi in range(nq):
        q = q_ref[qi * t:(qi + 1) * t, :]
        zero = jnp.zeros_like(q)
        qq_ref[qi, 0:t, :] = jnp.where(lane < DA_HEAD_DIM, q, zero)
        qq_ref[qi, t:2 * t, :] = jnp.where(lane >= DA_HEAD_DIM, q, zero)

    lp = lam_ref[...]
    lam = (jnp.exp(jnp.sum(lp[0:1] * lp[1:2], axis=1, keepdims=True))
           - jnp.exp(jnp.sum(lp[2:3] * lp[3:4], axis=1, keepdims=True)) + lam_init)
    gain = g_ref[...] * (1.0 - lam_init)

    m_ref[...] = jnp.full(m_ref.shape, NEG_INF, F32)
    l_ref[...] = jnp.zeros(l_ref.shape, F32)
    acc_ref[...] = jnp.zeros(acc_ref.shape, F32)

    def key_rows(kj):
        return pl.ds(kj * t, t) if isinstance(kj, int) else pl.ds(pl.multiple_of(kj * t, t), t)

    def scores(qi, kj, off, buf):
        s = lax.dot_general(k_ref[key_rows(kj), :], qq_ref[qi], (((1,), (1,)), ((), ())),
                            preferred_element_type=F32)
        if off is not None:
            s = s + bias_ref[off]
        s_ref[buf] = s
        mx_ref[buf] = jnp.max(s, axis=0, keepdims=True)

    def consume(qi, kj, off, buf):
        m_old = m_ref[qi]
        m_new = jnp.maximum(m_old, mx_ref[buf])
        alpha = jnp.exp2(m_old - m_new)
        p = jnp.exp2(s_ref[buf] - m_new)
        l_ref[qi] = alpha * l_ref[qi] + jnp.sum(p, axis=0, keepdims=True)
        pv = lax.dot_general(v_ref[key_rows(kj), :], p.astype(BF16),
                             (((0,), (0,)), ((), ())), preferred_element_type=F32)
        acc_ref[qi] = alpha * acc_ref[qi] + pv
        m_ref[qi] = m_new

    def sweep(n_steps, coords):
        if n_steps == 0:
            return
        scores(*coords(0), 0)

        def pair(i, carry):
            scores(*coords(2 * i + 1), 1)
            consume(*coords(2 * i), 0)
            scores(*coords(2 * i + 2), 0)
            consume(*coords(2 * i + 1), 1)
            return carry

        lax.fori_loop(0, (n_steps - 1) // 2, pair, 0)
        if n_steps % 2 == 1:
            consume(*coords(n_steps - 1), 0)
        else:
            scores(*coords(n_steps - 1), 1)
            consume(*coords(n_steps - 2), 0)
            consume(*coords(n_steps - 1), 1)

    def biased_coords(n):
        if isinstance(n, int):
            qi, off = (n + 1) // 2, n % 2
        else:
            qi, off = lax.shift_right_logical(n + 1, 1), lax.bitwise_and(n, 1)
        return qi, qi - off, off

    def far_coords(n):
        first = lambda qi: (qi - 1) * (qi - 2) // 2
        if isinstance(n, int):
            qi = max(c for c in range(2, nq) if first(c) <= n)
            return qi, n - first(qi), None
        qi = 2
        for c in range(3, nq):
            qi = qi + (n >= first(c)).astype(jnp.int32)
        return qi, n - lax.shift_right_logical((qi - 1) * (qi - 2), 1), None

    sweep(2 * nq - 1, biased_coords)
    sweep((nq - 1) * (nq - 2) // 2, far_coords)

    for qi in range(nq):
        on = acc_ref[qi] / l_ref[qi]
        o = on[:, 0:t] - lam * on[:, t:2 * t]
        y = o * lax.rsqrt(jnp.mean(o * o, axis=0, keepdims=True) + SUBLN_EPS) * gain
        o_ref[:, qi * t:(qi + 1) * t] = y.astype(o_ref.dtype)


def _attention(zq3, lam_params, subln_g, bias, l, lam_init):
    b, s, _ = zq3.shape
    t = ATTN_TILE
    nq = s // t
    return pl.pallas_call(
        functools.partial(_attn_kernel, lam_init=lam_init, nq=nq),
        grid=(b, DA_HEADS),
        in_specs=[
            pl.BlockSpec((None, 4, DA_HEAD_DIM), lambda bi, h: (l, 0, 0)),
            pl.BlockSpec((None, DA_VDIM, 1), lambda bi, h: (l, 0, 0)),
            pl.BlockSpec((None, 2, t, 2 * t), lambda bi, h: (h, 0, 0, 0)),
            pl.BlockSpec((None, s, LANES), lambda bi, h: (bi, 0, h)),
            pl.BlockSpec((None, s, LANES), lambda bi, h: (bi, 0, DA_HEADS + h)),
            pl.BlockSpec((None, s, LANES), lambda bi, h: (bi, 0, 2 * DA_HEADS + h)),
        ],
        out_specs=pl.BlockSpec((None, DA_VDIM, s), lambda bi, h: (bi, h, 0)),
        out_shape=jax.ShapeDtypeStruct((b, DA_WIDTH, s), BF16),
        scratch_shapes=[
            pltpu.VMEM((nq, 2 * t, LANES), BF16),
            pltpu.VMEM((2, t, 2 * t), F32),
            pltpu.VMEM((2, 1, 2 * t), F32),
            pltpu.VMEM((nq, 1, 2 * t), F32),
            pltpu.VMEM((nq, 1, 2 * t), F32),
            pltpu.VMEM((nq, DA_VDIM, 2 * t), F32),
        ],
        compiler_params=pltpu.CompilerParams(
            dimension_semantics=("arbitrary", "arbitrary"),
            vmem_limit_bytes=VMEM_LIMIT_BYTES),
        name="diff_attn",
    )(lam_params, subln_g, bias, zq3, zq3, zq3)


def _shift_rows(x, halo, j):
    r = pltpu.roll(x, j, 0)
    head_row = lax.broadcasted_iota(jnp.int32, halo.shape, 0)
    head = jnp.where(head_row < j, pltpu.roll(halo, j, 0), r[:SUBLANES])
    return jnp.concatenate([head, r[SUBLANES:]], axis=0)


def _gelu_tanh(x):
    return 0.5 * x * (1.0 + jnp.tanh(math.sqrt(2.0 / math.pi) * (x + 0.044715 * (x * x * x))))


def _mix_kernel(z_ref, scw_ref, lcw_ref, lcb_ref, wa_ref, ba_ref, wx_ref, bx_ref, lam_ref,
                o_ref, hc_ref, halo_sc_ref, halo_lx_ref):
    tc = MIX_TILE
    w = SC_WIDTH

    @pl.when(pl.program_id(1) == 0)
    def _():
        hc_ref[...] = jnp.zeros(hc_ref.shape, F32)
        halo_sc_ref[...] = jnp.zeros(halo_sc_ref.shape, F32)
        halo_lx_ref[...] = jnp.zeros(halo_lx_ref.shape, F32)

    row = lax.broadcasted_iota(jnp.int32, (tc, w), 0)

    sc_b = z_ref[:, 0:w]
    cx = z_ref[:, w:2 * w] * z_ref[:, 2 * w:3 * w]
    halo = halo_sc_ref[...]
    conv = scw_ref[SC_KERNEL - 1:SC_KERNEL, :] * cx
    for j in range(1, SC_KERNEL):
        conv = conv + scw_ref[SC_KERNEL - 1 - j:SC_KERNEL - j, :] * _shift_rows(cx, halo, j)
    halo_sc_ref[...] = cx[tc - SUBLANES:, :]
    o_ref[:, 0:w] = (sc_b * conv).astype(o_ref.dtype)

    lx = z_ref[:, 3 * w:4 * w]
    lg = z_ref[:, 4 * w:5 * w]
    halo = halo_lx_ref[...]
    xr = lcw_ref[LRU_CONV - 1:LRU_CONV, :] * lx + lcb_ref[...]
    for j in range(1, LRU_CONV):
        xr = xr + lcw_ref[LRU_CONV - 1 - j:LRU_CONV - j, :] * _shift_rows(lx, halo, j)
    halo_lx_ref[...] = lx[tc - SUBLANES:, :]

    xb = xr.astype(BF16)
    r = jax.nn.sigmoid(jnp.dot(xb, wa_ref[...], preferred_element_type=F32) + ba_ref[...])
    i = jax.nn.sigmoid(jnp.dot(xb, wx_ref[...], preferred_element_type=F32) + bx_ref[...])
    nl = -lam_ref[...]
    softplus = jnp.maximum(nl, 0.0) + jnp.log1p(jnp.exp(-jnp.abs(nl)))
    log_a = (-LRU_C) * r * softplus
    a = jnp.exp(log_a)
    bv = jnp.sqrt(-jnp.tanh(log_a) * (a * a + 1.0)) * (i * xr)

    s = 1
    while s < tc:
        keep = row >= s
        a_sh = jnp.where(keep, pltpu.roll(a, s, 0), 1.0)
        b_sh = jnp.where(keep, pltpu.roll(bv, s, 0), 0.0)
        bv = a * b_sh + bv
        a = a * a_sh
        s *= 2
    h = bv + a * hc_ref[0:1, :]
    hc_ref[0:1, :] = h[tc - 1:tc, :]
    o_ref[:, w:2 * w] = (_gelu_tanh(lg) * h).astype(o_ref.dtype)


def _mix(zr3, scw, lcw, lcb, wa, ba, wx, bx, lam, l):
    b, s, _ = zr3.shape
    w = SC_WIDTH
    par = lambda rows: pl.BlockSpec((None, rows, w), lambda bi, ti: (l, 0, 0))
    return pl.pallas_call(
        _mix_kernel,
        grid=(b, s // MIX_TILE),
        in_specs=[
            pl.BlockSpec((None, MIX_TILE, REST_WIDTH), lambda bi, ti: (bi, ti, 0)),
            par(SC_KERNEL), par(LRU_CONV), par(1), par(w), par(1), par(w), par(1), par(1),
        ],
        out_specs=pl.BlockSpec((None, MIX_TILE, 2 * w), lambda bi, ti: (bi, ti, 0)),
        out_shape=jax.ShapeDtypeStruct((b, s, 2 * w), BF16),
        scratch_shapes=[pltpu.VMEM((SUBLANES, w), F32)] * 3,
        compiler_params=pltpu.CompilerParams(
            dimension_semantics=("arbitrary", "arbitrary"),
            vmem_limit_bytes=VMEM_LIMIT_BYTES),
        name="conv_lru",
    )(zr3, scw, lcw, lcb, wa, ba, wx, bx, lam)


def _block_diag(w):
    depth, nb, blk, _ = w.shape
    eye = jnp.eye(nb, dtype=w.dtype)
    return jnp.einsum('lnij,nm->lnimj', w, eye).reshape(depth, nb * blk, nb * blk)


def kernel(x, rel_bias, ffn1_norm, ffn1_gate, ffn1_up, ffn1_down, mix_norm, w_in, w_out, lam_q1, lam_k1, lam_q2, lam_k2, subln_gain, sc_conv_w, lru_conv_w, lru_conv_b, lru_wa, lru_ba, lru_wx, lru_bx, lru_lambda, ffn2_norm, ffn2_gate, ffn2_up, ffn2_down, final_norm):
    b, s, d = x.shape
    depth = w_in.shape[0]
    assert d == D_MODEL and s % ATTN_TILE == 0 and s % MIX_TILE == 0 and (b * s) % ROW_TILE == 0

    bf = lambda w: w.astype(BF16)
    vec = lambda v: v.reshape(depth, 1, v.shape[-1])
    wg1, wu1, wd1 = bf(ffn1_gate), bf(ffn1_up), bf(ffn1_down)
    wg2, wu2, wd2 = bf(ffn2_gate), bf(ffn2_up), bf(ffn2_down)
    col_scale = jnp.where(jnp.arange(QKV_WIDTH) < DA_WIDTH, Q_SCALE, 1.0).astype(F32)
    wq, wr = bf(w_in[:, :, :QKV_WIDTH] * col_scale), bf(w_in[:, :, QKV_WIDTH:])
    woa, wom = bf(w_out[:, :DA_WIDTH]), bf(w_out[:, DA_WIDTH:])
    wa, wx = bf(_block_diag(lru_wa)), bf(_block_diag(lru_wx))
    lam_params = jnp.stack([lam_q1, lam_k1, lam_q2, lam_k2], axis=1)
    g1, gm, g2 = vec(ffn1_norm), vec(mix_norm), vec(ffn2_norm)
    gs = subln_gain.reshape(depth, DA_VDIM, 1)
    lcb, lam = vec(lru_conv_b), vec(lru_lambda)
    ba, bx = vec(lru_ba.reshape(depth, -1)), vec(lru_bx.reshape(depth, -1))
    gf = final_norm.reshape(1, d)

    bias = _bias_tiles(rel_bias)

    x2 = x.reshape(b * s, d)
    for l in range(depth):
        lam_init = 0.8 - 0.6 * math.exp(-0.3 * l)
        h2, zq, zr = _ffn_in(x2, g1, wg1, wu1, wd1, gm, wq, wr, l)
        ya = _attention(zq.reshape(b, s, QKV_WIDTH), lam_params, gs, bias, l, lam_init)
        ym = _mix(zr.reshape(b, s, REST_WIDTH), sc_conv_w, lru_conv_w, lcb, wa, ba, wx, bx, lam, l)
        x2 = _out_ffn(h2, ya, ym.reshape(b * s, 2 * SC_WIDTH),
                      woa, wom, g2, wg2, wu2, wd2, gf, l, l == depth - 1)
    return x2.reshape(b, s, d)
```

```python
import functools
import math

import jax
import jax.numpy as jnp
from jax import lax
from jax.experimental import pallas as pl
from jax.experimental.pallas import tpu as pltpu

F32 = jnp.float32
BF16 = jnp.bfloat16

D_MODEL = 1024
D_FF = 2816
DA_WIDTH = 512
SC_WIDTH = 256
LRU_WIDTH = 256
DA_HEAD_DIM = 64
DA_HEADS = 4
DA_VDIM = 2 * DA_HEAD_DIM
NUM_BUCKETS = 32
MAX_DISTANCE = 128
SUBLN_EPS = 1e-5
SC_KERNEL = 3
LRU_BLOCKS = 4
LRU_BLOCK = 64
LRU_CONV = 4
LRU_C = 8.0
RMS_EPS = 1e-6
NEG_INF = -1e30
QKV_WIDTH = 3 * DA_WIDTH
REST_WIDTH = 3 * SC_WIDTH + 2 * LRU_WIDTH
IN_WIDTH = QKV_WIDTH + REST_WIDTH
LOG2_E = math.log2(math.e)
Q_SCALE = DA_HEAD_DIM ** -0.5 * LOG2_E

LANES = 128
SUBLANES = 8
BF16_SUBLANES = 16
VT_ROWS = DA_VDIM + BF16_SUBLANES
VMEM_LIMIT_BYTES = 56 * 1024 * 1024
ROW_TILE = 256
ATTN_TILE = 512
SWEEP_UNROLL = 4
MIX_TILE = 256


def _rms(x, g, eps):
    return x * lax.rsqrt(jnp.mean(x * x, axis=-1, keepdims=True) + eps) * g


def _swiglu(xn, wg_ref, wu_ref, wd_ref):
    g = jnp.dot(xn, wg_ref[...], preferred_element_type=F32)
    u = jnp.dot(xn, wu_ref[...], preferred_element_type=F32)
    a = (g * jax.nn.sigmoid(g) * u).astype(BF16)
    return jnp.dot(a, wd_ref[...], preferred_element_type=F32)


def _ffn_in_kernel(x_ref, g1_ref, wg_ref, wu_ref, wd_ref, gm_ref, wq_ref, wr_ref,
                   h_ref, zq_ref, zr_ref):
    x = x_ref[...]
    xn = _rms(x, g1_ref[...], RMS_EPS).astype(BF16)
    h = x + 0.5 * _swiglu(xn, wg_ref, wu_ref, wd_ref)
    h_ref[...] = h
    u = _rms(h, gm_ref[...], RMS_EPS).astype(BF16)
    zq_ref[...] = jnp.dot(u, wq_ref[...], preferred_element_type=F32).astype(BF16)
    zr_ref[...] = jnp.dot(u, wr_ref[...], preferred_element_type=F32)


def _resident(shape, index):
    return pl.BlockSpec(shape, index, pipeline_mode=pl.Buffered(1))


def _ffn_in(x2, g1, wg, wu, wd, gm, wq, wr, l):
    n = x2.shape[0]
    row = lambda w: pl.BlockSpec((ROW_TILE, w), lambda i: (i, 0))
    vec = pl.BlockSpec((None, 1, D_MODEL), lambda i: (l, 0, 0))
    return pl.pallas_call(
        _ffn_in_kernel,
        grid=(n // ROW_TILE,),
        in_specs=[
            row(D_MODEL), vec,
            _resident((None, D_MODEL, D_FF), lambda i: (l, 0, 0)),
            _resident((None, D_MODEL, D_FF), lambda i: (l, 0, 0)),
            _resident((None, D_FF, D_MODEL), lambda i: (l, 0, 0)),
            vec,
            _resident((None, D_MODEL, QKV_WIDTH), lambda i: (l, 0, 0)),
            _resident((None, D_MODEL, REST_WIDTH), lambda i: (l, 0, 0)),
        ],
        out_specs=[row(D_MODEL), row(QKV_WIDTH), row(REST_WIDTH)],
        out_shape=[
            jax.ShapeDtypeStruct((n, D_MODEL), F32),
            jax.ShapeDtypeStruct((n, QKV_WIDTH), BF16),
            jax.ShapeDtypeStruct((n, REST_WIDTH), F32),
        ],
        compiler_params=pltpu.CompilerParams(
            dimension_semantics=("arbitrary",), vmem_limit_bytes=VMEM_LIMIT_BYTES),
        name="ffn_in",
    )(x2, g1, wg, wu, wd, gm, wq, wr)


def _out_ffn_kernel(h_ref, ya_ref, ym_ref, woa_ref, wom_ref, g2_ref, wg_ref, wu_ref, wd_ref,
                    gf_ref, o_ref, *, final_norm):
    h = (h_ref[...]
         + lax.dot_general(ya_ref[...], woa_ref[...], (((0,), (0,)), ((), ())),
                           preferred_element_type=F32)
         + jnp.dot(ym_ref[...], wom_ref[...], preferred_element_type=F32))
    hn = _rms(h, g2_ref[...], RMS_EPS).astype(BF16)
    x = h + 0.5 * _swiglu(hn, wg_ref, wu_ref, wd_ref)
    if final_norm:
        x = _rms(x, gf_ref[...], RMS_EPS)
    o_ref[...] = x


def _out_ffn(h2, ya_t, ym, woa, wom, g2, wg, wu, wd, gf, l, final_norm):
    n = h2.shape[0]
    tiles_per_seq = ya_t.shape[2] // ROW_TILE
    row = lambda w: pl.BlockSpec((ROW_TILE, w), lambda i: (i, 0))
    mix_half = SC_WIDTH + LRU_WIDTH
    return pl.pallas_call(
        functools.partial(_out_ffn_kernel, final_norm=final_norm),
        grid=(n // ROW_TILE,),
        in_specs=[
            row(D_MODEL),
            pl.BlockSpec((None, DA_WIDTH, ROW_TILE),
                         lambda i: (i // tiles_per_seq, 0, i % tiles_per_seq)),
            row(mix_half),
            _resident((None, DA_WIDTH, D_MODEL), lambda i: (l, 0, 0)),
            _resident((None, mix_half, D_MODEL), lambda i: (l, 0, 0)),
            pl.BlockSpec((None, 1, D_MODEL), lambda i: (l, 0, 0)),
            _resident((None, D_MODEL, D_FF), lambda i: (l, 0, 0)),
            _resident((None, D_MODEL, D_FF), lambda i: (l, 0, 0)),
            _resident((None, D_FF, D_MODEL), lambda i: (l, 0, 0)),
            pl.BlockSpec((1, D_MODEL), lambda i: (0, 0)),
        ],
        out_specs=row(D_MODEL),
        out_shape=jax.ShapeDtypeStruct((n, D_MODEL), F32),
        compiler_params=pltpu.CompilerParams(
            dimension_semantics=("arbitrary",), vmem_limit_bytes=VMEM_LIMIT_BYTES),
        name="out_ffn",
    )(h2, ya_t, ym, woa, wom, g2, wg, wu, wd, gf)


def _bias_kernel(rb_ref, o_ref):
    h = pl.program_id(0)
    t = ATTN_TILE
    max_exact = NUM_BUCKETS // 2
    key = lax.broadcasted_iota(jnp.int32, (t, t), 0)
    qry = lax.broadcasted_iota(jnp.int32, (t, t), 1)
    for off in range(2):
        dist = qry - key + off * t
        n = jnp.maximum(dist, 0)
        nf = jnp.maximum(n, 1).astype(F32)
        large = max_exact + (jnp.log(nf / max_exact) / math.log(MAX_DISTANCE / max_exact)
                             * (NUM_BUCKETS - max_exact)).astype(jnp.int32)
        large = jnp.minimum(large, NUM_BUCKETS - 1)
        bucket = jnp.where(n < max_exact, n, large)
        for mp in range(2):
            hm = 2 * h + mp
            val = jnp.zeros((t, t), F32)
            for j in range(NUM_BUCKETS):
                val = jnp.where(bucket == j, rb_ref[j, hm], val)
            far = rb_ref[NUM_BUCKETS - 1, hm]
            o_ref[off, :, mp * t:(mp + 1) * t] = jnp.where(dist >= 0, (val - far) * LOG2_E,
                                                           NEG_INF)


def _bias_tiles(rel_bias):
    t = ATTN_TILE
    return pl.pallas_call(
        _bias_kernel,
        grid=(DA_HEADS,),
        in_specs=[pl.BlockSpec(memory_space=pltpu.SMEM)],
        out_specs=pl.BlockSpec((None, 2, t, 2 * t), lambda i: (i, 0, 0, 0)),
        out_shape=jax.ShapeDtypeStruct((DA_HEADS, 2, t, 2 * t), F32),
        compiler_params=pltpu.CompilerParams(dimension_semantics=("arbitrary",)),
        name="bias_tiles",
    )(rel_bias)


def _attn_kernel(lam_ref, g_ref, bias_ref, q_ref, k_ref, v_ref, o_ref,
                 qq_ref, vt_ref, s_ref, mx_ref, m_ref, acc_ref, *, lam_init, nq):
    t = ATTN_TILE
    lane = lax.broadcasted_iota(jnp.int32, (t, LANES), 1)
    for qi in range(nq):
        q = q_ref[qi * t:(qi + 1) * t, :]
        zero = jnp.zeros_like(q)
        qq_ref[qi, 0:t, :] = jnp.where(lane < DA_HEAD_DIM, q, zero)
        qq_ref[qi, t:2 * t, :] = jnp.where(lane >= DA_HEAD_DIM, q, zero)

    lp = lam_ref[...]
    lam = (jnp.exp(jnp.sum(lp[0:1] * lp[1:2], axis=1, keepdims=True))
           - jnp.exp(jnp.sum(lp[2:3] * lp[3:4], axis=1, keepdims=True)) + lam_init)
    gain = g_ref[...] * (1.0 - lam_init)

    ones_row = lax.broadcasted_iota(jnp.int32, (VT_ROWS - DA_VDIM, t), 0) == 0
    for kj in range(nq):
        vt_ref[kj, 0:DA_VDIM, :] = v_ref[kj * t:(kj + 1) * t, :].T
        vt_ref[kj, DA_VDIM:VT_ROWS, :] = jnp.where(ones_row, 1.0, 0.0).astype(BF16)

    m_ref[...] = jnp.full(m_ref.shape, NEG_INF, F32)
    acc_ref[...] = jnp.zeros(acc_ref.shape, F32)

    def key_rows(kj):
        return pl.ds(kj * t, t) if isinstance(kj, int) else pl.ds(pl.multiple_of(kj * t, t), t)

    def scores(qi, kj, off, buf):
        s = lax.dot_general(k_ref[key_rows(kj), :], qq_ref[qi], (((1,), (1,)), ((), ())),
                            preferred_element_type=F32)
        if off is not None:
            s = s + bias_ref[off]
        s_ref[buf] = s
        mx_ref[buf] = jnp.max(s, axis=0, keepdims=True)

    def consume(qi, kj, off, buf):
        m_old = m_ref[qi]
        m_new = jnp.maximum(m_old, mx_ref[buf])
        alpha = jnp.exp2(m_old - m_new)
        p = jnp.exp2(s_ref[buf] - m_new)
        pv = jnp.dot(vt_ref[kj], p.astype(BF16), preferred_element_type=F32)
        acc_ref[qi] = alpha * acc_ref[qi] + pv
        m_ref[qi] = m_new

    def sweep(n_steps, coords):
        if n_steps == 0:
            return
        scores(*coords(0), 0)

        def group(i, carry):
            for u in range(SWEEP_UNROLL):
                n = SWEEP_UNROLL * i + u
                scores(*coords(n + 1), (u + 1) % 2)
                consume(*coords(n), u % 2)
            return carry

        n_groups = (n_steps - 1) // SWEEP_UNROLL
        lax.fori_loop(0, n_groups, group, 0)
        for n in range(n_groups * SWEEP_UNROLL, n_steps):
            if n + 1 < n_steps:
                scores(*coords(n + 1), (n + 1) % 2)
            consume(*coords(n), n % 2)

    def biased_coords(n):
        if isinstance(n, int):
            qi, off = (n + 1) // 2, n % 2
        else:
            qi, off = lax.shift_right_logical(n + 1, 1), lax.bitwise_and(n, 1)
        return qi, qi - off, off

    def far_coords(n):
        first = lambda qi: (qi - 1) * (qi - 2) // 2
        if isinstance(n, int):
            qi = max(c for c in range(2, nq) if first(c) <= n)
            return qi, n - first(qi), None
        qi = 2
        for c in range(3, nq):
            qi = qi + (n >= first(c)).astype(jnp.int32)
        return qi, n - lax.shift_right_logical((qi - 1) * (qi - 2), 1), None

    sweep(2 * nq - 1, biased_coords)
    sweep((nq - 1) * (nq - 2) // 2, far_coords)

    for qi in range(nq):
        on = acc_ref[qi, 0:DA_VDIM, :] / acc_ref[qi, DA_VDIM:DA_VDIM + 1, :]
        o = on[:, 0:t] - lam * on[:, t:2 * t]
        y = o * lax.rsqrt(jnp.mean(o * o, axis=0, keepdims=True) + SUBLN_EPS) * gain
        o_ref[:, qi * t:(qi + 1) * t] = y.astype(o_ref.dtype)


def _attention(zq3, lam_params, subln_g, bias, l, lam_init):
    b, s, _ = zq3.shape
    t = ATTN_TILE
    nq = s // t
    return pl.pallas_call(
        functools.partial(_attn_kernel, lam_init=lam_init, nq=nq),
        grid=(b, DA_HEADS),
        in_specs=[
            pl.BlockSpec((None, 4, DA_HEAD_DIM), lambda bi, h: (l, 0, 0)),
            pl.BlockSpec((None, DA_VDIM, 1), lambda bi, h: (l, 0, 0)),
            pl.BlockSpec((None, 2, t, 2 * t), lambda bi, h: (h, 0, 0, 0)),
            pl.BlockSpec((None, s, LANES), lambda bi, h: (bi, 0, h)),
            pl.BlockSpec((None, s, LANES), lambda bi, h: (bi, 0, DA_HEADS + h)),
            pl.BlockSpec((None, s, LANES), lambda bi, h: (bi, 0, 2 * DA_HEADS + h)),
        ],
        out_specs=pl.BlockSpec((None, DA_VDIM, s), lambda bi, h: (bi, h, 0)),
        out_shape=jax.ShapeDtypeStruct((b, DA_WIDTH, s), BF16),
        scratch_shapes=[
            pltpu.VMEM((nq, 2 * t, LANES), BF16),
            pltpu.VMEM((nq, VT_ROWS, t), BF16),
            pltpu.VMEM((2, t, 2 * t), F32),
            pltpu.VMEM((2, 1, 2 * t), F32),
            pltpu.VMEM((nq, 1, 2 * t), F32),
            pltpu.VMEM((nq, VT_ROWS, 2 * t), F32),
        ],
        compiler_params=pltpu.CompilerParams(
            dimension_semantics=("arbitrary", "arbitrary"),
            vmem_limit_bytes=VMEM_LIMIT_BYTES),
        name="diff_attn",
    )(lam_params, subln_g, bias, zq3, zq3, zq3)


def _shift_rows(x, halo, j):
    r = pltpu.roll(x, j, 0)
    head_row = lax.broadcasted_iota(jnp.int32, halo.shape, 0)
    head = jnp.where(head_row < j, pltpu.roll(halo, j, 0), r[:SUBLANES])
    return jnp.concatenate([head, r[SUBLANES:]], axis=0)


def _gelu_tanh(x):
    return 0.5 * x * (1.0 + jnp.tanh(math.sqrt(2.0 / math.pi) * (x + 0.044715 * (x * x * x))))


def _mix_kernel(z_ref, scw_ref, lcw_ref, lcb_ref, wa_ref, ba_ref, wx_ref, bx_ref, lam_ref,
                o_ref, hc_ref, halo_sc_ref, halo_lx_ref):
    tc = MIX_TILE
    w = SC_WIDTH

    @pl.when(pl.program_id(1) == 0)
    def _():
        hc_ref[...] = jnp.zeros(hc_ref.shape, F32)
        halo_sc_ref[...] = jnp.zeros(halo_sc_ref.shape, F32)
        halo_lx_ref[...] = jnp.zeros(halo_lx_ref.shape, F32)

    row = lax.broadcasted_iota(jnp.int32, (tc, w), 0)

    sc_b = z_ref[:, 0:w]
    cx = z_ref[:, w:2 * w] * z_ref[:, 2 * w:3 * w]
    halo = halo_sc_ref[...]
    conv = scw_ref[SC_KERNEL - 1:SC_KERNEL, :] * cx
    for j in range(1, SC_KERNEL):
        conv = conv + scw_ref[SC_KERNEL - 1 - j:SC_KERNEL - j, :] * _shift_rows(cx, halo, j)
    halo_sc_ref[...] = cx[tc - SUBLANES:, :]
    o_ref[:, 0:w] = (sc_b * conv).astype(o_ref.dtype)

    lx = z_ref[:, 3 * w:4 * w]
    lg = z_ref[:, 4 * w:5 * w]
    halo = halo_lx_ref[...]
    xr = lcw_ref[LRU_CONV - 1:LRU_CONV, :] * lx + lcb_ref[...]
    for j in range(1, LRU_CONV):
        xr = xr + lcw_ref[LRU_CONV - 1 - j:LRU_CONV - j, :] * _shift_rows(lx, halo, j)
    halo_lx_ref[...] = lx[tc - SUBLANES:, :]

    xb = xr.astype(BF16)
    r = jax.nn.sigmoid(jnp.dot(xb, wa_ref[...], preferred_element_type=F32) + ba_ref[...])
    i = jax.nn.sigmoid(jnp.dot(xb, wx_ref[...], preferred_element_type=F32) + bx_ref[...])
    nl = -lam_ref[...]
    softplus = jnp.maximum(nl, 0.0) + jnp.log1p(jnp.exp(-jnp.abs(nl)))
    log_a = (-LRU_C) * r * softplus
    a = jnp.exp(log_a)
    bv = jnp.sqrt(-jnp.tanh(log_a) * (a * a + 1.0)) * (i * xr)

    s = 1
    while s < tc:
        keep = row >= s
        a_sh = jnp.where(keep, pltpu.roll(a, s, 0), 1.0)
        b_sh = jnp.where(keep, pltpu.roll(bv, s, 0), 0.0)
        bv = a * b_sh + bv
        a = a * a_sh
        s *= 2
    h = bv + a * hc_ref[0:1, :]
    hc_ref[0:1, :] = h[tc - 1:tc, :]
    o_ref[:, w:2 * w] = (_gelu_tanh(lg) * h).astype(o_ref.dtype)


def _mix(zr3, scw, lcw, lcb, wa, ba, wx, bx, lam, l):
    b, s, _ = zr3.shape
    w = SC_WIDTH
    par = lambda rows: pl.BlockSpec((None, rows, w), lambda bi, ti: (l, 0, 0))
    return pl.pallas_call(
        _mix_kernel,
        grid=(b, s // MIX_TILE),
        in_specs=[
            pl.BlockSpec((None, MIX_TILE, REST_WIDTH), lambda bi, ti: (bi, ti, 0)),
            par(SC_KERNEL), par(LRU_CONV), par(1), par(w), par(1), par(w), par(1), par(1),
        ],
        out_specs=pl.BlockSpec((None, MIX_TILE, 2 * w), lambda bi, ti: (bi, ti, 0)),
        out_shape=jax.ShapeDtypeStruct((b, s, 2 * w), BF16),
        scratch_shapes=[pltpu.VMEM((SUBLANES, w), F32)] * 3,
        compiler_params=pltpu.CompilerParams(
            dimension_semantics=("arbitrary", "arbitrary"),
            vmem_limit_bytes=VMEM_LIMIT_BYTES),
        name="conv_lru",
    )(zr3, scw, lcw, lcb, wa, ba, wx, bx, lam)


def _block_diag(w):
    depth, nb, blk, _ = w.shape
    eye = jnp.eye(nb, dtype=w.dtype)
    return jnp.einsum('lnij,nm->lnimj', w, eye).reshape(depth, nb * blk, nb * blk)


def kernel(x, rel_bias, ffn1_norm, ffn1_gate, ffn1_up, ffn1_down, mix_norm, w_in, w_out, lam_q1, lam_k1, lam_q2, lam_k2, subln_gain, sc_conv_w, lru_conv_w, lru_conv_b, lru_wa, lru_ba, lru_wx, lru_bx, lru_lambda, ffn2_norm, ffn2_gate, ffn2_up, ffn2_down, final_norm):
    b, s, d = x.shape
    depth = w_in.shape[0]
    assert d == D_MODEL and s % ATTN_TILE == 0 and s % MIX_TILE == 0 and (b * s) % ROW_TILE == 0

    bf = lambda w: w.astype(BF16)
    vec = lambda v: v.reshape(depth, 1, v.shape[-1])
    wg1, wu1, wd1 = bf(ffn1_gate), bf(ffn1_up), bf(ffn1_down)
    wg2, wu2, wd2 = bf(ffn2_gate), bf(ffn2_up), bf(ffn2_down)
    col_scale = jnp.where(jnp.arange(QKV_WIDTH) < DA_WIDTH, Q_SCALE, 1.0).astype(F32)
    wq, wr = bf(w_in[:, :, :QKV_WIDTH] * col_scale), bf(w_in[:, :, QKV_WIDTH:])
    woa, wom = bf(w_out[:, :DA_WIDTH]), bf(w_out[:, DA_WIDTH:])
    wa, wx = bf(_block_diag(lru_wa)), bf(_block_diag(lru_wx))
    lam_params = jnp.stack([lam_q1, lam_k1, lam_q2, lam_k2], axis=1)
    g1, gm, g2 = vec(ffn1_norm), vec(mix_norm), vec(ffn2_norm)
    gs = subln_gain.reshape(depth, DA_VDIM, 1)
    lcb, lam = vec(lru_conv_b), vec(lru_lambda)
    ba, bx = vec(lru_ba.reshape(depth, -1)), vec(lru_bx.reshape(depth, -1))
    gf = final_norm.reshape(1, d)

    bias = _bias_tiles(rel_bias)

    x2 = x.reshape(b * s, d)
    for l in range(depth):
        lam_init = 0.8 - 0.6 * math.exp(-0.3 * l)
        h2, zq, zr = _ffn_in(x2, g1, wg1, wu1, wd1, gm, wq, wr, l)
        ya = _attention(zq.reshape(b, s, QKV_WIDTH), lam_params, gs, bias, l, lam_init)
        ym = _mix(zr.reshape(b, s, REST_WIDTH), sc_conv_w, lru_conv_w, lcb, wa, ba, wx, bx, lam, l)
        x2 = _out_ffn(h2, ya, ym.reshape(b * s, 2 * SC_WIDTH),
                      woa, wom, g2, wg2, wu2, wd2, gf, l, l == depth - 1)
    return x2.reshape(b, s, d)
```

```python
import functools
import math
from typing import Any, NamedTuple

import jax
import jax.numpy as jnp
from jax import lax
from jax.experimental import pallas as pl
from jax.experimental.pallas import tpu as pltpu

F32 = jnp.float32
BF16 = jnp.bfloat16

D_MODEL = 1024
D_FF = 2816
DA_WIDTH = 512
SC_WIDTH = 256
LRU_WIDTH = 256
DA_HEAD_DIM = 64
DA_HEADS = 4
DA_VDIM = 2 * DA_HEAD_DIM
NUM_BUCKETS = 32
MAX_DISTANCE = 128
SUBLN_EPS = 1e-5
SC_KERNEL = 3
LRU_BLOCKS = 4
LRU_BLOCK = 64
LRU_CONV = 4
LRU_C = 8.0
RMS_EPS = 1e-6
NEG_INF = -1e30
QKV_WIDTH = 3 * DA_WIDTH
REST_WIDTH = 3 * SC_WIDTH + 2 * LRU_WIDTH
IN_WIDTH = QKV_WIDTH + REST_WIDTH
LOG2_E = math.log2(math.e)
Q_SCALE = DA_HEAD_DIM ** -0.5 * LOG2_E

LANES = 128
SUBLANES = 8
BF16_SUBLANES = 16
VT_ROWS = DA_VDIM + BF16_SUBLANES
VMEM_LIMIT_BYTES = 56 * 1024 * 1024
ROW_TILE = 256
ATTN_TILE = 512
SWEEP_UNROLL = 4


def _rms(x, g, eps):
    return x * lax.rsqrt(jnp.mean(x * x, axis=-1, keepdims=True) + eps) * g


def _swiglu(xn, wg_ref, wu_ref, wd_ref):
    g = jnp.dot(xn, wg_ref[...], preferred_element_type=F32)
    u = jnp.dot(xn, wu_ref[...], preferred_element_type=F32)
    a = (g * jax.nn.sigmoid(g) * u).astype(BF16)
    return jnp.dot(a, wd_ref[...], preferred_element_type=F32)


def _ffn_in_kernel(x_ref, g1_ref, wg_ref, wu_ref, wd_ref, gm_ref, wq_ref, wr_ref,
                   h_ref, zq_ref, zr_ref):
    x = x_ref[...]
    xn = _rms(x, g1_ref[...], RMS_EPS).astype(BF16)
    h = x + 0.5 * _swiglu(xn, wg_ref, wu_ref, wd_ref)
    h_ref[...] = h
    u = _rms(h, gm_ref[...], RMS_EPS).astype(BF16)
    zq_ref[...] = jnp.dot(u, wq_ref[...], preferred_element_type=F32).astype(BF16)
    zr_ref[...] = jnp.dot(u, wr_ref[...], preferred_element_type=F32)


def _resident(shape, index):
    return pl.BlockSpec(shape, index, pipeline_mode=pl.Buffered(1))


def _ffn_in(x2, g1, wg, wu, wd, gm, wq, wr, l):
    n = x2.shape[0]
    row = lambda w: pl.BlockSpec((ROW_TILE, w), lambda i: (i, 0))
    vec = pl.BlockSpec((None, 1, D_MODEL), lambda i: (l, 0, 0))
    return pl.pallas_call(
        _ffn_in_kernel,
        grid=(n // ROW_TILE,),
        in_specs=[
            row(D_MODEL), vec,
            _resident((None, D_MODEL, D_FF), lambda i: (l, 0, 0)),
            _resident((None, D_MODEL, D_FF), lambda i: (l, 0, 0)),
            _resident((None, D_FF, D_MODEL), lambda i: (l, 0, 0)),
            vec,
            _resident((None, D_MODEL, QKV_WIDTH), lambda i: (l, 0, 0)),
            _resident((None, D_MODEL, REST_WIDTH), lambda i: (l, 0, 0)),
        ],
        out_specs=[row(D_MODEL), row(QKV_WIDTH), row(REST_WIDTH)],
        out_shape=[
            jax.ShapeDtypeStruct((n, D_MODEL), F32),
            jax.ShapeDtypeStruct((n, QKV_WIDTH), BF16),
            jax.ShapeDtypeStruct((n, REST_WIDTH), F32),
        ],
        compiler_params=pltpu.CompilerParams(
            dimension_semantics=("arbitrary",), vmem_limit_bytes=VMEM_LIMIT_BYTES),
        name="ffn_in",
    )(x2, g1, wg, wu, wd, gm, wq, wr)


def _bias_kernel(rb_ref, o_ref):
    h = pl.program_id(0)
    t = ATTN_TILE
    max_exact = NUM_BUCKETS // 2
    key = lax.broadcasted_iota(jnp.int32, (t, t), 0)
    qry = lax.broadcasted_iota(jnp.int32, (t, t), 1)
    for off in range(2):
        dist = qry - key + off * t
        n = jnp.maximum(dist, 0)
        nf = jnp.maximum(n, 1).astype(F32)
        large = max_exact + (jnp.log(nf / max_exact) / math.log(MAX_DISTANCE / max_exact)
                             * (NUM_BUCKETS - max_exact)).astype(jnp.int32)
        large = jnp.minimum(large, NUM_BUCKETS - 1)
        bucket = jnp.where(n < max_exact, n, large)
        for mp in range(2):
            hm = 2 * h + mp
            val = jnp.zeros((t, t), F32)
            for j in range(NUM_BUCKETS):
                val = jnp.where(bucket == j, rb_ref[j, hm], val)
            far = rb_ref[NUM_BUCKETS - 1, hm]
            o_ref[off, :, mp * t:(mp + 1) * t] = jnp.where(dist >= 0, (val - far) * LOG2_E,
                                                           NEG_INF)


def _bias_tiles(rel_bias):
    t = ATTN_TILE
    return pl.pallas_call(
        _bias_kernel,
        grid=(DA_HEADS,),
        in_specs=[pl.BlockSpec(memory_space=pltpu.SMEM)],
        out_specs=pl.BlockSpec((None, 2, t, 2 * t), lambda i: (i, 0, 0, 0)),
        out_shape=jax.ShapeDtypeStruct((DA_HEADS, 2, t, 2 * t), F32),
        compiler_params=pltpu.CompilerParams(dimension_semantics=("arbitrary",)),
        name="bias_tiles",
    )(rel_bias)


def _attn_kernel(lam_ref, g_ref, bias_ref, q_ref, k_ref, v_ref, o_ref,
                 qq_ref, vt_ref, s_ref, mx_ref, m_ref, acc_ref, *, lam_init, nq):
    t = ATTN_TILE
    lane = lax.broadcasted_iota(jnp.int32, (t, LANES), 1)
    for qi in range(nq):
        q = q_ref[qi * t:(qi + 1) * t, :]
        zero = jnp.zeros_like(q)
        qq_ref[qi, 0:t, :] = jnp.where(lane < DA_HEAD_DIM, q, zero)
        qq_ref[qi, t:2 * t, :] = jnp.where(lane >= DA_HEAD_DIM, q, zero)

    lp = lam_ref[...]
    lam = (jnp.exp(jnp.sum(lp[0:1] * lp[1:2], axis=1, keepdims=True))
           - jnp.exp(jnp.sum(lp[2:3] * lp[3:4], axis=1, keepdims=True)) + lam_init)
    gain = g_ref[...] * (1.0 - lam_init)

    ones_row = lax.broadcasted_iota(jnp.int32, (VT_ROWS - DA_VDIM, t), 0) == 0
    for kj in range(nq):
        vt_ref[kj, 0:DA_VDIM, :] = v_ref[kj * t:(kj + 1) * t, :].T
        vt_ref[kj, DA_VDIM:VT_ROWS, :] = jnp.where(ones_row, 1.0, 0.0).astype(BF16)

    m_ref[...] = jnp.full(m_ref.shape, NEG_INF, F32)
    acc_ref[...] = jnp.zeros(acc_ref.shape, F32)

    def key_rows(kj):
        return pl.ds(kj * t, t) if isinstance(kj, int) else pl.ds(pl.multiple_of(kj * t, t), t)

    def scores(qi, kj, off, buf):
        s = lax.dot_general(k_ref[key_rows(kj), :], qq_ref[qi], (((1,), (1,)), ((), ())),
                            preferred_element_type=F32)
        if off is not None:
            s = s + bias_ref[off]
        s_ref[buf] = s
        mx_ref[buf] = jnp.max(s, axis=0, keepdims=True)

    def consume(qi, kj, off, buf):
        m_old = m_ref[qi]
        m_new = jnp.maximum(m_old, mx_ref[buf])
        alpha = jnp.exp2(m_old - m_new)
        p = jnp.exp2(s_ref[buf] - m_new)
        pv = jnp.dot(vt_ref[kj], p.astype(BF16), preferred_element_type=F32)
        acc_ref[qi] = alpha * acc_ref[qi] + pv
        m_ref[qi] = m_new

    def sweep(n_steps, coords):
        if n_steps == 0:
            return
        scores(*coords(0), 0)

        def group(i, carry):
            for u in range(SWEEP_UNROLL):
                n = SWEEP_UNROLL * i + u
                scores(*coords(n + 1), (u + 1) % 2)
                consume(*coords(n), u % 2)
            return carry

        n_groups = (n_steps - 1) // SWEEP_UNROLL
        lax.fori_loop(0, n_groups, group, 0)
        for n in range(n_groups * SWEEP_UNROLL, n_steps):
            if n + 1 < n_steps:
                scores(*coords(n + 1), (n + 1) % 2)
            consume(*coords(n), n % 2)

    def biased_coords(n):
        if isinstance(n, int):
            qi, off = (n + 1) // 2, n % 2
        else:
            qi, off = lax.shift_right_logical(n + 1, 1), lax.bitwise_and(n, 1)
        return qi, qi - off, off

    def far_coords(n):
        first = lambda qi: (qi - 1) * (qi - 2) // 2
        if isinstance(n, int):
            qi = max(c for c in range(2, nq) if first(c) <= n)
            return qi, n - first(qi), None
        qi = 2
        for c in range(3, nq):
            qi = qi + (n >= first(c)).astype(jnp.int32)
        return qi, n - lax.shift_right_logical((qi - 1) * (qi - 2), 1), None

    sweep(2 * nq - 1, biased_coords)
    sweep((nq - 1) * (nq - 2) // 2, far_coords)

    for qi in range(nq):
        on = acc_ref[qi, 0:DA_VDIM, :] / acc_ref[qi, DA_VDIM:DA_VDIM + 1, :]
        o = on[:, 0:t] - lam * on[:, t:2 * t]
        y = o * lax.rsqrt(jnp.mean(o * o, axis=0, keepdims=True) + SUBLN_EPS) * gain
        o_ref[:, qi * t:(qi + 1) * t] = y.astype(o_ref.dtype)


def _attention(zq3, lam_params, subln_g, bias, l, lam_init):
    b, s, _ = zq3.shape
    t = ATTN_TILE
    nq = s // t
    return pl.pallas_call(
        functools.partial(_attn_kernel, lam_init=lam_init, nq=nq),
        grid=(b, DA_HEADS),
        in_specs=[
            pl.BlockSpec((None, 4, DA_HEAD_DIM), lambda bi, h: (l, 0, 0)),
            pl.BlockSpec((None, DA_VDIM, 1), lambda bi, h: (l, 0, 0)),
            pl.BlockSpec((None, 2, t, 2 * t), lambda bi, h: (h, 0, 0, 0)),
            pl.BlockSpec((None, s, LANES), lambda bi, h: (bi, 0, h)),
            pl.BlockSpec((None, s, LANES), lambda bi, h: (bi, 0, DA_HEADS + h)),
            pl.BlockSpec((None, s, LANES), lambda bi, h: (bi, 0, 2 * DA_HEADS + h)),
        ],
        out_specs=pl.BlockSpec((None, DA_VDIM, s), lambda bi, h: (bi, h, 0)),
        out_shape=jax.ShapeDtypeStruct((b, DA_WIDTH, s), BF16),
        scratch_shapes=[
            pltpu.VMEM((nq, 2 * t, LANES), BF16),
            pltpu.VMEM((nq, VT_ROWS, t), BF16),
            pltpu.VMEM((2, t, 2 * t), F32),
            pltpu.VMEM((2, 1, 2 * t), F32),
            pltpu.VMEM((nq, 1, 2 * t), F32),
            pltpu.VMEM((nq, VT_ROWS, 2 * t), F32),
        ],
        compiler_params=pltpu.CompilerParams(
            dimension_semantics=("arbitrary", "arbitrary"),
            vmem_limit_bytes=VMEM_LIMIT_BYTES),
        name="diff_attn",
    )(lam_params, subln_g, bias, zq3, zq3, zq3)


def _shift_rows(x, halo, j):
    r = pltpu.roll(x, j, 0)
    head_row = lax.broadcasted_iota(jnp.int32, halo.shape, 0)
    head = jnp.where(head_row < j, pltpu.roll(halo, j, 0), r[:SUBLANES])
    return jnp.concatenate([head, r[SUBLANES:]], axis=0)


def _gelu_tanh(x):
    return 0.5 * x * (1.0 + jnp.tanh(math.sqrt(2.0 / math.pi) * (x + 0.044715 * (x * x * x))))


def _scan_step(a, b, s, pos):
    keep = pos >= s
    a_sh = jnp.where(keep, pltpu.roll(a, s, 0), 1.0)
    b_sh = jnp.where(keep, pltpu.roll(b, s, 0), 0.0)
    return a * a_sh, a * b_sh + b


def _mix_tile(z_ref, p, st, reset):
    rows = z_ref.shape[0]
    w = SC_WIDTH
    groups = rows // SUBLANES
    fresh = lambda x: jnp.where(reset, jnp.zeros_like(x), x)

    sc_b = z_ref[:, 0:w]
    cx = z_ref[:, w:2 * w] * z_ref[:, 2 * w:3 * w]
    halo = fresh(st.halo_sc[...])
    conv = p.scw[SC_KERNEL - 1:SC_KERNEL, :] * cx
    for j in range(1, SC_KERNEL):
        conv = conv + p.scw[SC_KERNEL - 1 - j:SC_KERNEL - j, :] * _shift_rows(cx, halo, j)
    st.halo_sc[...] = cx[rows - SUBLANES:, :]
    y_sc = sc_b * conv

    lx = z_ref[:, 3 * w:4 * w]
    lg = z_ref[:, 4 * w:5 * w]
    halo = fresh(st.halo_lx[...])
    xr = p.lcw[LRU_CONV - 1:LRU_CONV, :] * lx + p.lcb[...]
    for j in range(1, LRU_CONV):
        xr = xr + p.lcw[LRU_CONV - 1 - j:LRU_CONV - j, :] * _shift_rows(lx, halo, j)
    st.halo_lx[...] = lx[rows - SUBLANES:, :]

    xb = xr.astype(BF16)
    r = jax.nn.sigmoid(jnp.dot(xb, p.wa[...], preferred_element_type=F32) + p.ba[...])
    i = jax.nn.sigmoid(jnp.dot(xb, p.wx[...], preferred_element_type=F32) + p.bx[...])
    nl = -p.lam[...]
    softplus = jnp.maximum(nl, 0.0) + jnp.log1p(jnp.exp(-jnp.abs(nl)))
    log_a = (-LRU_C) * r * softplus
    a = jnp.exp(log_a)
    b = jnp.sqrt(-jnp.tanh(log_a) * (a * a + 1.0)) * (i * xr)

    pos = lax.bitwise_and(lax.broadcasted_iota(jnp.int32, (rows, w), 0), SUBLANES - 1)
    s = 1
    while s < SUBLANES:
        a, b = _scan_step(a, b, s, pos)
        s *= 2
    halves = w // LANES
    last = pl.ds(SUBLANES - 1, groups, stride=SUBLANES)
    for k in range(halves):
        st.a[k] = a[:, k * LANES:(k + 1) * LANES]
        st.b[k] = b[:, k * LANES:(k + 1) * LANES]
    at = jnp.concatenate([st.a[k, last, :] for k in range(halves)], axis=1)
    bt = jnp.concatenate([st.b[k, last, :] for k in range(halves)], axis=1)
    gpos = lax.broadcasted_iota(jnp.int32, (groups, w), 0)
    s = 1
    while s < groups:
        at, bt = _scan_step(at, bt, s, gpos)
        s *= 2
    h_in = fresh(st.h[0:1, :])
    h_end = bt + at * h_in
    st.h[0:1, :] = h_end[groups - 1:groups, :]
    st.c[...] = jnp.where(gpos >= 1, pltpu.roll(h_end, 1, 0), h_in)
    h = jnp.concatenate(
        [jnp.concatenate(
            [st.b[k, g * SUBLANES:(g + 1) * SUBLANES, :]
             + st.a[k, g * SUBLANES:(g + 1) * SUBLANES, :] * st.c[g:g + 1, k * LANES:(k + 1) * LANES]
             for g in range(groups)], axis=0) for k in range(halves)], axis=1)
    y_lru = _gelu_tanh(lg) * h
    return jnp.concatenate([y_sc, y_lru], axis=1).astype(BF16)


class _MixParams(NamedTuple):
    scw: Any
    lcw: Any
    lcb: Any
    wa: Any
    ba: Any
    wx: Any
    bx: Any
    lam: Any


class _MixState(NamedTuple):
    h: Any
    halo_sc: Any
    halo_lx: Any
    a: Any
    b: Any
    c: Any


def _out_ffn_kernel(h_ref, ya_ref, z0_ref, zn_ref, scw_ref, lcw_ref, lcb_ref, wa_ref, ba_ref,
                    wx_ref, bx_ref, lam_ref, woa_ref, wom_ref, g2_ref, wg_ref, wu_ref, wd_ref,
                    gf_ref, o_ref, ym_ref, hs_ref, halo_sc_ref, halo_lx_ref, a_ref, b_ref, c_ref,
                    *, final_norm, tiles_per_seq):
    i = pl.program_id(0)
    p = _MixParams(scw_ref, lcw_ref, lcb_ref, wa_ref, ba_ref, wx_ref, bx_ref, lam_ref)
    st = _MixState(hs_ref, halo_sc_ref, halo_lx_ref, a_ref, b_ref, c_ref)

    @pl.when(i == 0)
    def _():
        ym_ref[0] = _mix_tile(z0_ref, p, st, i == 0)

    nxt = i + 1
    ym_cur = ym_ref[lax.rem(i, 2)]
    ym_ref[lax.rem(nxt, 2)] = _mix_tile(zn_ref, p, st, lax.rem(nxt, tiles_per_seq) == 0)

    h = (h_ref[...]
         + lax.dot_general(ya_ref[...], woa_ref[...], (((0,), (0,)), ((), ())),
                           preferred_element_type=F32)
         + jnp.dot(ym_cur, wom_ref[...], preferred_element_type=F32))
    hn = _rms(h, g2_ref[...], RMS_EPS).astype(BF16)
    x = h + 0.5 * _swiglu(hn, wg_ref, wu_ref, wd_ref)
    if final_norm:
        x = _rms(x, gf_ref[...], RMS_EPS)
    o_ref[...] = x


def _out_ffn(h2, ya_t, zr, mix_params, woa, wom, g2, wg, wu, wd, gf, l, final_norm):
    n = h2.shape[0]
    n_tiles = n // ROW_TILE
    tiles_per_seq = ya_t.shape[2] // ROW_TILE
    w = SC_WIDTH
    row = lambda width: pl.BlockSpec((ROW_TILE, width), lambda i: (i, 0))
    par = lambda rows: pl.BlockSpec((None, rows, w), lambda i: (l, 0, 0))
    mix_half = SC_WIDTH + LRU_WIDTH
    return pl.pallas_call(
        functools.partial(_out_ffn_kernel, final_norm=final_norm, tiles_per_seq=tiles_per_seq),
        grid=(n_tiles,),
        in_specs=[
            row(D_MODEL),
            pl.BlockSpec((None, DA_WIDTH, ROW_TILE),
                         lambda i: (i // tiles_per_seq, 0, i % tiles_per_seq)),
            pl.BlockSpec((ROW_TILE, REST_WIDTH), lambda i: (0, 0)),
            pl.BlockSpec((ROW_TILE, REST_WIDTH), lambda i: (jnp.minimum(i + 1, n_tiles - 1), 0)),
            par(SC_KERNEL), par(LRU_CONV), par(1), par(w), par(1), par(w), par(1), par(1),
            _resident((None, DA_WIDTH, D_MODEL), lambda i: (l, 0, 0)),
            _resident((None, mix_half, D_MODEL), lambda i: (l, 0, 0)),
            pl.BlockSpec((None, 1, D_MODEL), lambda i: (l, 0, 0)),
            _resident((None, D_MODEL, D_FF), lambda i: (l, 0, 0)),
            _resident((None, D_MODEL, D_FF), lambda i: (l, 0, 0)),
            _resident((None, D_FF, D_MODEL), lambda i: (l, 0, 0)),
            pl.BlockSpec((1, D_MODEL), lambda i: (0, 0)),
        ],
        out_specs=row(D_MODEL),
        out_shape=jax.ShapeDtypeStruct((n, D_MODEL), F32),
        scratch_shapes=[
            pltpu.VMEM((2, ROW_TILE, mix_half), BF16),
            pltpu.VMEM((SUBLANES, w), F32),
            pltpu.VMEM((SUBLANES, w), F32),
            pltpu.VMEM((SUBLANES, w), F32),
            pltpu.VMEM((w // LANES, ROW_TILE, LANES), F32),
            pltpu.VMEM((w // LANES, ROW_TILE, LANES), F32),
            pltpu.VMEM((ROW_TILE // SUBLANES, w), F32),
        ],
        compiler_params=pltpu.CompilerParams(
            dimension_semantics=("arbitrary",), vmem_limit_bytes=VMEM_LIMIT_BYTES),
        name="out_ffn",
    )(h2, ya_t, zr, zr, *mix_params, woa, wom, g2, wg, wu, wd, gf)


def _block_diag(w):
    depth, nb, blk, _ = w.shape
    eye = jnp.eye(nb, dtype=w.dtype)
    return jnp.einsum('lnij,nm->lnimj', w, eye).reshape(depth, nb * blk, nb * blk)


def kernel(x, rel_bias, ffn1_norm, ffn1_gate, ffn1_up, ffn1_down, mix_norm, w_in, w_out, lam_q1, lam_k1, lam_q2, lam_k2, subln_gain, sc_conv_w, lru_conv_w, lru_conv_b, lru_wa, lru_ba, lru_wx, lru_bx, lru_lambda, ffn2_norm, ffn2_gate, ffn2_up, ffn2_down, final_norm):
    b, s, d = x.shape
    depth = w_in.shape[0]
    assert d == D_MODEL and s % ATTN_TILE == 0 and s % ROW_TILE == 0

    bf = lambda w: w.astype(BF16)
    vec = lambda v: v.reshape(depth, 1, v.shape[-1])
    wg1, wu1, wd1 = bf(ffn1_gate), bf(ffn1_up), bf(ffn1_down)
    wg2, wu2, wd2 = bf(ffn2_gate), bf(ffn2_up), bf(ffn2_down)
    col_scale = jnp.where(jnp.arange(QKV_WIDTH) < DA_WIDTH, Q_SCALE, 1.0).astype(F32)
    wq, wr = bf(w_in[:, :, :QKV_WIDTH] * col_scale), bf(w_in[:, :, QKV_WIDTH:])
    woa, wom = bf(w_out[:, :DA_WIDTH]), bf(w_out[:, DA_WIDTH:])
    wa, wx = bf(_block_diag(lru_wa)), bf(_block_diag(lru_wx))
    lam_params = jnp.stack([lam_q1, lam_k1, lam_q2, lam_k2], axis=1)
    g1, gm, g2 = vec(ffn1_norm), vec(mix_norm), vec(ffn2_norm)
    gs = subln_gain.reshape(depth, DA_VDIM, 1)
    lcb, lam = vec(lru_conv_b), vec(lru_lambda)
    ba, bx = vec(lru_ba.reshape(depth, -1)), vec(lru_bx.reshape(depth, -1))
    gf = final_norm.reshape(1, d)

    bias = _bias_tiles(rel_bias)

    x2 = x.reshape(b * s, d)
    for l in range(depth):
        lam_init = 0.8 - 0.6 * math.exp(-0.3 * l)
        h2, zq, zr = _ffn_in(x2, g1, wg1, wu1, wd1, gm, wq, wr, l)
        ya = _attention(zq.reshape(b, s, QKV_WIDTH), lam_params, gs, bias, l, lam_init)
        x2 = _out_ffn(h2, ya, zr, (sc_conv_w, lru_conv_w, lcb, wa, ba, wx, bx, lam),
                      woa, wom, g2, wg2, wu2, wd2, gf, l, l == depth - 1)
    return x2.reshape(b, s, d)
```

```python
import functools
import math
from typing import Any, NamedTuple

import jax
import jax.numpy as jnp
from jax import lax
from jax.experimental import pallas as pl
from jax.experimental.pallas import tpu as pltpu

F32 = jnp.float32
BF16 = jnp.bfloat16

D_MODEL = 1024
D_FF = 2816
DA_WIDTH = 512
SC_WIDTH = 256
LRU_WIDTH = 256
DA_HEAD_DIM = 64
DA_HEADS = 4
DA_VDIM = 2 * DA_HEAD_DIM
NUM_BUCKETS = 32
MAX_DISTANCE = 128
SUBLN_EPS = 1e-5
SC_KERNEL = 3
LRU_BLOCKS = 4
LRU_BLOCK = 64
LRU_CONV = 4
LRU_C = 8.0
RMS_EPS = 1e-6
NEG_INF = -1e30
QKV_WIDTH = 3 * DA_WIDTH
REST_WIDTH = 3 * SC_WIDTH + 2 * LRU_WIDTH
IN_WIDTH = QKV_WIDTH + REST_WIDTH
LOG2_E = math.log2(math.e)
Q_SCALE = DA_HEAD_DIM ** -0.5 * LOG2_E

LANES = 128
SUBLANES = 8
BF16_SUBLANES = 16
VT_ROWS = DA_VDIM + BF16_SUBLANES
VMEM_LIMIT_BYTES = 56 * 1024 * 1024
ROW_TILE = 512
ATTN_TILE = 512
SWEEP_UNROLL = 4


def _rms(x, g, eps):
    return x * lax.rsqrt(jnp.mean(x * x, axis=-1, keepdims=True) + eps) * g


def _swiglu(xn, wg_ref, wu_ref, wd_ref):
    g = jnp.dot(xn, wg_ref[...], preferred_element_type=F32)
    u = jnp.dot(xn, wu_ref[...], preferred_element_type=F32)
    a = (g * jax.nn.sigmoid(g) * u).astype(BF16)
    return jnp.dot(a, wd_ref[...], preferred_element_type=F32)


def _ffn_in_kernel(x_ref, g1_ref, wg_ref, wu_ref, wd_ref, gm_ref, wq_ref, wr_ref,
                   h_ref, zq_ref, zr_ref):
    x = x_ref[...]
    xn = _rms(x, g1_ref[...], RMS_EPS).astype(BF16)
    h = x + 0.5 * _swiglu(xn, wg_ref, wu_ref, wd_ref)
    h_ref[...] = h
    u = _rms(h, gm_ref[...], RMS_EPS).astype(BF16)
    zq_ref[...] = jnp.dot(u, wq_ref[...], preferred_element_type=F32).astype(BF16)
    zr_ref[...] = jnp.dot(u, wr_ref[...], preferred_element_type=F32)


def _resident(shape, index):
    return pl.BlockSpec(shape, index, pipeline_mode=pl.Buffered(1))


def _ffn_in(x2, g1, wg, wu, wd, gm, wq, wr, l):
    n = x2.shape[0]
    row = lambda w: pl.BlockSpec((ROW_TILE, w), lambda i: (i, 0))
    vec = pl.BlockSpec((None, 1, D_MODEL), lambda i: (l, 0, 0))
    return pl.pallas_call(
        _ffn_in_kernel,
        grid=(n // ROW_TILE,),
        in_specs=[
            row(D_MODEL), vec,
            _resident((None, D_MODEL, D_FF), lambda i: (l, 0, 0)),
            _resident((None, D_MODEL, D_FF), lambda i: (l, 0, 0)),
            _resident((None, D_FF, D_MODEL), lambda i: (l, 0, 0)),
            vec,
            _resident((None, D_MODEL, QKV_WIDTH), lambda i: (l, 0, 0)),
            _resident((None, D_MODEL, REST_WIDTH), lambda i: (l, 0, 0)),
        ],
        out_specs=[row(D_MODEL), row(QKV_WIDTH), row(REST_WIDTH)],
        out_shape=[
            jax.ShapeDtypeStruct((n, D_MODEL), F32),
            jax.ShapeDtypeStruct((n, QKV_WIDTH), BF16),
            jax.ShapeDtypeStruct((n, REST_WIDTH), F32),
        ],
        compiler_params=pltpu.CompilerParams(
            dimension_semantics=("arbitrary",), vmem_limit_bytes=VMEM_LIMIT_BYTES),
        name="ffn_in",
    )(x2, g1, wg, wu, wd, gm, wq, wr)


def _bias_kernel(rb_ref, o_ref):
    h = pl.program_id(0)
    t = ATTN_TILE
    max_exact = NUM_BUCKETS // 2
    key = lax.broadcasted_iota(jnp.int32, (t, t), 0)
    qry = lax.broadcasted_iota(jnp.int32, (t, t), 1)
    for off in range(2):
        dist = qry - key + off * t
        n = jnp.maximum(dist, 0)
        nf = jnp.maximum(n, 1).astype(F32)
        large = max_exact + (jnp.log(nf / max_exact) / math.log(MAX_DISTANCE / max_exact)
                             * (NUM_BUCKETS - max_exact)).astype(jnp.int32)
        large = jnp.minimum(large, NUM_BUCKETS - 1)
        bucket = jnp.where(n < max_exact, n, large)
        for mp in range(2):
            hm = 2 * h + mp
            val = jnp.zeros((t, t), F32)
            for j in range(NUM_BUCKETS):
                val = jnp.where(bucket == j, rb_ref[j, hm], val)
            far = rb_ref[NUM_BUCKETS - 1, hm]
            o_ref[off, :, mp * t:(mp + 1) * t] = jnp.where(dist >= 0, (val - far) * LOG2_E,
                                                           NEG_INF)


def _bias_tiles(rel_bias):
    t = ATTN_TILE
    return pl.pallas_call(
        _bias_kernel,
        grid=(DA_HEADS,),
        in_specs=[pl.BlockSpec(memory_space=pltpu.SMEM)],
        out_specs=pl.BlockSpec((None, 2, t, 2 * t), lambda i: (i, 0, 0, 0)),
        out_shape=jax.ShapeDtypeStruct((DA_HEADS, 2, t, 2 * t), F32),
        compiler_params=pltpu.CompilerParams(dimension_semantics=("arbitrary",)),
        name="bias_tiles",
    )(rel_bias)


def _attn_kernel(lam_ref, g_ref, bias_ref, q_ref, k_ref, v_ref, o_ref,
                 qq_ref, vt_ref, s_ref, mx_ref, m_ref, acc_ref, *, lam_init, nq):
    t = ATTN_TILE
    lane = lax.broadcasted_iota(jnp.int32, (t, LANES), 1)
    for qi in range(nq):
        q = q_ref[qi * t:(qi + 1) * t, :]
        zero = jnp.zeros_like(q)
        qq_ref[qi, 0:t, :] = jnp.where(lane < DA_HEAD_DIM, q, zero)
        qq_ref[qi, t:2 * t, :] = jnp.where(lane >= DA_HEAD_DIM, q, zero)

    lp = lam_ref[...]
    lam = (jnp.exp(jnp.sum(lp[0:1] * lp[1:2], axis=1, keepdims=True))
           - jnp.exp(jnp.sum(lp[2:3] * lp[3:4], axis=1, keepdims=True)) + lam_init)
    gain = g_ref[...] * (1.0 - lam_init)

    ones_row = lax.broadcasted_iota(jnp.int32, (VT_ROWS - DA_VDIM, t), 0) == 0
    for kj in range(nq):
        vt_ref[kj, 0:DA_VDIM, :] = v_ref[kj * t:(kj + 1) * t, :].T
        vt_ref[kj, DA_VDIM:VT_ROWS, :] = jnp.where(ones_row, 1.0, 0.0).astype(BF16)

    m_ref[...] = jnp.full(m_ref.shape, NEG_INF, F32)
    acc_ref[...] = jnp.zeros(acc_ref.shape, F32)

    def key_rows(kj):
        return pl.ds(kj * t, t) if isinstance(kj, int) else pl.ds(pl.multiple_of(kj * t, t), t)

    def scores(qi, kj, off, buf):
        s = lax.dot_general(k_ref[key_rows(kj), :], qq_ref[qi], (((1,), (1,)), ((), ())),
                            preferred_element_type=F32)
        if off is not None:
            s = s + bias_ref[off]
        s_ref[buf] = s
        mx_ref[buf] = jnp.max(s, axis=0, keepdims=True)

    def consume(qi, kj, off, buf):
        m_old = m_ref[qi]
        m_new = jnp.maximum(m_old, mx_ref[buf])
        alpha = jnp.exp2(m_old - m_new)
        p = jnp.exp2(s_ref[buf] - m_new)
        pv = jnp.dot(vt_ref[kj], p.astype(BF16), preferred_element_type=F32)
        acc_ref[qi] = alpha * acc_ref[qi] + pv
        m_ref[qi] = m_new

    def sweep(n_steps, coords):
        if n_steps == 0:
            return
        scores(*coords(0), 0)

        def group(i, carry):
            for u in range(SWEEP_UNROLL):
                n = SWEEP_UNROLL * i + u
                scores(*coords(n + 1), (u + 1) % 2)
                consume(*coords(n), u % 2)
            return carry

        n_groups = (n_steps - 1) // SWEEP_UNROLL
        lax.fori_loop(0, n_groups, group, 0)
        for n in range(n_groups * SWEEP_UNROLL, n_steps):
            if n + 1 < n_steps:
                scores(*coords(n + 1), (n + 1) % 2)
            consume(*coords(n), n % 2)

    def biased_coords(n):
        if isinstance(n, int):
            qi, off = (n + 1) // 2, n % 2
        else:
            qi, off = lax.shift_right_logical(n + 1, 1), lax.bitwise_and(n, 1)
        return qi, qi - off, off

    def far_coords(n):
        first = lambda qi: (qi - 1) * (qi - 2) // 2
        if isinstance(n, int):
            qi = max(c for c in range(2, nq) if first(c) <= n)
            return qi, n - first(qi), None
        qi = 2
        for c in range(3, nq):
            qi = qi + (n >= first(c)).astype(jnp.int32)
        return qi, n - lax.shift_right_logical((qi - 1) * (qi - 2), 1), None

    sweep(2 * nq - 1, biased_coords)
    sweep((nq - 1) * (nq - 2) // 2, far_coords)

    for qi in range(nq):
        on = acc_ref[qi, 0:DA_VDIM, :] / acc_ref[qi, DA_VDIM:DA_VDIM + 1, :]
        o = on[:, 0:t] - lam * on[:, t:2 * t]
        y = o * lax.rsqrt(jnp.mean(o * o, axis=0, keepdims=True) + SUBLN_EPS) * gain
        o_ref[:, qi * t:(qi + 1) * t] = y.astype(o_ref.dtype)


def _attention(zq3, lam_params, subln_g, bias, l, lam_init):
    b, s, _ = zq3.shape
    t = ATTN_TILE
    nq = s // t
    return pl.pallas_call(
        functools.partial(_attn_kernel, lam_init=lam_init, nq=nq),
        grid=(b, DA_HEADS),
        in_specs=[
            pl.BlockSpec((None, 4, DA_HEAD_DIM), lambda bi, h: (l, 0, 0)),
            pl.BlockSpec((None, DA_VDIM, 1), lambda bi, h: (l, 0, 0)),
            pl.BlockSpec((None, 2, t, 2 * t), lambda bi, h: (h, 0, 0, 0)),
            pl.BlockSpec((None, s, LANES), lambda bi, h: (bi, 0, h)),
            pl.BlockSpec((None, s, LANES), lambda bi, h: (bi, 0, DA_HEADS + h)),
            pl.BlockSpec((None, s, LANES), lambda bi, h: (bi, 0, 2 * DA_HEADS + h)),
        ],
        out_specs=pl.BlockSpec((None, DA_VDIM, s), lambda bi, h: (bi, h, 0)),
        out_shape=jax.ShapeDtypeStruct((b, DA_WIDTH, s), BF16),
        scratch_shapes=[
            pltpu.VMEM((nq, 2 * t, LANES), BF16),
            pltpu.VMEM((nq, VT_ROWS, t), BF16),
            pltpu.VMEM((2, t, 2 * t), F32),
            pltpu.VMEM((2, 1, 2 * t), F32),
            pltpu.VMEM((nq, 1, 2 * t), F32),
            pltpu.VMEM((nq, VT_ROWS, 2 * t), F32),
        ],
        compiler_params=pltpu.CompilerParams(
            dimension_semantics=("arbitrary", "arbitrary"),
            vmem_limit_bytes=VMEM_LIMIT_BYTES),
        name="diff_attn",
    )(lam_params, subln_g, bias, zq3, zq3, zq3)


def _shift_rows(x, halo, j):
    r = pltpu.roll(x, j, 0)
    head_row = lax.broadcasted_iota(jnp.int32, halo.shape, 0)
    head = jnp.where(head_row < j, pltpu.roll(halo, j, 0), r[:SUBLANES])
    return jnp.concatenate([head, r[SUBLANES:]], axis=0)


def _gelu_tanh(x):
    return 0.5 * x * (1.0 + jnp.tanh(math.sqrt(2.0 / math.pi) * (x + 0.044715 * (x * x * x))))


def _scan_step(a, b, s, pos):
    keep = pos >= s
    a_sh = jnp.where(keep, pltpu.roll(a, s, 0), 1.0)
    b_sh = jnp.where(keep, pltpu.roll(b, s, 0), 0.0)
    return a * a_sh, a * b_sh + b


def _mix_tile(z_ref, p, st, reset):
    rows = z_ref.shape[0]
    w = SC_WIDTH
    groups = rows // SUBLANES
    fresh = lambda x: jnp.where(reset, jnp.zeros_like(x), x)

    sc_b = z_ref[:, 0:w]
    cx = z_ref[:, w:2 * w] * z_ref[:, 2 * w:3 * w]
    halo = fresh(st.halo_sc[...])
    conv = p.scw[SC_KERNEL - 1:SC_KERNEL, :] * cx
    for j in range(1, SC_KERNEL):
        conv = conv + p.scw[SC_KERNEL - 1 - j:SC_KERNEL - j, :] * _shift_rows(cx, halo, j)
    st.halo_sc[...] = cx[rows - SUBLANES:, :]
    y_sc = sc_b * conv

    lx = z_ref[:, 3 * w:4 * w]
    lg = z_ref[:, 4 * w:5 * w]
    halo = fresh(st.halo_lx[...])
    xr = p.lcw[LRU_CONV - 1:LRU_CONV, :] * lx + p.lcb[...]
    for j in range(1, LRU_CONV):
        xr = xr + p.lcw[LRU_CONV - 1 - j:LRU_CONV - j, :] * _shift_rows(lx, halo, j)
    st.halo_lx[...] = lx[rows - SUBLANES:, :]

    xb = xr.astype(BF16)
    r = jax.nn.sigmoid(jnp.dot(xb, p.wa[...], preferred_element_type=F32) + p.ba[...])
    i = jax.nn.sigmoid(jnp.dot(xb, p.wx[...], preferred_element_type=F32) + p.bx[...])
    nl = -p.lam[...]
    softplus = jnp.maximum(nl, 0.0) + jnp.log1p(jnp.exp(-jnp.abs(nl)))
    log_a = (-LRU_C) * r * softplus
    a = jnp.exp(log_a)
    b = jnp.sqrt(-jnp.tanh(log_a) * (a * a + 1.0)) * (i * xr)

    pos = lax.bitwise_and(lax.broadcasted_iota(jnp.int32, (rows, w), 0), SUBLANES - 1)
    s = 1
    while s < SUBLANES:
        a, b = _scan_step(a, b, s, pos)
        s *= 2
    halves = w // LANES
    last = pl.ds(SUBLANES - 1, groups, stride=SUBLANES)
    for k in range(halves):
        st.a[k] = a[:, k * LANES:(k + 1) * LANES]
        st.b[k] = b[:, k * LANES:(k + 1) * LANES]
    at = jnp.concatenate([st.a[k, last, :] for k in range(halves)], axis=1)
    bt = jnp.concatenate([st.b[k, last, :] for k in range(halves)], axis=1)
    gpos = lax.broadcasted_iota(jnp.int32, (groups, w), 0)
    s = 1
    while s < groups:
        at, bt = _scan_step(at, bt, s, gpos)
        s *= 2
    h_in = fresh(st.h[0:1, :])
    h_end = bt + at * h_in
    st.h[0:1, :] = h_end[groups - 1:groups, :]
    st.c[...] = jnp.where(gpos >= 1, pltpu.roll(h_end, 1, 0), h_in)
    h = jnp.concatenate(
        [jnp.concatenate(
            [st.b[k, g * SUBLANES:(g + 1) * SUBLANES, :]
             + st.a[k, g * SUBLANES:(g + 1) * SUBLANES, :] * st.c[g:g + 1, k * LANES:(k + 1) * LANES]
             for g in range(groups)], axis=0) for k in range(halves)], axis=1)
    y_lru = _gelu_tanh(lg) * h
    return jnp.concatenate([y_sc, y_lru], axis=1).astype(BF16)


class _MixParams(NamedTuple):
    scw: Any
    lcw: Any
    lcb: Any
    wa: Any
    ba: Any
    wx: Any
    bx: Any
    lam: Any


class _MixState(NamedTuple):
    h: Any
    halo_sc: Any
    halo_lx: Any
    a: Any
    b: Any
    c: Any


def _out_ffn_kernel(h_ref, ya_ref, z0_ref, zn_ref, scw_ref, lcw_ref, lcb_ref, wa_ref, ba_ref,
                    wx_ref, bx_ref, lam_ref, woa_ref, wom_ref, g2_ref, wg_ref, wu_ref, wd_ref,
                    gf_ref, o_ref, ym_ref, hs_ref, halo_sc_ref, halo_lx_ref, a_ref, b_ref, c_ref,
                    *, final_norm, tiles_per_seq):
    i = pl.program_id(0)
    p = _MixParams(scw_ref, lcw_ref, lcb_ref, wa_ref, ba_ref, wx_ref, bx_ref, lam_ref)
    st = _MixState(hs_ref, halo_sc_ref, halo_lx_ref, a_ref, b_ref, c_ref)

    @pl.when(i == 0)
    def _():
        ym_ref[0] = _mix_tile(z0_ref, p, st, i == 0)

    nxt = i + 1
    ym_cur = ym_ref[lax.rem(i, 2)]
    ym_ref[lax.rem(nxt, 2)] = _mix_tile(zn_ref, p, st, lax.rem(nxt, tiles_per_seq) == 0)

    h = (h_ref[...]
         + lax.dot_general(ya_ref[...], woa_ref[...], (((0,), (0,)), ((), ())),
                           preferred_element_type=F32)
         + jnp.dot(ym_cur, wom_ref[...], preferred_element_type=F32))
    hn = _rms(h, g2_ref[...], RMS_EPS).astype(BF16)
    x = h + 0.5 * _swiglu(hn, wg_ref, wu_ref, wd_ref)
    if final_norm:
        x = _rms(x, gf_ref[...], RMS_EPS)
    o_ref[...] = x


def _out_ffn(h2, ya_t, zr, mix_params, woa, wom, g2, wg, wu, wd, gf, l, final_norm):
    n = h2.shape[0]
    n_tiles = n // ROW_TILE
    tiles_per_seq = ya_t.shape[2] // ROW_TILE
    w = SC_WIDTH
    row = lambda width: pl.BlockSpec((ROW_TILE, width), lambda i: (i, 0))
    par = lambda rows: pl.BlockSpec((None, rows, w), lambda i: (l, 0, 0))
    mix_half = SC_WIDTH + LRU_WIDTH
    return pl.pallas_call(
        functools.partial(_out_ffn_kernel, final_norm=final_norm, tiles_per_seq=tiles_per_seq),
        grid=(n_tiles,),
        in_specs=[
            row(D_MODEL),
            pl.BlockSpec((None, DA_WIDTH, ROW_TILE),
                         lambda i: (i // tiles_per_seq, 0, i % tiles_per_seq)),
            pl.BlockSpec((ROW_TILE, REST_WIDTH), lambda i: (0, 0)),
            pl.BlockSpec((ROW_TILE, REST_WIDTH), lambda i: (jnp.minimum(i + 1, n_tiles - 1), 0)),
            par(SC_KERNEL), par(LRU_CONV), par(1), par(w), par(1), par(w), par(1), par(1),
            _resident((None, DA_WIDTH, D_MODEL), lambda i: (l, 0, 0)),
            _resident((None, mix_half, D_MODEL), lambda i: (l, 0, 0)),
            pl.BlockSpec((None, 1, D_MODEL), lambda i: (l, 0, 0)),
            _resident((None, D_MODEL, D_FF), lambda i: (l, 0, 0)),
            _resident((None, D_MODEL, D_FF), lambda i: (l, 0, 0)),
            _resident((None, D_FF, D_MODEL), lambda i: (l, 0, 0)),
            pl.BlockSpec((1, D_MODEL), lambda i: (0, 0)),
        ],
        out_specs=row(D_MODEL),
        out_shape=jax.ShapeDtypeStruct((n, D_MODEL), F32),
        scratch_shapes=[
            pltpu.VMEM((2, ROW_TILE, mix_half), BF16),
            pltpu.VMEM((SUBLANES, w), F32),
            pltpu.VMEM((SUBLANES, w), F32),
            pltpu.VMEM((SUBLANES, w), F32),
            pltpu.VMEM((w // LANES, ROW_TILE, LANES), F32),
            pltpu.VMEM((w // LANES, ROW_TILE, LANES), F32),
            pltpu.VMEM((ROW_TILE // SUBLANES, w), F32),
        ],
        compiler_params=pltpu.CompilerParams(
            dimension_semantics=("arbitrary",), vmem_limit_bytes=VMEM_LIMIT_BYTES),
        name="out_ffn",
    )(h2, ya_t, zr, zr, *mix_params, woa, wom, g2, wg, wu, wd, gf)


def _block_diag(w):
    depth, nb, blk, _ = w.shape
    eye = jnp.eye(nb, dtype=w.dtype)
    return jnp.einsum('lnij,nm->lnimj', w, eye).reshape(depth, nb * blk, nb * blk)


def kernel(x, rel_bias, ffn1_norm, ffn1_gate, ffn1_up, ffn1_down, mix_norm, w_in, w_out, lam_q1, lam_k1, lam_q2, lam_k2, subln_gain, sc_conv_w, lru_conv_w, lru_conv_b, lru_wa, lru_ba, lru_wx, lru_bx, lru_lambda, ffn2_norm, ffn2_gate, ffn2_up, ffn2_down, final_norm):
    b, s, d = x.shape
    depth = w_in.shape[0]
    assert d == D_MODEL and s % ATTN_TILE == 0 and s % ROW_TILE == 0

    bf = lambda w: w.astype(BF16)
    vec = lambda v: v.reshape(depth, 1, v.shape[-1])
    wg1, wu1, wd1 = bf(ffn1_gate), bf(ffn1_up), bf(ffn1_down)
    wg2, wu2, wd2 = bf(ffn2_gate), bf(ffn2_up), bf(ffn2_down)
    col_scale = jnp.where(jnp.arange(QKV_WIDTH) < DA_WIDTH, Q_SCALE, 1.0).astype(F32)
    wq, wr = bf(w_in[:, :, :QKV_WIDTH] * col_scale), bf(w_in[:, :, QKV_WIDTH:])
    woa, wom = bf(w_out[:, :DA_WIDTH]), bf(w_out[:, DA_WIDTH:])
    wa, wx = bf(_block_diag(lru_wa)), bf(_block_diag(lru_wx))
    lam_params = jnp.stack([lam_q1, lam_k1, lam_q2, lam_k2], axis=1)
    g1, gm, g2 = vec(ffn1_norm), vec(mix_norm), vec(ffn2_norm)
    gs = subln_gain.reshape(depth, DA_VDIM, 1)
    lcb, lam = vec(lru_conv_b), vec(lru_lambda)
    ba, bx = vec(lru_ba.reshape(depth, -1)), vec(lru_bx.reshape(depth, -1))
    gf = final_norm.reshape(1, d)

    bias = _bias_tiles(rel_bias)

    x2 = x.reshape(b * s, d)
    for l in range(depth):
        lam_init = 0.8 - 0.6 * math.exp(-0.3 * l)
        h2, zq, zr = _ffn_in(x2, g1, wg1, wu1, wd1, gm, wq, wr, l)
        ya = _attention(zq.reshape(b, s, QKV_WIDTH), lam_params, gs, bias, l, lam_init)
        x2 = _out_ffn(h2, ya, zr, (sc_conv_w, lru_conv_w, lcb, wa, ba, wx, bx, lam),
                      woa, wom, g2, wg2, wu2, wd2, gf, l, l == depth - 1)
    return x2.reshape(b, s, d)
```

```python
import functools
import math
from typing import Any, NamedTuple

import jax
import jax.numpy as jnp
from jax import lax
from jax.experimental import pallas as pl
from jax.experimental.pallas import tpu as pltpu

F32 = jnp.float32
BF16 = jnp.bfloat16

D_MODEL = 1024
D_FF = 2816
DA_WIDTH = 512
SC_WIDTH = 256
LRU_WIDTH = 256
DA_HEAD_DIM = 64
DA_HEADS = 4
DA_VDIM = 2 * DA_HEAD_DIM
NUM_BUCKETS = 32
MAX_DISTANCE = 128
SUBLN_EPS = 1e-5
SC_KERNEL = 3
LRU_BLOCKS = 4
LRU_BLOCK = 64
LRU_CONV = 4
LRU_C = 8.0
RMS_EPS = 1e-6
NEG_INF = -1e30
QKV_WIDTH = 3 * DA_WIDTH
REST_WIDTH = 3 * SC_WIDTH + 2 * LRU_WIDTH
IN_WIDTH = QKV_WIDTH + REST_WIDTH
LOG2_E = math.log2(math.e)
Q_SCALE = DA_HEAD_DIM ** -0.5 * LOG2_E

LANES = 128
SUBLANES = 8
BF16_SUBLANES = 16
VT_ROWS = DA_VDIM + BF16_SUBLANES
VMEM_LIMIT_BYTES = 56 * 1024 * 1024
ROW_TILE = 512
ATTN_TILE = 512
SWEEP_UNROLL = 4


def _rms(x, g, eps):
    return x * lax.rsqrt(jnp.mean(x * x, axis=-1, keepdims=True) + eps) * g


def _swiglu(xn, wg_ref, wu_ref, wd_ref):
    g = jnp.dot(xn, wg_ref[...], preferred_element_type=F32)
    u = jnp.dot(xn, wu_ref[...], preferred_element_type=F32)
    a = (g * jax.nn.sigmoid(g) * u).astype(BF16)
    return jnp.dot(a, wd_ref[...], preferred_element_type=F32)


def _ffn_in_kernel(x_ref, g1_ref, wg_ref, wu_ref, wd_ref, gm_ref, wq_ref, wr_ref,
                   h_ref, zq_ref, zr_ref):
    x = x_ref[...]
    xn = _rms(x, g1_ref[...], RMS_EPS).astype(BF16)
    h = x + 0.5 * _swiglu(xn, wg_ref, wu_ref, wd_ref)
    h_ref[...] = h
    u = _rms(h, gm_ref[...], RMS_EPS).astype(BF16)
    zq_ref[...] = jnp.dot(u, wq_ref[...], preferred_element_type=F32).astype(BF16)
    zr_ref[...] = jnp.dot(u, wr_ref[...], preferred_element_type=F32)


def _resident(shape, index):
    return pl.BlockSpec(shape, index, pipeline_mode=pl.Buffered(1))


def _ffn_in(x2, g1, wg, wu, wd, gm, wq, wr, l):
    n = x2.shape[0]
    row = lambda w: pl.BlockSpec((ROW_TILE, w), lambda i: (i, 0))
    vec = pl.BlockSpec((None, 1, D_MODEL), lambda i: (l, 0, 0))
    return pl.pallas_call(
        _ffn_in_kernel,
        grid=(n // ROW_TILE,),
        in_specs=[
            row(D_MODEL), vec,
            _resident((None, D_MODEL, D_FF), lambda i: (l, 0, 0)),
            _resident((None, D_MODEL, D_FF), lambda i: (l, 0, 0)),
            _resident((None, D_FF, D_MODEL), lambda i: (l, 0, 0)),
            vec,
            _resident((None, D_MODEL, QKV_WIDTH), lambda i: (l, 0, 0)),
            _resident((None, D_MODEL, REST_WIDTH), lambda i: (l, 0, 0)),
        ],
        out_specs=[row(D_MODEL), row(QKV_WIDTH), row(REST_WIDTH)],
        out_shape=[
            jax.ShapeDtypeStruct((n, D_MODEL), F32),
            jax.ShapeDtypeStruct((n, QKV_WIDTH), BF16),
            jax.ShapeDtypeStruct((n, REST_WIDTH), F32),
        ],
        compiler_params=pltpu.CompilerParams(
            dimension_semantics=("arbitrary",), vmem_limit_bytes=VMEM_LIMIT_BYTES),
        name="ffn_in",
    )(x2, g1, wg, wu, wd, gm, wq, wr)


def _bias_kernel(rb_ref, o_ref):
    h = pl.program_id(0)
    t = ATTN_TILE
    max_exact = NUM_BUCKETS // 2
    key = lax.broadcasted_iota(jnp.int32, (t, t), 0)
    qry = lax.broadcasted_iota(jnp.int32, (t, t), 1)
    for off in range(2):
        dist = qry - key + off * t
        n = jnp.maximum(dist, 0)
        nf = jnp.maximum(n, 1).astype(F32)
        large = max_exact + (jnp.log(nf / max_exact) / math.log(MAX_DISTANCE / max_exact)
                             * (NUM_BUCKETS - max_exact)).astype(jnp.int32)
        large = jnp.minimum(large, NUM_BUCKETS - 1)
        bucket = jnp.where(n < max_exact, n, large)
        for mp in range(2):
            hm = 2 * h + mp
            val = jnp.zeros((t, t), F32)
            for j in range(NUM_BUCKETS):
                val = jnp.where(bucket == j, rb_ref[j, hm], val)
            far = rb_ref[NUM_BUCKETS - 1, hm]
            o_ref[off, :, mp * t:(mp + 1) * t] = jnp.where(dist >= 0, (val - far) * LOG2_E,
                                                           NEG_INF)


def _bias_tiles(rel_bias):
    t = ATTN_TILE
    return pl.pallas_call(
        _bias_kernel,
        grid=(DA_HEADS,),
        in_specs=[pl.BlockSpec(memory_space=pltpu.SMEM)],
        out_specs=pl.BlockSpec((None, 2, t, 2 * t), lambda i: (i, 0, 0, 0)),
        out_shape=jax.ShapeDtypeStruct((DA_HEADS, 2, t, 2 * t), F32),
        compiler_params=pltpu.CompilerParams(dimension_semantics=("arbitrary",)),
        name="bias_tiles",
    )(rel_bias)


def _attn_kernel(lam_ref, g_ref, bias_ref, q_ref, k_ref, v_ref, o_ref,
                 qq_ref, vt_ref, s_ref, mx_ref, m_ref, acc_ref, *, lam_init, nq):
    t = ATTN_TILE
    chan = lax.broadcasted_iota(jnp.int32, (LANES, t), 0)
    for qi in range(nq):
        q_t = q_ref[qi * t:(qi + 1) * t, :].T
        zero = jnp.zeros_like(q_t)
        qq_ref[qi, :, 0:t] = jnp.where(chan < DA_HEAD_DIM, q_t, zero)
        qq_ref[qi, :, t:2 * t] = jnp.where(chan >= DA_HEAD_DIM, q_t, zero)

    lp = lam_ref[...]
    lam = (jnp.exp(jnp.sum(lp[0:1] * lp[1:2], axis=1, keepdims=True))
           - jnp.exp(jnp.sum(lp[2:3] * lp[3:4], axis=1, keepdims=True)) + lam_init)
    gain = g_ref[...] * (1.0 - lam_init)

    ones_row = lax.broadcasted_iota(jnp.int32, (VT_ROWS - DA_VDIM, t), 0) == 0
    for kj in range(nq):
        vt_ref[kj, 0:DA_VDIM, :] = v_ref[kj * t:(kj + 1) * t, :].T
        vt_ref[kj, DA_VDIM:VT_ROWS, :] = jnp.where(ones_row, 1.0, 0.0).astype(BF16)

    m_ref[...] = jnp.full(m_ref.shape, NEG_INF, F32)
    acc_ref[...] = jnp.zeros(acc_ref.shape, F32)

    def key_rows(kj):
        return pl.ds(kj * t, t) if isinstance(kj, int) else pl.ds(pl.multiple_of(kj * t, t), t)

    def scores(qi, kj, off, buf):
        s = jnp.dot(k_ref[key_rows(kj), :], qq_ref[qi], preferred_element_type=F32)
        if off is not None:
            s = s + bias_ref[off]
        s_ref[buf] = s
        mx_ref[buf] = jnp.max(s, axis=0, keepdims=True)

    def consume(qi, kj, off, buf):
        m_old = m_ref[qi]
        m_new = jnp.maximum(m_old, mx_ref[buf])
        alpha = jnp.exp2(m_old - m_new)
        p = jnp.exp2(s_ref[buf] - m_new)
        pv = jnp.dot(vt_ref[kj], p.astype(BF16), preferred_element_type=F32)
        acc_ref[qi] = alpha * acc_ref[qi] + pv
        m_ref[qi] = m_new

    def sweep(n_steps, coords):
        if n_steps == 0:
            return
        scores(*coords(0), 0)

        def group(i, carry):
            for u in range(SWEEP_UNROLL):
                n = SWEEP_UNROLL * i + u
                scores(*coords(n + 1), (u + 1) % 2)
                consume(*coords(n), u % 2)
            return carry

        n_groups = (n_steps - 1) // SWEEP_UNROLL
        lax.fori_loop(0, n_groups, group, 0)
        for n in range(n_groups * SWEEP_UNROLL, n_steps):
            if n + 1 < n_steps:
                scores(*coords(n + 1), (n + 1) % 2)
            consume(*coords(n), n % 2)

    def biased_coords(n):
        if isinstance(n, int):
            qi, off = (n + 1) // 2, n % 2
        else:
            qi, off = lax.shift_right_logical(n + 1, 1), lax.bitwise_and(n, 1)
        return qi, qi - off, off

    def far_coords(n):
        first = lambda qi: (qi - 1) * (qi - 2) // 2
        if isinstance(n, int):
            qi = max(c for c in range(2, nq) if first(c) <= n)
            return qi, n - first(qi), None
        qi = 2
        for c in range(3, nq):
            qi = qi + (n >= first(c)).astype(jnp.int32)
        return qi, n - lax.shift_right_logical((qi - 1) * (qi - 2), 1), None

    sweep(2 * nq - 1, biased_coords)
    sweep((nq - 1) * (nq - 2) // 2, far_coords)

    for qi in range(nq):
        on = acc_ref[qi, 0:DA_VDIM, :] / acc_ref[qi, DA_VDIM:DA_VDIM + 1, :]
        o = on[:, 0:t] - lam * on[:, t:2 * t]
        y = o * lax.rsqrt(jnp.mean(o * o, axis=0, keepdims=True) + SUBLN_EPS) * gain
        o_ref[:, qi * t:(qi + 1) * t] = y.astype(o_ref.dtype)


def _attention(zq3, lam_params, subln_g, bias, l, lam_init):
    b, s, _ = zq3.shape
    t = ATTN_TILE
    nq = s // t
    return pl.pallas_call(
        functools.partial(_attn_kernel, lam_init=lam_init, nq=nq),
        grid=(b, DA_HEADS),
        in_specs=[
            pl.BlockSpec((None, 4, DA_HEAD_DIM), lambda bi, h: (l, 0, 0)),
            pl.BlockSpec((None, DA_VDIM, 1), lambda bi, h: (l, 0, 0)),
            pl.BlockSpec((None, 2, t, 2 * t), lambda bi, h: (h, 0, 0, 0)),
            pl.BlockSpec((None, s, LANES), lambda bi, h: (bi, 0, h)),
            pl.BlockSpec((None, s, LANES), lambda bi, h: (bi, 0, DA_HEADS + h)),
            pl.BlockSpec((None, s, LANES), lambda bi, h: (bi, 0, 2 * DA_HEADS + h)),
        ],
        out_specs=pl.BlockSpec((None, DA_VDIM, s), lambda bi, h: (bi, h, 0)),
        out_shape=jax.ShapeDtypeStruct((b, DA_WIDTH, s), BF16),
        scratch_shapes=[
            pltpu.VMEM((nq, LANES, 2 * t), BF16),
            pltpu.VMEM((nq, VT_ROWS, t), BF16),
            pltpu.VMEM((2, t, 2 * t), F32),
            pltpu.VMEM((2, 1, 2 * t), F32),
            pltpu.VMEM((nq, 1, 2 * t), F32),
            pltpu.VMEM((nq, VT_ROWS, 2 * t), F32),
        ],
        compiler_params=pltpu.CompilerParams(
            dimension_semantics=("arbitrary", "arbitrary"),
            vmem_limit_bytes=VMEM_LIMIT_BYTES),
        name="diff_attn",
    )(lam_params, subln_g, bias, zq3, zq3, zq3)


def _shift_rows(x, halo, j):
    r = pltpu.roll(x, j, 0)
    head_row = lax.broadcasted_iota(jnp.int32, halo.shape, 0)
    head = jnp.where(head_row < j, pltpu.roll(halo, j, 0), r[:SUBLANES])
    return jnp.concatenate([head, r[SUBLANES:]], axis=0)


def _gelu_tanh(x):
    return 0.5 * x * (1.0 + jnp.tanh(math.sqrt(2.0 / math.pi) * (x + 0.044715 * (x * x * x))))


def _scan_step(a, b, s, pos):
    keep = pos >= s
    a_sh = jnp.where(keep, pltpu.roll(a, s, 0), 1.0)
    b_sh = jnp.where(keep, pltpu.roll(b, s, 0), 0.0)
    return a * a_sh, a * b_sh + b


def _mix_tile(z_ref, p, st, reset):
    rows = z_ref.shape[0]
    w = SC_WIDTH
    groups = rows // SUBLANES
    fresh = lambda x: jnp.where(reset, jnp.zeros_like(x), x)

    sc_b = z_ref[:, 0:w]
    cx = z_ref[:, w:2 * w] * z_ref[:, 2 * w:3 * w]
    halo = fresh(st.halo_sc[...])
    conv = p.scw[SC_KERNEL - 1:SC_KERNEL, :] * cx
    for j in range(1, SC_KERNEL):
        conv = conv + p.scw[SC_KERNEL - 1 - j:SC_KERNEL - j, :] * _shift_rows(cx, halo, j)
    st.halo_sc[...] = cx[rows - SUBLANES:, :]
    y_sc = sc_b * conv

    lx = z_ref[:, 3 * w:4 * w]
    lg = z_ref[:, 4 * w:5 * w]
    halo = fresh(st.halo_lx[...])
    xr = p.lcw[LRU_CONV - 1:LRU_CONV, :] * lx + p.lcb[...]
    for j in range(1, LRU_CONV):
        xr = xr + p.lcw[LRU_CONV - 1 - j:LRU_CONV - j, :] * _shift_rows(lx, halo, j)
    st.halo_lx[...] = lx[rows - SUBLANES:, :]

    xb = xr.astype(BF16)
    r = jax.nn.sigmoid(jnp.dot(xb, p.wa[...], preferred_element_type=F32) + p.ba[...])
    i = jax.nn.sigmoid(jnp.dot(xb, p.wx[...], preferred_element_type=F32) + p.bx[...])
    nl = -p.lam[...]
    softplus = jnp.maximum(nl, 0.0) + jnp.log1p(jnp.exp(-jnp.abs(nl)))
    log_a = (-LRU_C) * r * softplus
    a = jnp.exp(log_a)
    b = jnp.sqrt(-jnp.tanh(log_a) * (a * a + 1.0)) * (i * xr)

    pos = lax.bitwise_and(lax.broadcasted_iota(jnp.int32, (rows, w), 0), SUBLANES - 1)
    s = 1
    while s < SUBLANES:
        a, b = _scan_step(a, b, s, pos)
        s *= 2
    halves = w // LANES
    last = pl.ds(SUBLANES - 1, groups, stride=SUBLANES)
    for k in range(halves):
        st.a[k] = a[:, k * LANES:(k + 1) * LANES]
        st.b[k] = b[:, k * LANES:(k + 1) * LANES]
    at = jnp.concatenate([st.a[k, last, :] for k in range(halves)], axis=1)
    bt = jnp.concatenate([st.b[k, last, :] for k in range(halves)], axis=1)
    gpos = lax.broadcasted_iota(jnp.int32, (groups, w), 0)
    s = 1
    while s < groups:
        at, bt = _scan_step(at, bt, s, gpos)
        s *= 2
    h_in = fresh(st.h[0:1, :])
    h_end = bt + at * h_in
    st.h[0:1, :] = h_end[groups - 1:groups, :]
    st.c[...] = jnp.where(gpos >= 1, pltpu.roll(h_end, 1, 0), h_in)
    h = jnp.concatenate(
        [jnp.concatenate(
            [st.b[k, g * SUBLANES:(g + 1) * SUBLANES, :]
             + st.a[k, g * SUBLANES:(g + 1) * SUBLANES, :] * st.c[g:g + 1, k * LANES:(k + 1) * LANES]
             for g in range(groups)], axis=0) for k in range(halves)], axis=1)
    y_lru = _gelu_tanh(lg) * h
    return jnp.concatenate([y_sc, y_lru], axis=1).astype(BF16)


class _MixParams(NamedTuple):
    scw: Any
    lcw: Any
    lcb: Any
    wa: Any
    ba: Any
    wx: Any
    bx: Any
    lam: Any


class _MixState(NamedTuple):
    h: Any
    halo_sc: Any
    halo_lx: Any
    a: Any
    b: Any
    c: Any


def _out_ffn_kernel(h_ref, ya_ref, z0_ref, zn_ref, scw_ref, lcw_ref, lcb_ref, wa_ref, ba_ref,
                    wx_ref, bx_ref, lam_ref, woa_ref, wom_ref, g2_ref, wg_ref, wu_ref, wd_ref,
                    gf_ref, o_ref, ym_ref, hs_ref, halo_sc_ref, halo_lx_ref, a_ref, b_ref, c_ref,
                    *, final_norm, tiles_per_seq):
    i = pl.program_id(0)
    p = _MixParams(scw_ref, lcw_ref, lcb_ref, wa_ref, ba_ref, wx_ref, bx_ref, lam_ref)
    st = _MixState(hs_ref, halo_sc_ref, halo_lx_ref, a_ref, b_ref, c_ref)

    @pl.when(i == 0)
    def _():
        ym_ref[0] = _mix_tile(z0_ref, p, st, i == 0)

    nxt = i + 1
    ym_cur = ym_ref[lax.rem(i, 2)]
    ym_ref[lax.rem(nxt, 2)] = _mix_tile(zn_ref, p, st, lax.rem(nxt, tiles_per_seq) == 0)

    h = (h_ref[...]
         + lax.dot_general(ya_ref[...], woa_ref[...], (((0,), (0,)), ((), ())),
                           preferred_element_type=F32)
         + jnp.dot(ym_cur, wom_ref[...], preferred_element_type=F32))
    hn = _rms(h, g2_ref[...], RMS_EPS).astype(BF16)
    x = h + 0.5 * _swiglu(hn, wg_ref, wu_ref, wd_ref)
    if final_norm:
        x = _rms(x, gf_ref[...], RMS_EPS)
    o_ref[...] = x


def _out_ffn(h2, ya_t, zr, mix_params, woa, wom, g2, wg, wu, wd, gf, l, final_norm):
    n = h2.shape[0]
    n_tiles = n // ROW_TILE
    tiles_per_seq = ya_t.shape[2] // ROW_TILE
    w = SC_WIDTH
    row = lambda width: pl.BlockSpec((ROW_TILE, width), lambda i: (i, 0))
    par = lambda rows: pl.BlockSpec((None, rows, w), lambda i: (l, 0, 0))
    mix_half = SC_WIDTH + LRU_WIDTH
    return pl.pallas_call(
        functools.partial(_out_ffn_kernel, final_norm=final_norm, tiles_per_seq=tiles_per_seq),
        grid=(n_tiles,),
        in_specs=[
            row(D_MODEL),
            pl.BlockSpec((None, DA_WIDTH, ROW_TILE),
                         lambda i: (i // tiles_per_seq, 0, i % tiles_per_seq)),
            pl.BlockSpec((ROW_TILE, REST_WIDTH), lambda i: (0, 0)),
            pl.BlockSpec((ROW_TILE, REST_WIDTH), lambda i: (jnp.minimum(i + 1, n_tiles - 1), 0)),
            par(SC_KERNEL), par(LRU_CONV), par(1), par(w), par(1), par(w), par(1), par(1),
            _resident((None, DA_WIDTH, D_MODEL), lambda i: (l, 0, 0)),
            _resident((None, mix_half, D_MODEL), lambda i: (l, 0, 0)),
            pl.BlockSpec((None, 1, D_MODEL), lambda i: (l, 0, 0)),
            _resident((None, D_MODEL, D_FF), lambda i: (l, 0, 0)),
            _resident((None, D_MODEL, D_FF), lambda i: (l, 0, 0)),
            _resident((None, D_FF, D_MODEL), lambda i: (l, 0, 0)),
            pl.BlockSpec((1, D_MODEL), lambda i: (0, 0)),
        ],
        out_specs=row(D_MODEL),
        out_shape=jax.ShapeDtypeStruct((n, D_MODEL), F32),
        scratch_shapes=[
            pltpu.VMEM((2, ROW_TILE, mix_half), BF16),
            pltpu.VMEM((SUBLANES, w), F32),
            pltpu.VMEM((SUBLANES, w), F32),
            pltpu.VMEM((SUBLANES, w), F32),
            pltpu.VMEM((w // LANES, ROW_TILE, LANES), F32),
            pltpu.VMEM((w // LANES, ROW_TILE, LANES), F32),
            pltpu.VMEM((ROW_TILE // SUBLANES, w), F32),
        ],
        compiler_params=pltpu.CompilerParams(
            dimension_semantics=("arbitrary",), vmem_limit_bytes=VMEM_LIMIT_BYTES),
        name="out_ffn",
    )(h2, ya_t, zr, zr, *mix_params, woa, wom, g2, wg, wu, wd, gf)


def _block_diag(w):
    depth, nb, blk, _ = w.shape
    eye = jnp.eye(nb, dtype=w.dtype)
    return jnp.einsum('lnij,nm->lnimj', w, eye).reshape(depth, nb * blk, nb * blk)


def kernel(x, rel_bias, ffn1_norm, ffn1_gate, ffn1_up, ffn1_down, mix_norm, w_in, w_out, lam_q1, lam_k1, lam_q2, lam_k2, subln_gain, sc_conv_w, lru_conv_w, lru_conv_b, lru_wa, lru_ba, lru_wx, lru_bx, lru_lambda, ffn2_norm, ffn2_gate, ffn2_up, ffn2_down, final_norm):
    b, s, d = x.shape
    depth = w_in.shape[0]
    assert d == D_MODEL and s % ATTN_TILE == 0 and s % ROW_TILE == 0

    bf = lambda w: w.astype(BF16)
    vec = lambda v: v.reshape(depth, 1, v.shape[-1])
    wg1, wu1, wd1 = bf(ffn1_gate), bf(ffn1_up), bf(ffn1_down)
    wg2, wu2, wd2 = bf(ffn2_gate), bf(ffn2_up), bf(ffn2_down)
    col_scale = jnp.where(jnp.arange(QKV_WIDTH) < DA_WIDTH, Q_SCALE, 1.0).astype(F32)
    wq, wr = bf(w_in[:, :, :QKV_WIDTH] * col_scale), bf(w_in[:, :, QKV_WIDTH:])
    woa, wom = bf(w_out[:, :DA_WIDTH]), bf(w_out[:, DA_WIDTH:])
    wa, wx = bf(_block_diag(lru_wa)), bf(_block_diag(lru_wx))
    lam_params = jnp.stack([lam_q1, lam_k1, lam_q2, lam_k2], axis=1)
    g1, gm, g2 = vec(ffn1_norm), vec(mix_norm), vec(ffn2_norm)
    gs = subln_gain.reshape(depth, DA_VDIM, 1)
    lcb, lam = vec(lru_conv_b), vec(lru_lambda)
    ba, bx = vec(lru_ba.reshape(depth, -1)), vec(lru_bx.reshape(depth, -1))
    gf = final_norm.reshape(1, d)

    bias = _bias_tiles(rel_bias)

    x2 = x.reshape(b * s, d)
    for l in range(depth):
        lam_init = 0.8 - 0.6 * math.exp(-0.3 * l)
        h2, zq, zr = _ffn_in(x2, g1, wg1, wu1, wd1, gm, wq, wr, l)
        ya = _attention(zq.reshape(b, s, QKV_WIDTH), lam_params, gs, bias, l, lam_init)
        x2 = _out_ffn(h2, ya, zr, (sc_conv_w, lru_conv_w, lcb, wa, ba, wx, bx, lam),
                      woa, wom, g2, wg2, wu2, wd2, gf, l, l == depth - 1)
    return x2.reshape(b, s, d)
```

```python
import functools
import math
from typing import Any, NamedTuple

import jax
import jax.numpy as jnp
from jax import lax
from jax.experimental import pallas as pl
from jax.experimental.pallas import tpu as pltpu

F32 = jnp.float32
BF16 = jnp.bfloat16

D_MODEL = 1024
D_FF = 2816
DA_WIDTH = 512
SC_WIDTH = 256
LRU_WIDTH = 256
DA_HEAD_DIM = 64
DA_HEADS = 4
DA_VDIM = 2 * DA_HEAD_DIM
NUM_BUCKETS = 32
MAX_DISTANCE = 128
SUBLN_EPS = 1e-5
SC_KERNEL = 3
LRU_BLOCKS = 4
LRU_BLOCK = 64
LRU_CONV = 4
LRU_C = 8.0
RMS_EPS = 1e-6
NEG_INF = -1e30
QKV_WIDTH = 3 * DA_WIDTH
REST_WIDTH = 3 * SC_WIDTH + 2 * LRU_WIDTH
IN_WIDTH = QKV_WIDTH + REST_WIDTH
LOG2_E = math.log2(math.e)
Q_SCALE = DA_HEAD_DIM ** -0.5 * LOG2_E

LANES = 128
SUBLANES = 8
BF16_SUBLANES = 16
VT_ROWS = DA_VDIM + BF16_SUBLANES
VMEM_LIMIT_BYTES = 56 * 1024 * 1024
ROW_TILE = 512
ATTN_TILE = 512
SWEEP_UNROLL = 4


def _rms(x, g, eps):
    return x * lax.rsqrt(jnp.mean(x * x, axis=-1, keepdims=True) + eps) * g


def _swiglu(xn, wg_ref, wu_ref, wd_ref):
    g = jnp.dot(xn, wg_ref[...], preferred_element_type=F32)
    u = jnp.dot(xn, wu_ref[...], preferred_element_type=F32)
    a = (g * jax.nn.sigmoid(g) * u).astype(BF16)
    return jnp.dot(a, wd_ref[...], preferred_element_type=F32)


def _ffn_in_kernel(x_ref, g1_ref, wg_ref, wu_ref, wd_ref, gm_ref, wq_ref, wr_ref,
                   h_ref, zq_ref, zr_ref):
    half = ROW_TILE // 2
    for rows in (slice(0, half), slice(half, ROW_TILE)):
        x = x_ref[rows, :]
        xn = _rms(x, g1_ref[...], RMS_EPS).astype(BF16)
        h = x + 0.5 * _swiglu(xn, wg_ref, wu_ref, wd_ref)
        h_ref[rows, :] = h
        u = _rms(h, gm_ref[...], RMS_EPS).astype(BF16)
        zq_ref[rows, :] = jnp.dot(u, wq_ref[...], preferred_element_type=F32).astype(BF16)
        zr_ref[rows, :] = jnp.dot(u, wr_ref[...], preferred_element_type=F32)


def _resident(shape, index):
    return pl.BlockSpec(shape, index, pipeline_mode=pl.Buffered(1))


def _ffn_in(x2, g1, wg, wu, wd, gm, wq, wr, l):
    n = x2.shape[0]
    row = lambda w: pl.BlockSpec((ROW_TILE, w), lambda i: (i, 0))
    vec = pl.BlockSpec((None, 1, D_MODEL), lambda i: (l, 0, 0))
    return pl.pallas_call(
        _ffn_in_kernel,
        grid=(n // ROW_TILE,),
        in_specs=[
            row(D_MODEL), vec,
            _resident((None, D_MODEL, D_FF), lambda i: (l, 0, 0)),
            _resident((None, D_MODEL, D_FF), lambda i: (l, 0, 0)),
            _resident((None, D_FF, D_MODEL), lambda i: (l, 0, 0)),
            vec,
            _resident((None, D_MODEL, QKV_WIDTH), lambda i: (l, 0, 0)),
            _resident((None, D_MODEL, REST_WIDTH), lambda i: (l, 0, 0)),
        ],
        out_specs=[row(D_MODEL), row(QKV_WIDTH), row(REST_WIDTH)],
        out_shape=[
            jax.ShapeDtypeStruct((n, D_MODEL), F32),
            jax.ShapeDtypeStruct((n, QKV_WIDTH), BF16),
            jax.ShapeDtypeStruct((n, REST_WIDTH), F32),
        ],
        compiler_params=pltpu.CompilerParams(
            dimension_semantics=("arbitrary",), vmem_limit_bytes=VMEM_LIMIT_BYTES),
        name="ffn_in",
    )(x2, g1, wg, wu, wd, gm, wq, wr)


def _bias_kernel(rb_ref, o_ref):
    h = pl.program_id(0)
    t = ATTN_TILE
    max_exact = NUM_BUCKETS // 2
    key = lax.broadcasted_iota(jnp.int32, (t, t), 0)
    qry = lax.broadcasted_iota(jnp.int32, (t, t), 1)
    for off in range(2):
        dist = qry - key + off * t
        n = jnp.maximum(dist, 0)
        nf = jnp.maximum(n, 1).astype(F32)
        large = max_exact + (jnp.log(nf / max_exact) / math.log(MAX_DISTANCE / max_exact)
                             * (NUM_BUCKETS - max_exact)).astype(jnp.int32)
        large = jnp.minimum(large, NUM_BUCKETS - 1)
        bucket = jnp.where(n < max_exact, n, large)
        for mp in range(2):
            hm = 2 * h + mp
            val = jnp.zeros((t, t), F32)
            for j in range(NUM_BUCKETS):
                val = jnp.where(bucket == j, rb_ref[j, hm], val)
            far = rb_ref[NUM_BUCKETS - 1, hm]
            o_ref[off, :, mp * t:(mp + 1) * t] = jnp.where(dist >= 0, (val - far) * LOG2_E,
                                                           NEG_INF)


def _bias_tiles(rel_bias):
    t = ATTN_TILE
    return pl.pallas_call(
        _bias_kernel,
        grid=(DA_HEADS,),
        in_specs=[pl.BlockSpec(memory_space=pltpu.SMEM)],
        out_specs=pl.BlockSpec((None, 2, t, 2 * t), lambda i: (i, 0, 0, 0)),
        out_shape=jax.ShapeDtypeStruct((DA_HEADS, 2, t, 2 * t), F32),
        compiler_params=pltpu.CompilerParams(dimension_semantics=("arbitrary",)),
        name="bias_tiles",
    )(rel_bias)


def _attn_kernel(lam_ref, g_ref, bias_ref, q_ref, k_ref, v_ref, o_ref,
                 qq_ref, vt_ref, s_ref, mx_ref, m_ref, acc_ref, *, lam_init, nq):
    t = ATTN_TILE
    chan = lax.broadcasted_iota(jnp.int32, (LANES, t), 0)
    for qi in range(nq):
        q_t = q_ref[qi * t:(qi + 1) * t, :].T
        zero = jnp.zeros_like(q_t)
        qq_ref[qi, :, 0:t] = jnp.where(chan < DA_HEAD_DIM, q_t, zero)
        qq_ref[qi, :, t:2 * t] = jnp.where(chan >= DA_HEAD_DIM, q_t, zero)

    lp = lam_ref[...]
    lam = (jnp.exp(jnp.sum(lp[0:1] * lp[1:2], axis=1, keepdims=True))
           - jnp.exp(jnp.sum(lp[2:3] * lp[3:4], axis=1, keepdims=True)) + lam_init)
    gain = g_ref[...] * (1.0 - lam_init)

    ones_row = lax.broadcasted_iota(jnp.int32, (VT_ROWS - DA_VDIM, t), 0) == 0
    for kj in range(nq):
        vt_ref[kj, 0:DA_VDIM, :] = v_ref[kj * t:(kj + 1) * t, :].T
        vt_ref[kj, DA_VDIM:VT_ROWS, :] = jnp.where(ones_row, 1.0, 0.0).astype(BF16)

    m_ref[...] = jnp.full(m_ref.shape, NEG_INF, F32)
    acc_ref[...] = jnp.zeros(acc_ref.shape, F32)

    def key_rows(kj):
        return pl.ds(kj * t, t) if isinstance(kj, int) else pl.ds(pl.multiple_of(kj * t, t), t)

    def scores(qi, kj, off, buf):
        s = jnp.dot(k_ref[key_rows(kj), :], qq_ref[qi], preferred_element_type=F32)
        if off is not None:
            s = s + bias_ref[off]
        s_ref[buf] = s
        mx_ref[buf] = jnp.max(s, axis=0, keepdims=True)

    def consume(qi, kj, off, buf):
        m_old = m_ref[qi]
        m_new = jnp.maximum(m_old, mx_ref[buf])
        alpha = jnp.exp2(m_old - m_new)
        p = jnp.exp2(s_ref[buf] - m_new)
        pv = jnp.dot(vt_ref[kj], p.astype(BF16), preferred_element_type=F32)
        acc_ref[qi] = alpha * acc_ref[qi] + pv
        m_ref[qi] = m_new

    def sweep(n_steps, coords):
        if n_steps == 0:
            return
        scores(*coords(0), 0)

        def group(i, carry):
            for u in range(SWEEP_UNROLL):
                n = SWEEP_UNROLL * i + u
                scores(*coords(n + 1), (u + 1) % 2)
                consume(*coords(n), u % 2)
            return carry

        n_groups = (n_steps - 1) // SWEEP_UNROLL
        lax.fori_loop(0, n_groups, group, 0)
        for n in range(n_groups * SWEEP_UNROLL, n_steps):
            if n + 1 < n_steps:
                scores(*coords(n + 1), (n + 1) % 2)
            consume(*coords(n), n % 2)

    def biased_coords(n):
        if isinstance(n, int):
            qi, off = (n + 1) // 2, n % 2
        else:
            qi, off = lax.shift_right_logical(n + 1, 1), lax.bitwise_and(n, 1)
        return qi, qi - off, off

    def far_coords(n):
        first = lambda qi: (qi - 1) * (qi - 2) // 2
        if isinstance(n, int):
            qi = max(c for c in range(2, nq) if first(c) <= n)
            return qi, n - first(qi), None
        qi = 2
        for c in range(3, nq):
            qi = qi + (n >= first(c)).astype(jnp.int32)
        return qi, n - lax.shift_right_logical((qi - 1) * (qi - 2), 1), None

    sweep(2 * nq - 1, biased_coords)
    sweep((nq - 1) * (nq - 2) // 2, far_coords)

    for qi in range(nq):
        on = acc_ref[qi, 0:DA_VDIM, :] / acc_ref[qi, DA_VDIM:DA_VDIM + 1, :]
        o = on[:, 0:t] - lam * on[:, t:2 * t]
        y = o * lax.rsqrt(jnp.mean(o * o, axis=0, keepdims=True) + SUBLN_EPS) * gain
        o_ref[:, qi * t:(qi + 1) * t] = y.astype(o_ref.dtype)


def _attention(zq3, lam_params, subln_g, bias, l, lam_init):
    b, s, _ = zq3.shape
    t = ATTN_TILE
    nq = s // t
    return pl.pallas_call(
        functools.partial(_attn_kernel, lam_init=lam_init, nq=nq),
        grid=(b, DA_HEADS),
        in_specs=[
            pl.BlockSpec((None, 4, DA_HEAD_DIM), lambda bi, h: (l, 0, 0)),
            pl.BlockSpec((None, DA_VDIM, 1), lambda bi, h: (l, 0, 0)),
            pl.BlockSpec((None, 2, t, 2 * t), lambda bi, h: (h, 0, 0, 0)),
            pl.BlockSpec((None, s, LANES), lambda bi, h: (bi, 0, h)),
            pl.BlockSpec((None, s, LANES), lambda bi, h: (bi, 0, DA_HEADS + h)),
            pl.BlockSpec((None, s, LANES), lambda bi, h: (bi, 0, 2 * DA_HEADS + h)),
        ],
        out_specs=pl.BlockSpec((None, DA_VDIM, s), lambda bi, h: (bi, h, 0)),
        out_shape=jax.ShapeDtypeStruct((b, DA_WIDTH, s), BF16),
        scratch_shapes=[
            pltpu.VMEM((nq, LANES, 2 * t), BF16),
            pltpu.VMEM((nq, VT_ROWS, t), BF16),
            pltpu.VMEM((2, t, 2 * t), F32),
            pltpu.VMEM((2, 1, 2 * t), F32),
            pltpu.VMEM((nq, 1, 2 * t), F32),
            pltpu.VMEM((nq, VT_ROWS, 2 * t), F32),
        ],
        compiler_params=pltpu.CompilerParams(
            dimension_semantics=("arbitrary", "arbitrary"),
            vmem_limit_bytes=VMEM_LIMIT_BYTES),
        name="diff_attn",
    )(lam_params, subln_g, bias, zq3, zq3, zq3)


def _shift_rows(x, halo, j):
    r = pltpu.roll(x, j, 0)
    head_row = lax.broadcasted_iota(jnp.int32, halo.shape, 0)
    head = jnp.where(head_row < j, pltpu.roll(halo, j, 0), r[:SUBLANES])
    return jnp.concatenate([head, r[SUBLANES:]], axis=0)


def _gelu_tanh(x):
    return 0.5 * x * (1.0 + jnp.tanh(math.sqrt(2.0 / math.pi) * (x + 0.044715 * (x * x * x))))


def _scan_step(a, b, s, pos):
    keep = pos >= s
    a_sh = jnp.where(keep, pltpu.roll(a, s, 0), 1.0)
    b_sh = jnp.where(keep, pltpu.roll(b, s, 0), 0.0)
    return a * a_sh, a * b_sh + b


def _mix_tile(z_ref, p, st, reset):
    rows = z_ref.shape[0]
    w = SC_WIDTH
    groups = rows // SUBLANES
    fresh = lambda x: jnp.where(reset, jnp.zeros_like(x), x)

    sc_b = z_ref[:, 0:w]
    cx = z_ref[:, w:2 * w] * z_ref[:, 2 * w:3 * w]
    halo = fresh(st.halo_sc[...])
    conv = p.scw[SC_KERNEL - 1:SC_KERNEL, :] * cx
    for j in range(1, SC_KERNEL):
        conv = conv + p.scw[SC_KERNEL - 1 - j:SC_KERNEL - j, :] * _shift_rows(cx, halo, j)
    st.halo_sc[...] = cx[rows - SUBLANES:, :]
    y_sc = sc_b * conv

    lx = z_ref[:, 3 * w:4 * w]
    lg = z_ref[:, 4 * w:5 * w]
    halo = fresh(st.halo_lx[...])
    xr = p.lcw[LRU_CONV - 1:LRU_CONV, :] * lx + p.lcb[...]
    for j in range(1, LRU_CONV):
        xr = xr + p.lcw[LRU_CONV - 1 - j:LRU_CONV - j, :] * _shift_rows(lx, halo, j)
    st.halo_lx[...] = lx[rows - SUBLANES:, :]

    xb = xr.astype(BF16)
    r = jax.nn.sigmoid(jnp.dot(xb, p.wa[...], preferred_element_type=F32) + p.ba[...])
    i = jax.nn.sigmoid(jnp.dot(xb, p.wx[...], preferred_element_type=F32) + p.bx[...])
    nl = -p.lam[...]
    softplus = jnp.maximum(nl, 0.0) + jnp.log1p(jnp.exp(-jnp.abs(nl)))
    log_a = (-LRU_C) * r * softplus
    a = jnp.exp(log_a)
    b = jnp.sqrt(-jnp.tanh(log_a) * (a * a + 1.0)) * (i * xr)

    pos = lax.bitwise_and(lax.broadcasted_iota(jnp.int32, (rows, w), 0), SUBLANES - 1)
    s = 1
    while s < SUBLANES:
        a, b = _scan_step(a, b, s, pos)
        s *= 2
    halves = w // LANES
    last = pl.ds(SUBLANES - 1, groups, stride=SUBLANES)
    for k in range(halves):
        st.a[k] = a[:, k * LANES:(k + 1) * LANES]
        st.b[k] = b[:, k * LANES:(k + 1) * LANES]
    at = jnp.concatenate([st.a[k, last, :] for k in range(halves)], axis=1)
    bt = jnp.concatenate([st.b[k, last, :] for k in range(halves)], axis=1)
    gpos = lax.broadcasted_iota(jnp.int32, (groups, w), 0)
    s = 1
    while s < groups:
        at, bt = _scan_step(at, bt, s, gpos)
        s *= 2
    h_in = fresh(st.h[0:1, :])
    h_end = bt + at * h_in
    st.h[0:1, :] = h_end[groups - 1:groups, :]
    st.c[...] = jnp.where(gpos >= 1, pltpu.roll(h_end, 1, 0), h_in)
    h = jnp.concatenate(
        [jnp.concatenate(
            [st.b[k, g * SUBLANES:(g + 1) * SUBLANES, :]
             + st.a[k, g * SUBLANES:(g + 1) * SUBLANES, :] * st.c[g:g + 1, k * LANES:(k + 1) * LANES]
             for g in range(groups)], axis=0) for k in range(halves)], axis=1)
    y_lru = _gelu_tanh(lg) * h
    return jnp.concatenate([y_sc, y_lru], axis=1).astype(BF16)


class _MixParams(NamedTuple):
    scw: Any
    lcw: Any
    lcb: Any
    wa: Any
    ba: Any
    wx: Any
    bx: Any
    lam: Any


class _MixState(NamedTuple):
    h: Any
    halo_sc: Any
    halo_lx: Any
    a: Any
    b: Any
    c: Any


def _out_ffn_kernel(h_ref, ya_ref, z0_ref, zn_ref, scw_ref, lcw_ref, lcb_ref, wa_ref, ba_ref,
                    wx_ref, bx_ref, lam_ref, woa_ref, wom_ref, g2_ref, wg_ref, wu_ref, wd_ref,
                    gf_ref, o_ref, ym_ref, hs_ref, halo_sc_ref, halo_lx_ref, a_ref, b_ref, c_ref,
                    *, final_norm, tiles_per_seq):
    i = pl.program_id(0)
    p = _MixParams(scw_ref, lcw_ref, lcb_ref, wa_ref, ba_ref, wx_ref, bx_ref, lam_ref)
    st = _MixState(hs_ref, halo_sc_ref, halo_lx_ref, a_ref, b_ref, c_ref)

    @pl.when(i == 0)
    def _():
        ym_ref[0] = _mix_tile(z0_ref, p, st, i == 0)

    nxt = i + 1
    ym_cur = ym_ref[lax.rem(i, 2)]
    ym_ref[lax.rem(nxt, 2)] = _mix_tile(zn_ref, p, st, lax.rem(nxt, tiles_per_seq) == 0)

    h = (h_ref[...]
         + lax.dot_general(ya_ref[...], woa_ref[...], (((0,), (0,)), ((), ())),
                           preferred_element_type=F32)
         + jnp.dot(ym_cur, wom_ref[...], preferred_element_type=F32))
    hn = _rms(h, g2_ref[...], RMS_EPS).astype(BF16)
    x = h + 0.5 * _swiglu(hn, wg_ref, wu_ref, wd_ref)
    if final_norm:
        x = _rms(x, gf_ref[...], RMS_EPS)
    o_ref[...] = x


def _out_ffn(h2, ya_t, zr, mix_params, woa, wom, g2, wg, wu, wd, gf, l, final_norm):
    n = h2.shape[0]
    n_tiles = n // ROW_TILE
    tiles_per_seq = ya_t.shape[2] // ROW_TILE
    w = SC_WIDTH
    row = lambda width: pl.BlockSpec((ROW_TILE, width), lambda i: (i, 0))
    par = lambda rows: pl.BlockSpec((None, rows, w), lambda i: (l, 0, 0))
    mix_half = SC_WIDTH + LRU_WIDTH
    return pl.pallas_call(
        functools.partial(_out_ffn_kernel, final_norm=final_norm, tiles_per_seq=tiles_per_seq),
        grid=(n_tiles,),
        in_specs=[
            row(D_MODEL),
            pl.BlockSpec((None, DA_WIDTH, ROW_TILE),
                         lambda i: (i // tiles_per_seq, 0, i % tiles_per_seq)),
            pl.BlockSpec((ROW_TILE, REST_WIDTH), lambda i: (0, 0)),
            pl.BlockSpec((ROW_TILE, REST_WIDTH), lambda i: (jnp.minimum(i + 1, n_tiles - 1), 0)),
            par(SC_KERNEL), par(LRU_CONV), par(1), par(w), par(1), par(w), par(1), par(1),
            _resident((None, DA_WIDTH, D_MODEL), lambda i: (l, 0, 0)),
            _resident((None, mix_half, D_MODEL), lambda i: (l, 0, 0)),
            pl.BlockSpec((None, 1, D_MODEL), lambda i: (l, 0, 0)),
            _resident((None, D_MODEL, D_FF), lambda i: (l, 0, 0)),
            _resident((None, D_MODEL, D_FF), lambda i: (l, 0, 0)),
            _resident((None, D_FF, D_MODEL), lambda i: (l, 0, 0)),
            pl.BlockSpec((1, D_MODEL), lambda i: (0, 0)),
        ],
        out_specs=row(D_MODEL),
        out_shape=jax.ShapeDtypeStruct((n, D_MODEL), F32),
        scratch_shapes=[
            pltpu.VMEM((2, ROW_TILE, mix_half), BF16),
            pltpu.VMEM((SUBLANES, w), F32),
            pltpu.VMEM((SUBLANES, w), F32),
            pltpu.VMEM((SUBLANES, w), F32),
            pltpu.VMEM((w // LANES, ROW_TILE, LANES), F32),
            pltpu.VMEM((w // LANES, ROW_TILE, LANES), F32),
            pltpu.VMEM((ROW_TILE // SUBLANES, w), F32),
        ],
        compiler_params=pltpu.CompilerParams(
            dimension_semantics=("arbitrary",), vmem_limit_bytes=VMEM_LIMIT_BYTES),
        name="out_ffn",
    )(h2, ya_t, zr, zr, *mix_params, woa, wom, g2, wg, wu, wd, gf)


def _block_diag(w):
    depth, nb, blk, _ = w.shape
    eye = jnp.eye(nb, dtype=w.dtype)
    return jnp.einsum('lnij,nm->lnimj', w, eye).reshape(depth, nb * blk, nb * blk)


def kernel(x, rel_bias, ffn1_norm, ffn1_gate, ffn1_up, ffn1_down, mix_norm, w_in, w_out, lam_q1, lam_k1, lam_q2, lam_k2, subln_gain, sc_conv_w, lru_conv_w, lru_conv_b, lru_wa, lru_ba, lru_wx, lru_bx, lru_lambda, ffn2_norm, ffn2_gate, ffn2_up, ffn2_down, final_norm):
    b, s, d = x.shape
    depth = w_in.shape[0]
    assert d == D_MODEL and s % ATTN_TILE == 0 and s % ROW_TILE == 0

    bf = lambda w: w.astype(BF16)
    vec = lambda v: v.reshape(depth, 1, v.shape[-1])
    wg1, wu1, wd1 = bf(ffn1_gate), bf(ffn1_up), bf(ffn1_down)
    wg2, wu2, wd2 = bf(ffn2_gate), bf(ffn2_up), bf(ffn2_down)
    col_scale = jnp.where(jnp.arange(QKV_WIDTH) < DA_WIDTH, Q_SCALE, 1.0).astype(F32)
    wq, wr = bf(w_in[:, :, :QKV_WIDTH] * col_scale), bf(w_in[:, :, QKV_WIDTH:])
    woa, wom = bf(w_out[:, :DA_WIDTH]), bf(w_out[:, DA_WIDTH:])
    wa, wx = bf(_block_diag(lru_wa)), bf(_block_diag(lru_wx))
    lam_params = jnp.stack([lam_q1, lam_k1, lam_q2, lam_k2], axis=1)
    g1, gm, g2 = vec(ffn1_norm), vec(mix_norm), vec(ffn2_norm)
    gs = subln_gain.reshape(depth, DA_VDIM, 1)
    lcb, lam = vec(lru_conv_b), vec(lru_lambda)
    ba, bx = vec(lru_ba.reshape(depth, -1)), vec(lru_bx.reshape(depth, -1))
    gf = final_norm.reshape(1, d)

    bias = _bias_tiles(rel_bias)

    x2 = x.reshape(b * s, d)
    for l in range(depth):
        lam_init = 0.8 - 0.6 * math.exp(-0.3 * l)
        h2, zq, zr = _ffn_in(x2, g1, wg1, wu1, wd1, gm, wq, wr, l)
        ya = _attention(zq.reshape(b, s, QKV_WIDTH), lam_params, gs, bias, l, lam_init)
        x2 = _out_ffn(h2, ya, zr, (sc_conv_w, lru_conv_w, lcb, wa, ba, wx, bx, lam),
                      woa, wom, g2, wg2, wu2, wd2, gf, l, l == depth - 1)
    return x2.reshape(b, s, d)
```

```python
import functools
import math
from typing import Any, NamedTuple

import jax
import jax.numpy as jnp
from jax import lax
from jax.experimental import pallas as pl
from jax.experimental.pallas import tpu as pltpu

F32 = jnp.float32
BF16 = jnp.bfloat16

D_MODEL = 1024
D_FF = 2816
DA_WIDTH = 512
SC_WIDTH = 256
LRU_WIDTH = 256
DA_HEAD_DIM = 64
DA_HEADS = 4
DA_VDIM = 2 * DA_HEAD_DIM
NUM_BUCKETS = 32
MAX_DISTANCE = 128
SUBLN_EPS = 1e-5
SC_KERNEL = 3
LRU_BLOCKS = 4
LRU_BLOCK = 64
LRU_CONV = 4
LRU_C = 8.0
RMS_EPS = 1e-6
NEG_INF = -1e30
QKV_WIDTH = 3 * DA_WIDTH
REST_WIDTH = 3 * SC_WIDTH + 2 * LRU_WIDTH
IN_WIDTH = QKV_WIDTH + REST_WIDTH
LOG2_E = math.log2(math.e)
Q_SCALE = DA_HEAD_DIM ** -0.5 * LOG2_E

LANES = 128
SUBLANES = 8
BF16_SUBLANES = 16
VT_ROWS = DA_VDIM + BF16_SUBLANES
VMEM_LIMIT_BYTES = 56 * 1024 * 1024
ROW_TILE = 512
ATTN_TILE = 512
SWEEP_UNROLL = 4


def _rms(x, g, eps):
    return x * lax.rsqrt(jnp.mean(x * x, axis=-1, keepdims=True) + eps) * g


def _swiglu(xn, wg_ref, wu_ref, wd_ref):
    g = jnp.dot(xn, wg_ref[...], preferred_element_type=F32)
    u = jnp.dot(xn, wu_ref[...], preferred_element_type=F32)
    a = (g * jax.nn.sigmoid(g) * u).astype(BF16)
    return jnp.dot(a, wd_ref[...], preferred_element_type=F32)


def _resident(shape, index):
    return pl.BlockSpec(shape, index, pipeline_mode=pl.Buffered(1))


class _CastJob(NamedTuple):
    src: Any
    layer: int
    chunks: int
    in_proj: bool


def _cast_job(src, layer, n_steps, in_proj=False):
    rows = src.shape[1]
    chunks = max(c for c in range(1, n_steps + 1)
                 if rows % c == 0 and (rows // c) % BF16_SUBLANES == 0)
    return _CastJob(src, layer, chunks, in_proj)


def _cast_widths(job):
    return (QKV_WIDTH, REST_WIDTH) if job.in_proj else (job.src.shape[2],)


def _cast_in_spec(job):
    rows, last = job.src.shape[1] // job.chunks, job.chunks - 1
    return pl.BlockSpec((None, rows, job.src.shape[2]),
                        lambda i: (job.layer, jnp.minimum(i, last), 0))


def _cast_out_specs(job):
    rows, last = job.src.shape[1] // job.chunks, job.chunks - 1
    return [pl.BlockSpec((rows, w), lambda i: (jnp.minimum(i, last), 0)) for w in _cast_widths(job)]


def _cast_out_shapes(job):
    return [jax.ShapeDtypeStruct((job.src.shape[1], w), BF16) for w in _cast_widths(job)]


def _cast_chunks(jobs, in_refs, out_refs, n_steps):
    out_refs = list(out_refs)
    for job, in_ref in zip(jobs, in_refs):
        outs = [out_refs.pop(0) for _ in _cast_widths(job)]

        def body(job=job, in_ref=in_ref, outs=outs):
            c = in_ref[...]
            if job.in_proj:
                col = lax.broadcasted_iota(jnp.int32, (1, QKV_WIDTH), 1)
                scale = jnp.where(col < DA_WIDTH, Q_SCALE, 1.0)
                outs[0][...] = (c[:, :QKV_WIDTH] * scale).astype(BF16)
                outs[1][...] = c[:, QKV_WIDTH:].astype(BF16)
            else:
                outs[0][...] = c.astype(BF16)

        if job.chunks == n_steps:
            body()
        else:
            pl.when(pl.program_id(0) < job.chunks)(body)


def _ffn_in_kernel(*refs, jobs, n_steps):
    (x_ref, g1_ref, wg_ref, wu_ref, wd_ref, gm_ref, wq_ref, wr_ref), refs = refs[:8], refs[8:]
    cast_in, refs = refs[:len(jobs)], refs[len(jobs):]
    (h_ref, zq_ref, zr_ref), cast_out = refs[:3], refs[3:]
    _cast_chunks(jobs, cast_in, cast_out, n_steps)
    half = ROW_TILE // 2
    for rows in (slice(0, half), slice(half, ROW_TILE)):
        x = x_ref[rows, :]
        xn = _rms(x, g1_ref[...], RMS_EPS).astype(BF16)
        h = x + 0.5 * _swiglu(xn, wg_ref, wu_ref, wd_ref)
        h_ref[rows, :] = h
        u = _rms(h, gm_ref[...], RMS_EPS).astype(BF16)
        zq_ref[rows, :] = jnp.dot(u, wq_ref[...], preferred_element_type=F32).astype(BF16)
        zr_ref[rows, :] = jnp.dot(u, wr_ref[...], preferred_element_type=F32)


def _ffn_in(x2, g1, wg, wu, wd, gm, wq, wr, l, cast_srcs):
    n = x2.shape[0]
    n_steps = n // ROW_TILE
    jobs = tuple(_cast_job(src, l, n_steps) for src in cast_srcs)
    row = lambda w: pl.BlockSpec((ROW_TILE, w), lambda i: (i, 0))
    vec = pl.BlockSpec((None, 1, D_MODEL), lambda i: (l, 0, 0))
    whole = lambda w: _resident(w.shape, lambda i: (0, 0))
    return pl.pallas_call(
        functools.partial(_ffn_in_kernel, jobs=jobs, n_steps=n_steps),
        grid=(n_steps,),
        in_specs=[row(D_MODEL), vec, whole(wg), whole(wu), whole(wd), vec, whole(wq), whole(wr)]
        + [_cast_in_spec(j) for j in jobs],
        out_specs=[row(D_MODEL), row(QKV_WIDTH), row(REST_WIDTH)]
        + [sp for j in jobs for sp in _cast_out_specs(j)],
        out_shape=[
            jax.ShapeDtypeStruct((n, D_MODEL), F32),
            jax.ShapeDtypeStruct((n, QKV_WIDTH), BF16),
            jax.ShapeDtypeStruct((n, REST_WIDTH), F32),
        ] + [sh for j in jobs for sh in _cast_out_shapes(j)],
        compiler_params=pltpu.CompilerParams(
            dimension_semantics=("arbitrary",), vmem_limit_bytes=VMEM_LIMIT_BYTES),
        name="ffn_in",
    )(x2, g1, wg, wu, wd, gm, wq, wr, *[j.src for j in jobs])


def _bias_kernel(rb_ref, o_ref):
    h = pl.program_id(0)
    t = ATTN_TILE
    max_exact = NUM_BUCKETS // 2
    key = lax.broadcasted_iota(jnp.int32, (t, t), 0)
    qry = lax.broadcasted_iota(jnp.int32, (t, t), 1)
    for off in range(2):
        dist = qry - key + off * t
        n = jnp.maximum(dist, 0)
        nf = jnp.maximum(n, 1).astype(F32)
        large = max_exact + (jnp.log(nf / max_exact) / math.log(MAX_DISTANCE / max_exact)
                             * (NUM_BUCKETS - max_exact)).astype(jnp.int32)
        large = jnp.minimum(large, NUM_BUCKETS - 1)
        bucket = jnp.where(n < max_exact, n, large)
        for mp in range(2):
            hm = 2 * h + mp
            val = jnp.zeros((t, t), F32)
            for j in range(NUM_BUCKETS):
                val = jnp.where(bucket == j, rb_ref[j, hm], val)
            far = rb_ref[NUM_BUCKETS - 1, hm]
            o_ref[off, :, mp * t:(mp + 1) * t] = jnp.where(dist >= 0, (val - far) * LOG2_E,
                                                           NEG_INF)


def _bias_tiles(rel_bias):
    t = ATTN_TILE
    return pl.pallas_call(
        _bias_kernel,
        grid=(DA_HEADS,),
        in_specs=[pl.BlockSpec(memory_space=pltpu.SMEM)],
        out_specs=pl.BlockSpec((None, 2, t, 2 * t), lambda i: (i, 0, 0, 0)),
        out_shape=jax.ShapeDtypeStruct((DA_HEADS, 2, t, 2 * t), F32),
        compiler_params=pltpu.CompilerParams(dimension_semantics=("arbitrary",)),
        name="bias_tiles",
    )(rel_bias)


def _attn_kernel(lam_ref, g_ref, bias_ref, q_ref, k_ref, v_ref, o_ref,
                 qq_ref, vt_ref, s_ref, mx_ref, m_ref, acc_ref, *, lam_init, nq):
    t = ATTN_TILE
    chan = lax.broadcasted_iota(jnp.int32, (LANES, t), 0)
    for qi in range(nq):
        q_t = q_ref[qi * t:(qi + 1) * t, :].T
        zero = jnp.zeros_like(q_t)
        qq_ref[qi, :, 0:t] = jnp.where(chan < DA_HEAD_DIM, q_t, zero)
        qq_ref[qi, :, t:2 * t] = jnp.where(chan >= DA_HEAD_DIM, q_t, zero)

    lp = lam_ref[...]
    lam = (jnp.exp(jnp.sum(lp[0:1] * lp[1:2], axis=1, keepdims=True))
           - jnp.exp(jnp.sum(lp[2:3] * lp[3:4], axis=1, keepdims=True)) + lam_init)
    gain = g_ref[...] * (1.0 - lam_init)

    ones_row = lax.broadcasted_iota(jnp.int32, (VT_ROWS - DA_VDIM, t), 0) == 0
    for kj in range(nq):
        vt_ref[kj, 0:DA_VDIM, :] = v_ref[kj * t:(kj + 1) * t, :].T
        vt_ref[kj, DA_VDIM:VT_ROWS, :] = jnp.where(ones_row, 1.0, 0.0).astype(BF16)

    m_ref[...] = jnp.full(m_ref.shape, NEG_INF, F32)
    acc_ref[...] = jnp.zeros(acc_ref.shape, F32)

    def key_rows(kj):
        return pl.ds(kj * t, t) if isinstance(kj, int) else pl.ds(pl.multiple_of(kj * t, t), t)

    def scores(qi, kj, off, buf):
        s = jnp.dot(k_ref[key_rows(kj), :], qq_ref[qi], preferred_element_type=F32)
        if off is not None:
            s = s + bias_ref[off]
        s_ref[buf] = s
        mx_ref[buf] = jnp.max(s, axis=0, keepdims=True)

    def consume(qi, kj, off, buf):
        m_old = m_ref[qi]
        m_new = jnp.maximum(m_old, mx_ref[buf])
        alpha = jnp.exp2(m_old - m_new)
        p = jnp.exp2(s_ref[buf] - m_new)
        pv = jnp.dot(vt_ref[kj], p.astype(BF16), preferred_element_type=F32)
        acc_ref[qi] = alpha * acc_ref[qi] + pv
        m_ref[qi] = m_new

    def sweep(n_steps, coords):
        if n_steps == 0:
            return
        scores(*coords(0), 0)

        def group(i, carry):
            for u in range(SWEEP_UNROLL):
                n = SWEEP_UNROLL * i + u
                scores(*coords(n + 1), (u + 1) % 2)
                consume(*coords(n), u % 2)
            return carry

        n_groups = (n_steps - 1) // SWEEP_UNROLL
        lax.fori_loop(0, n_groups, group, 0)
        for n in range(n_groups * SWEEP_UNROLL, n_steps):
            if n + 1 < n_steps:
                scores(*coords(n + 1), (n + 1) % 2)
            consume(*coords(n), n % 2)

    def biased_coords(n):
        if isinstance(n, int):
            qi, off = (n + 1) // 2, n % 2
        else:
            qi, off = lax.shift_right_logical(n + 1, 1), lax.bitwise_and(n, 1)
        return qi, qi - off, off

    def far_coords(n):
        first = lambda qi: (qi - 1) * (qi - 2) // 2
        if isinstance(n, int):
            qi = max(c for c in range(2, nq) if first(c) <= n)
            return qi, n - first(qi), None
        qi = 2
        for c in range(3, nq):
            qi = qi + (n >= first(c)).astype(jnp.int32)
        return qi, n - lax.shift_right_logical((qi - 1) * (qi - 2), 1), None

    sweep(2 * nq - 1, biased_coords)
    sweep((nq - 1) * (nq - 2) // 2, far_coords)

    for qi in range(nq):
        on = acc_ref[qi, 0:DA_VDIM, :] / acc_ref[qi, DA_VDIM:DA_VDIM + 1, :]
        o = on[:, 0:t] - lam * on[:, t:2 * t]
        y = o * lax.rsqrt(jnp.mean(o * o, axis=0, keepdims=True) + SUBLN_EPS) * gain
        o_ref[:, qi * t:(qi + 1) * t] = y.astype(o_ref.dtype)


def _attention(zq3, lam_params, subln_g, bias, l, lam_init):
    b, s, _ = zq3.shape
    t = ATTN_TILE
    nq = s // t
    return pl.pallas_call(
        functools.partial(_attn_kernel, lam_init=lam_init, nq=nq),
        grid=(b, DA_HEADS),
        in_specs=[
            pl.BlockSpec((None, 4, DA_HEAD_DIM), lambda bi, h: (l, 0, 0)),
            pl.BlockSpec((None, DA_VDIM, 1), lambda bi, h: (l, 0, 0)),
            pl.BlockSpec((None, 2, t, 2 * t), lambda bi, h: (h, 0, 0, 0)),
            pl.BlockSpec((None, s, LANES), lambda bi, h: (bi, 0, h)),
            pl.BlockSpec((None, s, LANES), lambda bi, h: (bi, 0, DA_HEADS + h)),
            pl.BlockSpec((None, s, LANES), lambda bi, h: (bi, 0, 2 * DA_HEADS + h)),
        ],
        out_specs=pl.BlockSpec((None, DA_VDIM, s), lambda bi, h: (bi, h, 0)),
        out_shape=jax.ShapeDtypeStruct((b, DA_WIDTH, s), BF16),
        scratch_shapes=[
            pltpu.VMEM((nq, LANES, 2 * t), BF16),
            pltpu.VMEM((nq, VT_ROWS, t), BF16),
            pltpu.VMEM((2, t, 2 * t), F32),
            pltpu.VMEM((2, 1, 2 * t), F32),
            pltpu.VMEM((nq, 1, 2 * t), F32),
            pltpu.VMEM((nq, VT_ROWS, 2 * t), F32),
        ],
        compiler_params=pltpu.CompilerParams(
            dimension_semantics=("arbitrary", "arbitrary"),
            vmem_limit_bytes=VMEM_LIMIT_BYTES),
        name="diff_attn",
    )(lam_params, subln_g, bias, zq3, zq3, zq3)


def _shift_rows(x, halo, j):
    r = pltpu.roll(x, j, 0)
    head_row = lax.broadcasted_iota(jnp.int32, halo.shape, 0)
    head = jnp.where(head_row < j, pltpu.roll(halo, j, 0), r[:SUBLANES])
    return jnp.concatenate([head, r[SUBLANES:]], axis=0)


def _gelu_tanh(x):
    return 0.5 * x * (1.0 + jnp.tanh(math.sqrt(2.0 / math.pi) * (x + 0.044715 * (x * x * x))))


def _scan_step(a, b, s, pos):
    keep = pos >= s
    a_sh = jnp.where(keep, pltpu.roll(a, s, 0), 1.0)
    b_sh = jnp.where(keep, pltpu.roll(b, s, 0), 0.0)
    return a * a_sh, a * b_sh + b


def _mix_tile(z_ref, p, st, reset):
    rows = z_ref.shape[0]
    w = SC_WIDTH
    groups = rows // SUBLANES
    fresh = lambda x: jnp.where(reset, jnp.zeros_like(x), x)

    sc_b = z_ref[:, 0:w]
    cx = z_ref[:, w:2 * w] * z_ref[:, 2 * w:3 * w]
    halo = fresh(st.halo_sc[...])
    conv = p.scw[SC_KERNEL - 1:SC_KERNEL, :] * cx
    for j in range(1, SC_KERNEL):
        conv = conv + p.scw[SC_KERNEL - 1 - j:SC_KERNEL - j, :] * _shift_rows(cx, halo, j)
    st.halo_sc[...] = cx[rows - SUBLANES:, :]
    y_sc = sc_b * conv

    lx = z_ref[:, 3 * w:4 * w]
    lg = z_ref[:, 4 * w:5 * w]
    halo = fresh(st.halo_lx[...])
    xr = p.lcw[LRU_CONV - 1:LRU_CONV, :] * lx + p.lcb[...]
    for j in range(1, LRU_CONV):
        xr = xr + p.lcw[LRU_CONV - 1 - j:LRU_CONV - j, :] * _shift_rows(lx, halo, j)
    st.halo_lx[...] = lx[rows - SUBLANES:, :]

    xb = xr.astype(BF16)
    r = jax.nn.sigmoid(jnp.dot(xb, p.wa[...], preferred_element_type=F32) + p.ba[...])
    i = jax.nn.sigmoid(jnp.dot(xb, p.wx[...], preferred_element_type=F32) + p.bx[...])
    nl = -p.lam[...]
    softplus = jnp.maximum(nl, 0.0) + jnp.log1p(jnp.exp(-jnp.abs(nl)))
    log_a = (-LRU_C) * r * softplus
    a = jnp.exp(log_a)
    b = jnp.sqrt(-jnp.tanh(log_a) * (a * a + 1.0)) * (i * xr)

    pos = lax.bitwise_and(lax.broadcasted_iota(jnp.int32, (rows, w), 0), SUBLANES - 1)
    s = 1
    while s < SUBLANES:
        a, b = _scan_step(a, b, s, pos)
        s *= 2
    halves = w // LANES
    last = pl.ds(SUBLANES - 1, groups, stride=SUBLANES)
    for k in range(halves):
        st.a[k] = a[:, k * LANES:(k + 1) * LANES]
        st.b[k] = b[:, k * LANES:(k + 1) * LANES]
    at = jnp.concatenate([st.a[k, last, :] for k in range(halves)], axis=1)
    bt = jnp.concatenate([st.b[k, last, :] for k in range(halves)], axis=1)
    gpos = lax.broadcasted_iota(jnp.int32, (groups, w), 0)
    s = 1
    while s < groups:
        at, bt = _scan_step(at, bt, s, gpos)
        s *= 2
    h_in = fresh(st.h[0:1, :])
    h_end = bt + at * h_in
    st.h[0:1, :] = h_end[groups - 1:groups, :]
    st.c[...] = jnp.where(gpos >= 1, pltpu.roll(h_end, 1, 0), h_in)
    h = jnp.concatenate(
        [jnp.concatenate(
            [st.b[k, g * SUBLANES:(g + 1) * SUBLANES, :]
             + st.a[k, g * SUBLANES:(g + 1) * SUBLANES, :] * st.c[g:g + 1, k * LANES:(k + 1) * LANES]
             for g in range(groups)], axis=0) for k in range(halves)], axis=1)
    y_lru = _gelu_tanh(lg) * h
    return jnp.concatenate([y_sc, y_lru], axis=1).astype(BF16)


class _MixParams(NamedTuple):
    scw: Any
    lcw: Any
    lcb: Any
    wa: Any
    ba: Any
    wx: Any
    bx: Any
    lam: Any


class _MixState(NamedTuple):
    h: Any
    halo_sc: Any
    halo_lx: Any
    a: Any
    b: Any
    c: Any


def _out_ffn_kernel(*refs, final_norm, tiles_per_seq, jobs, n_steps):
    (h_ref, ya_ref, z0_ref, zn_ref, scw_ref, lcw_ref, lcb_ref, wa_ref, ba_ref, wx_ref, bx_ref,
     lam_ref, woa_ref, wom_ref, g2_ref, wg_ref, wu_ref, wd_ref, gf_ref), refs = refs[:19], refs[19:]
    cast_in, refs = refs[:len(jobs)], refs[len(jobs):]
    n_cast_out = sum(len(_cast_widths(j)) for j in jobs)
    o_ref, cast_out, refs = refs[0], refs[1:1 + n_cast_out], refs[1 + n_cast_out:]
    ym_ref, hs_ref, halo_sc_ref, halo_lx_ref, a_ref, b_ref, c_ref = refs
    _cast_chunks(jobs, cast_in, cast_out, n_steps)
    i = pl.program_id(0)
    p = _MixParams(scw_ref, lcw_ref, lcb_ref, wa_ref, ba_ref, wx_ref, bx_ref, lam_ref)
    st = _MixState(hs_ref, halo_sc_ref, halo_lx_ref, a_ref, b_ref, c_ref)

    @pl.when(i == 0)
    def _():
        ym_ref[0] = _mix_tile(z0_ref, p, st, i == 0)

    nxt = i + 1
    ym_cur = ym_ref[lax.rem(i, 2)]
    ym_ref[lax.rem(nxt, 2)] = _mix_tile(zn_ref, p, st, lax.rem(nxt, tiles_per_seq) == 0)

    h = (h_ref[...]
         + lax.dot_general(ya_ref[...], woa_ref[...], (((0,), (0,)), ((), ())),
                           preferred_element_type=F32)
         + jnp.dot(ym_cur, wom_ref[...], preferred_element_type=F32))
    hn = _rms(h, g2_ref[...], RMS_EPS).astype(BF16)
    x = h + 0.5 * _swiglu(hn, wg_ref, wu_ref, wd_ref)
    if final_norm:
        x = _rms(x, gf_ref[...], RMS_EPS)
    o_ref[...] = x


def _out_ffn(h2, ya_t, zr, mix_params, wo, g2, wg, wu, wd, gf, l, final_norm, cast_srcs):
    n = h2.shape[0]
    n_tiles = n // ROW_TILE
    jobs = tuple(_cast_job(src, l + 1, n_tiles, in_proj=(k == len(cast_srcs) - 1))
                 for k, src in enumerate(cast_srcs))
    whole = lambda w: _resident(w.shape, lambda i: (0, 0))
    tiles_per_seq = ya_t.shape[2] // ROW_TILE
    w = SC_WIDTH
    row = lambda width: pl.BlockSpec((ROW_TILE, width), lambda i: (i, 0))
    par = lambda rows: pl.BlockSpec((None, rows, w), lambda i: (l, 0, 0))
    mix_half = SC_WIDTH + LRU_WIDTH
    return pl.pallas_call(
        functools.partial(_out_ffn_kernel, final_norm=final_norm, tiles_per_seq=tiles_per_seq,
                          jobs=jobs, n_steps=n_tiles),
        grid=(n_tiles,),
        in_specs=[
            row(D_MODEL),
            pl.BlockSpec((None, DA_WIDTH, ROW_TILE),
                         lambda i: (i // tiles_per_seq, 0, i % tiles_per_seq)),
            pl.BlockSpec((ROW_TILE, REST_WIDTH), lambda i: (0, 0)),
            pl.BlockSpec((ROW_TILE, REST_WIDTH), lambda i: (jnp.minimum(i + 1, n_tiles - 1), 0)),
            par(SC_KERNEL), par(LRU_CONV), par(1), par(w), par(1), par(w), par(1), par(1),
            _resident((DA_WIDTH, D_MODEL), lambda i: (0, 0)),
            _resident((mix_half, D_MODEL), lambda i: (1, 0)),
            pl.BlockSpec((None, 1, D_MODEL), lambda i: (l, 0, 0)),
            whole(wg), whole(wu), whole(wd),
            pl.BlockSpec((1, D_MODEL), lambda i: (0, 0)),
        ] + [_cast_in_spec(j) for j in jobs],
        out_specs=[row(D_MODEL)] + [sp for j in jobs for sp in _cast_out_specs(j)],
        out_shape=[jax.ShapeDtypeStruct((n, D_MODEL), F32)]
        + [sh for j in jobs for sh in _cast_out_shapes(j)],
        scratch_shapes=[
            pltpu.VMEM((2, ROW_TILE, mix_half), BF16),
            pltpu.VMEM((SUBLANES, w), F32),
            pltpu.VMEM((SUBLANES, w), F32),
            pltpu.VMEM((SUBLANES, w), F32),
            pltpu.VMEM((w // LANES, ROW_TILE, LANES), F32),
            pltpu.VMEM((w // LANES, ROW_TILE, LANES), F32),
            pltpu.VMEM((ROW_TILE // SUBLANES, w), F32),
        ],
        compiler_params=pltpu.CompilerParams(
            dimension_semantics=("arbitrary",), vmem_limit_bytes=VMEM_LIMIT_BYTES),
        name="out_ffn",
    )(h2, ya_t, zr, zr, *mix_params, wo, wo, g2, wg, wu, wd, gf, *[j.src for j in jobs])


def _block_diag(w):
    depth, nb, blk, _ = w.shape
    eye = jnp.eye(nb, dtype=w.dtype)
    return jnp.einsum('lnij,nm->lnimj', w, eye).reshape(depth, nb * blk, nb * blk)


def kernel(x, rel_bias, ffn1_norm, ffn1_gate, ffn1_up, ffn1_down, mix_norm, w_in, w_out, lam_q1, lam_k1, lam_q2, lam_k2, subln_gain, sc_conv_w, lru_conv_w, lru_conv_b, lru_wa, lru_ba, lru_wx, lru_bx, lru_lambda, ffn2_norm, ffn2_gate, ffn2_up, ffn2_down, final_norm):
    b, s, d = x.shape
    depth = w_in.shape[0]
    assert d == D_MODEL and s % ATTN_TILE == 0 and s % ROW_TILE == 0

    bf = lambda w: w.astype(BF16)
    vec = lambda v: v.reshape(depth, 1, v.shape[-1])
    wg, wu, wd = bf(ffn1_gate[0]), bf(ffn1_up[0]), bf(ffn1_down[0])
    col_scale = jnp.where(jnp.arange(QKV_WIDTH) < DA_WIDTH, Q_SCALE, 1.0).astype(F32)
    wq, wr = bf(w_in[0, :, :QKV_WIDTH] * col_scale), bf(w_in[0, :, QKV_WIDTH:])
    wa, wx = bf(_block_diag(lru_wa)), bf(_block_diag(lru_wx))
    lam_params = jnp.stack([lam_q1, lam_k1, lam_q2, lam_k2], axis=1)
    g1, gm, g2 = vec(ffn1_norm), vec(mix_norm), vec(ffn2_norm)
    gs = subln_gain.reshape(depth, DA_VDIM, 1)
    lcb, lam = vec(lru_conv_b), vec(lru_lambda)
    ba, bx = vec(lru_ba.reshape(depth, -1)), vec(lru_bx.reshape(depth, -1))
    gf = final_norm.reshape(1, d)

    bias = _bias_tiles(rel_bias)

    x2 = x.reshape(b * s, d)
    for l in range(depth):
        lam_init = 0.8 - 0.6 * math.exp(-0.3 * l)
        last = l == depth - 1
        h2, zq, zr, wg2, wu2, wd2, wo = _ffn_in(x2, g1, wg, wu, wd, gm, wq, wr, l,
                                                (ffn2_gate, ffn2_up, ffn2_down, w_out))
        ya = _attention(zq.reshape(b, s, QKV_WIDTH), lam_params, gs, bias, l, lam_init)
        x2, *nxt = _out_ffn(h2, ya, zr, (sc_conv_w, lru_conv_w, lcb, wa, ba, wx, bx, lam),
                            wo, g2, wg2, wu2, wd2, gf, l, last,
                            () if last else (ffn1_gate, ffn1_up, ffn1_down, w_in))
        if not last:
            wg, wu, wd, wq, wr = nxt
    return x2.reshape(b, s, d)
```

```python
import functools
import math
from typing import Any, NamedTuple

import jax
import jax.numpy as jnp
from jax import lax
from jax.experimental import pallas as pl
from jax.experimental.pallas import tpu as pltpu

F32 = jnp.float32
BF16 = jnp.bfloat16

D_MODEL = 1024
D_FF = 2816
DA_WIDTH = 512
SC_WIDTH = 256
LRU_WIDTH = 256
DA_HEAD_DIM = 64
DA_HEADS = 4
DA_VDIM = 2 * DA_HEAD_DIM
NUM_BUCKETS = 32
MAX_DISTANCE = 128
SUBLN_EPS = 1e-5
SC_KERNEL = 3
LRU_BLOCKS = 4
LRU_BLOCK = 64
LRU_CONV = 4
LRU_C = 8.0
RMS_EPS = 1e-6
NEG_INF = -1e30
QKV_WIDTH = 3 * DA_WIDTH
REST_WIDTH = 3 * SC_WIDTH + 2 * LRU_WIDTH
IN_WIDTH = QKV_WIDTH + REST_WIDTH
LOG2_E = math.log2(math.e)
Q_SCALE = DA_HEAD_DIM ** -0.5 * LOG2_E

LANES = 128
SUBLANES = 8
BF16_SUBLANES = 16
VT_ROWS = DA_VDIM + BF16_SUBLANES
VMEM_LIMIT_BYTES = 56 * 1024 * 1024
ROW_TILE = 512
ATTN_TILE = 512
SWEEP_UNROLL = 4


def _rms(x, g, eps):
    return x * lax.rsqrt(jnp.mean(x * x, axis=-1, keepdims=True) + eps) * g


def _swiglu(xn, wg_ref, wu_ref, wd_ref):
    g = jnp.dot(xn, wg_ref[...], preferred_element_type=F32)
    u = jnp.dot(xn, wu_ref[...], preferred_element_type=F32)
    a = (g * jax.nn.sigmoid(g) * u).astype(BF16)
    return jnp.dot(a, wd_ref[...], preferred_element_type=F32)


def _resident(shape, index):
    return pl.BlockSpec(shape, index, pipeline_mode=pl.Buffered(1))


class _CastJob(NamedTuple):
    src: Any
    layer: int
    chunks: int
    in_proj: bool


def _cast_job(src, layer, n_steps, in_proj=False):
    rows = src.shape[1]
    chunks = max(c for c in range(1, n_steps + 1)
                 if rows % c == 0 and (rows // c) % BF16_SUBLANES == 0)
    return _CastJob(src, layer, chunks, in_proj)


def _cast_widths(job):
    return (QKV_WIDTH, REST_WIDTH) if job.in_proj else (job.src.shape[2],)


def _cast_in_spec(job):
    rows, last = job.src.shape[1] // job.chunks, job.chunks - 1
    return pl.BlockSpec((None, rows, job.src.shape[2]),
                        lambda i: (job.layer, jnp.minimum(i, last), 0))


def _cast_out_specs(job):
    rows, last = job.src.shape[1] // job.chunks, job.chunks - 1
    return [pl.BlockSpec((rows, w), lambda i: (jnp.minimum(i, last), 0)) for w in _cast_widths(job)]


def _cast_out_shapes(job):
    return [jax.ShapeDtypeStruct((job.src.shape[1], w), BF16) for w in _cast_widths(job)]


def _cast_chunks(jobs, in_refs, out_refs, n_steps):
    out_refs = list(out_refs)
    for job, in_ref in zip(jobs, in_refs):
        outs = [out_refs.pop(0) for _ in _cast_widths(job)]

        def body(job=job, in_ref=in_ref, outs=outs):
            c = in_ref[...]
            if job.in_proj:
                col = lax.broadcasted_iota(jnp.int32, (1, QKV_WIDTH), 1)
                scale = jnp.where(col < DA_WIDTH, Q_SCALE, 1.0)
                outs[0][...] = (c[:, :QKV_WIDTH] * scale).astype(BF16)
                outs[1][...] = c[:, QKV_WIDTH:].astype(BF16)
            else:
                outs[0][...] = c.astype(BF16)

        if job.chunks == n_steps:
            body()
        else:
            pl.when(pl.program_id(0) < job.chunks)(body)


def _ffn_in_kernel(*refs, jobs, n_steps):
    (x_ref, g1_ref, wg_ref, wu_ref, wd_ref, gm_ref, wq_ref, wr_ref), refs = refs[:8], refs[8:]
    cast_in, refs = refs[:len(jobs)], refs[len(jobs):]
    (h_ref, zq_ref, zr_ref), cast_out = refs[:3], refs[3:]
    _cast_chunks(jobs, cast_in, cast_out, n_steps)
    half = ROW_TILE // 2
    for rows in (slice(0, half), slice(half, ROW_TILE)):
        x = x_ref[rows, :]
        xn = _rms(x, g1_ref[...], RMS_EPS).astype(BF16)
        h = x + 0.5 * _swiglu(xn, wg_ref, wu_ref, wd_ref)
        h_ref[rows, :] = h
        u = _rms(h, gm_ref[...], RMS_EPS).astype(BF16)
        zq_ref[rows, :] = jnp.dot(u, wq_ref[...], preferred_element_type=F32).astype(BF16)
        zr_ref[rows, :] = jnp.dot(u, wr_ref[...], preferred_element_type=F32)


def _ffn_in(x2, g1, wg, wu, wd, gm, wq, wr, l, cast_srcs):
    n = x2.shape[0]
    n_steps = n // ROW_TILE
    jobs = tuple(_cast_job(src, l, n_steps) for src in cast_srcs)
    row = lambda w: pl.BlockSpec((ROW_TILE, w), lambda i: (i, 0))
    vec = pl.BlockSpec((None, 1, D_MODEL), lambda i: (l, 0, 0))
    whole = lambda w: _resident(w.shape, lambda i: (0, 0))
    return pl.pallas_call(
        functools.partial(_ffn_in_kernel, jobs=jobs, n_steps=n_steps),
        grid=(n_steps,),
        in_specs=[row(D_MODEL), vec, whole(wg), whole(wu), whole(wd), vec, whole(wq), whole(wr)]
        + [_cast_in_spec(j) for j in jobs],
        out_specs=[row(D_MODEL), row(QKV_WIDTH), row(REST_WIDTH)]
        + [sp for j in jobs for sp in _cast_out_specs(j)],
        out_shape=[
            jax.ShapeDtypeStruct((n, D_MODEL), F32),
            jax.ShapeDtypeStruct((n, QKV_WIDTH), BF16),
            jax.ShapeDtypeStruct((n, REST_WIDTH), F32),
        ] + [sh for j in jobs for sh in _cast_out_shapes(j)],
        compiler_params=pltpu.CompilerParams(
            dimension_semantics=("arbitrary",), vmem_limit_bytes=VMEM_LIMIT_BYTES),
        name="ffn_in",
    )(x2, g1, wg, wu, wd, gm, wq, wr, *[j.src for j in jobs])


def _bias_kernel(rb_ref, o_ref):
    h = pl.program_id(0)
    t = ATTN_TILE
    max_exact = NUM_BUCKETS // 2
    key = lax.broadcasted_iota(jnp.int32, (t, t), 0)
    qry = lax.broadcasted_iota(jnp.int32, (t, t), 1)
    for off in range(2):
        dist = qry - key + off * t
        n = jnp.maximum(dist, 0)
        nf = jnp.maximum(n, 1).astype(F32)
        large = max_exact + (jnp.log(nf / max_exact) / math.log(MAX_DISTANCE / max_exact)
                             * (NUM_BUCKETS - max_exact)).astype(jnp.int32)
        large = jnp.minimum(large, NUM_BUCKETS - 1)
        bucket = jnp.where(n < max_exact, n, large)
        for mp in range(2):
            hm = 2 * h + mp
            val = jnp.zeros((t, t), F32)
            for j in range(NUM_BUCKETS):
                val = jnp.where(bucket == j, rb_ref[j, hm], val)
            far = rb_ref[NUM_BUCKETS - 1, hm]
            o_ref[off, :, mp * t:(mp + 1) * t] = jnp.where(dist >= 0, (val - far) * LOG2_E,
                                                           NEG_INF)


def _bias_tiles(rel_bias):
    t = ATTN_TILE
    return pl.pallas_call(
        _bias_kernel,
        grid=(DA_HEADS,),
        in_specs=[pl.BlockSpec(memory_space=pltpu.SMEM)],
        out_specs=pl.BlockSpec((None, 2, t, 2 * t), lambda i: (i, 0, 0, 0)),
        out_shape=jax.ShapeDtypeStruct((DA_HEADS, 2, t, 2 * t), F32),
        compiler_params=pltpu.CompilerParams(dimension_semantics=("arbitrary",)),
        name="bias_tiles",
    )(rel_bias)


def _attn_kernel(lam_ref, g_ref, bias_ref, q_ref, k_ref, v_ref, o_ref,
                 qq_ref, vt_ref, s_ref, mx_ref, m_ref, acc_ref, *, lam_init, nq):
    t = ATTN_TILE
    chan = lax.broadcasted_iota(jnp.int32, (LANES, t), 0)
    for qi in range(nq):
        q_t = q_ref[qi * t:(qi + 1) * t, :].T
        zero = jnp.zeros_like(q_t)
        qq_ref[qi, :, 0:t] = jnp.where(chan < DA_HEAD_DIM, q_t, zero)
        qq_ref[qi, :, t:2 * t] = jnp.where(chan >= DA_HEAD_DIM, q_t, zero)

    lp = lam_ref[...]
    lam = (jnp.exp(jnp.sum(lp[0:1] * lp[1:2], axis=1, keepdims=True))
           - jnp.exp(jnp.sum(lp[2:3] * lp[3:4], axis=1, keepdims=True)) + lam_init)
    gain = g_ref[...] * (1.0 - lam_init)

    ones_row = lax.broadcasted_iota(jnp.int32, (VT_ROWS - DA_VDIM, t), 0) == 0
    for kj in range(nq):
        vt_ref[kj, 0:DA_VDIM, :] = v_ref[kj * t:(kj + 1) * t, :].T
        vt_ref[kj, DA_VDIM:VT_ROWS, :] = jnp.where(ones_row, 1.0, 0.0).astype(BF16)

    m_ref[...] = jnp.full(m_ref.shape, NEG_INF, F32)
    acc_ref[...] = jnp.zeros(acc_ref.shape, F32)

    def key_rows(kj):
        return pl.ds(kj * t, t) if isinstance(kj, int) else pl.ds(pl.multiple_of(kj * t, t), t)

    def scores(qi, kj, off, buf):
        s = jnp.dot(k_ref[key_rows(kj), :], qq_ref[qi], preferred_element_type=F32)
        if off is not None:
            s = s + bias_ref[off]
        s_ref[buf] = s
        mx_ref[buf] = jnp.max(s, axis=0, keepdims=True)

    def consume(qi, kj, off, buf):
        m_old = m_ref[qi]
        m_new = jnp.maximum(m_old, mx_ref[buf])
        alpha = jnp.exp2(m_old - m_new)
        p = jnp.exp2(s_ref[buf] - m_new)
        pv = jnp.dot(vt_ref[kj], p.astype(BF16), preferred_element_type=F32)
        acc_ref[qi] = alpha * acc_ref[qi] + pv
        m_ref[qi] = m_new

    def sweep(segments):
        segments = [seg for seg in segments if seg[0] > 0]
        if not segments:
            return
        scores(*segments[0][1](0), 0)
        done = 0
        for k, (n_steps, coords) in enumerate(segments):

            def group(i, carry, done=done, coords=coords):
                for u in range(SWEEP_UNROLL):
                    n = SWEEP_UNROLL * i + u
                    scores(*coords(n + 1), (done + u + 1) % 2)
                    consume(*coords(n), (done + u) % 2)
                return carry

            n_groups = (n_steps - 1) // SWEEP_UNROLL
            lax.fori_loop(0, n_groups, group, 0)
            for n in range(n_groups * SWEEP_UNROLL, n_steps):
                if n + 1 < n_steps:
                    scores(*coords(n + 1), (done + n + 1) % 2)
                elif k + 1 < len(segments):
                    scores(*segments[k + 1][1](0), (done + n + 1) % 2)
                consume(*coords(n), (done + n) % 2)
            done += n_steps

    def biased_coords(n):
        if isinstance(n, int):
            qi, off = (n + 1) // 2, n % 2
        else:
            qi, off = lax.shift_right_logical(n + 1, 1), lax.bitwise_and(n, 1)
        return qi, qi - off, off

    def far_coords(n):
        first = lambda qi: (qi - 1) * (qi - 2) // 2
        if isinstance(n, int):
            qi = max(c for c in range(2, nq) if first(c) <= n)
            return qi, n - first(qi), None
        qi = 2
        for c in range(3, nq):
            qi = qi + (n >= first(c)).astype(jnp.int32)
        return qi, n - lax.shift_right_logical((qi - 1) * (qi - 2), 1), None

    sweep([(2 * nq - 1, biased_coords), ((nq - 1) * (nq - 2) // 2, far_coords)])

    for qi in range(nq):
        on = acc_ref[qi, 0:DA_VDIM, :] * (1.0 / acc_ref[qi, DA_VDIM:DA_VDIM + 1, :])
        o = on[:, 0:t] - lam * on[:, t:2 * t]
        y = o * lax.rsqrt(jnp.mean(o * o, axis=0, keepdims=True) + SUBLN_EPS) * gain
        o_ref[:, qi * t:(qi + 1) * t] = y.astype(o_ref.dtype)


def _attention(zq3, lam_params, subln_g, bias, l, lam_init):
    b, s, _ = zq3.shape
    t = ATTN_TILE
    nq = s // t
    return pl.pallas_call(
        functools.partial(_attn_kernel, lam_init=lam_init, nq=nq),
        grid=(b, DA_HEADS),
        in_specs=[
            pl.BlockSpec((None, 4, DA_HEAD_DIM), lambda bi, h: (l, 0, 0)),
            pl.BlockSpec((None, DA_VDIM, 1), lambda bi, h: (l, 0, 0)),
            pl.BlockSpec((None, 2, t, 2 * t), lambda bi, h: (h, 0, 0, 0)),
            pl.BlockSpec((None, s, LANES), lambda bi, h: (bi, 0, h)),
            pl.BlockSpec((None, s, LANES), lambda bi, h: (bi, 0, DA_HEADS + h)),
            pl.BlockSpec((None, s, LANES), lambda bi, h: (bi, 0, 2 * DA_HEADS + h)),
        ],
        out_specs=pl.BlockSpec((None, DA_VDIM, s), lambda bi, h: (bi, h, 0)),
        out_shape=jax.ShapeDtypeStruct((b, DA_WIDTH, s), BF16),
        scratch_shapes=[
            pltpu.VMEM((nq, LANES, 2 * t), BF16),
            pltpu.VMEM((nq, VT_ROWS, t), BF16),
            pltpu.VMEM((2, t, 2 * t), F32),
            pltpu.VMEM((2, 1, 2 * t), F32),
            pltpu.VMEM((nq, 1, 2 * t), F32),
            pltpu.VMEM((nq, VT_ROWS, 2 * t), F32),
        ],
        compiler_params=pltpu.CompilerParams(
            dimension_semantics=("arbitrary", "arbitrary"),
            vmem_limit_bytes=VMEM_LIMIT_BYTES),
        name="diff_attn",
    )(lam_params, subln_g, bias, zq3, zq3, zq3)


def _shift_rows(x, halo, j):
    r = pltpu.roll(x, j, 0)
    head_row = lax.broadcasted_iota(jnp.int32, halo.shape, 0)
    head = jnp.where(head_row < j, pltpu.roll(halo, j, 0), r[:SUBLANES])
    return jnp.concatenate([head, r[SUBLANES:]], axis=0)


def _gelu_tanh(x):
    return 0.5 * x * (1.0 + jnp.tanh(math.sqrt(2.0 / math.pi) * (x + 0.044715 * (x * x * x))))


def _scan_step(a, b, s, pos):
    keep = pos >= s
    a_sh = jnp.where(keep, pltpu.roll(a, s, 0), 1.0)
    b_sh = jnp.where(keep, pltpu.roll(b, s, 0), 0.0)
    return a * a_sh, a * b_sh + b


def _mix_tile(z_ref, p, st, reset):
    rows = z_ref.shape[0]
    w = SC_WIDTH
    groups = rows // SUBLANES
    fresh = lambda x: jnp.where(reset, jnp.zeros_like(x), x)

    sc_b = z_ref[:, 0:w]
    cx = z_ref[:, w:2 * w] * z_ref[:, 2 * w:3 * w]
    halo = fresh(st.halo_sc[...])
    conv = p.scw[SC_KERNEL - 1:SC_KERNEL, :] * cx
    for j in range(1, SC_KERNEL):
        conv = conv + p.scw[SC_KERNEL - 1 - j:SC_KERNEL - j, :] * _shift_rows(cx, halo, j)
    st.halo_sc[...] = cx[rows - SUBLANES:, :]
    y_sc = sc_b * conv

    lx = z_ref[:, 3 * w:4 * w]
    lg = z_ref[:, 4 * w:5 * w]
    halo = fresh(st.halo_lx[...])
    xr = p.lcw[LRU_CONV - 1:LRU_CONV, :] * lx + p.lcb[...]
    for j in range(1, LRU_CONV):
        xr = xr + p.lcw[LRU_CONV - 1 - j:LRU_CONV - j, :] * _shift_rows(lx, halo, j)
    st.halo_lx[...] = lx[rows - SUBLANES:, :]

    xb = xr.astype(BF16)
    r = jax.nn.sigmoid(jnp.dot(xb, p.wa[...], preferred_element_type=F32) + p.ba[...])
    i = jax.nn.sigmoid(jnp.dot(xb, p.wx[...], preferred_element_type=F32) + p.bx[...])
    nl = -p.lam[...]
    softplus = jnp.maximum(nl, 0.0) + jnp.log1p(jnp.exp(-jnp.abs(nl)))
    log_a = (-LRU_C) * r * softplus
    a = jnp.exp(log_a)
    b = jnp.sqrt(-jnp.tanh(log_a) * (a * a + 1.0)) * (i * xr)

    pos = lax.bitwise_and(lax.broadcasted_iota(jnp.int32, (rows, w), 0), SUBLANES - 1)
    s = 1
    while s < SUBLANES:
        a, b = _scan_step(a, b, s, pos)
        s *= 2
    halves = w // LANES
    last = pl.ds(SUBLANES - 1, groups, stride=SUBLANES)
    for k in range(halves):
        st.a[k] = a[:, k * LANES:(k + 1) * LANES]
        st.b[k] = b[:, k * LANES:(k + 1) * LANES]
    at = jnp.concatenate([st.a[k, last, :] for k in range(halves)], axis=1)
    bt = jnp.concatenate([st.b[k, last, :] for k in range(halves)], axis=1)
    gpos = lax.broadcasted_iota(jnp.int32, (groups, w), 0)
    s = 1
    while s < groups:
        at, bt = _scan_step(at, bt, s, gpos)
        s *= 2
    h_in = fresh(st.h[0:1, :])
    h_end = bt + at * h_in
    st.h[0:1, :] = h_end[groups - 1:groups, :]
    st.c[...] = jnp.where(gpos >= 1, pltpu.roll(h_end, 1, 0), h_in)
    h = jnp.concatenate(
        [jnp.concatenate(
            [st.b[k, g * SUBLANES:(g + 1) * SUBLANES, :]
             + st.a[k, g * SUBLANES:(g + 1) * SUBLANES, :] * st.c[g:g + 1, k * LANES:(k + 1) * LANES]
             for g in range(groups)], axis=0) for k in range(halves)], axis=1)
    y_lru = _gelu_tanh(lg) * h
    return jnp.concatenate([y_sc, y_lru], axis=1).astype(BF16)


class _MixParams(NamedTuple):
    scw: Any
    lcw: Any
    lcb: Any
    wa: Any
    ba: Any
    wx: Any
    bx: Any
    lam: Any


class _MixState(NamedTuple):
    h: Any
    halo_sc: Any
    halo_lx: Any
    a: Any
    b: Any
    c: Any


def _out_ffn_kernel(*refs, final_norm, tiles_per_seq, jobs, n_steps):
    (h_ref, ya_ref, z0_ref, zn_ref, scw_ref, lcw_ref, lcb_ref, wa_ref, ba_ref, wx_ref, bx_ref,
     lam_ref, woa_ref, wom_ref, g2_ref, wg_ref, wu_ref, wd_ref, gf_ref), refs = refs[:19], refs[19:]
    cast_in, refs = refs[:len(jobs)], refs[len(jobs):]
    n_cast_out = sum(len(_cast_widths(j)) for j in jobs)
    o_ref, cast_out, refs = refs[0], refs[1:1 + n_cast_out], refs[1 + n_cast_out:]
    ym_ref, hs_ref, halo_sc_ref, halo_lx_ref, a_ref, b_ref, c_ref = refs
    _cast_chunks(jobs, cast_in, cast_out, n_steps)
    i = pl.program_id(0)
    p = _MixParams(scw_ref, lcw_ref, lcb_ref, wa_ref, ba_ref, wx_ref, bx_ref, lam_ref)
    st = _MixState(hs_ref, halo_sc_ref, halo_lx_ref, a_ref, b_ref, c_ref)

    @pl.when(i == 0)
    def _():
        ym_ref[0] = _mix_tile(z0_ref, p, st, i == 0)

    nxt = i + 1
    ym_cur = ym_ref[lax.rem(i, 2)]
    ym_ref[lax.rem(nxt, 2)] = _mix_tile(zn_ref, p, st, lax.rem(nxt, tiles_per_seq) == 0)

    h = (h_ref[...]
         + lax.dot_general(ya_ref[...], woa_ref[...], (((0,), (0,)), ((), ())),
                           preferred_element_type=F32)
         + jnp.dot(ym_cur, wom_ref[...], preferred_element_type=F32))
    hn = _rms(h, g2_ref[...], RMS_EPS).astype(BF16)
    x = h + 0.5 * _swiglu(hn, wg_ref, wu_ref, wd_ref)
    if final_norm:
        x = _rms(x, gf_ref[...], RMS_EPS)
    o_ref[...] = x


def _out_ffn(h2, ya_t, zr, mix_params, wo, g2, wg, wu, wd, gf, l, final_norm, cast_srcs):
    n = h2.shape[0]
    n_tiles = n // ROW_TILE
    jobs = tuple(_cast_job(src, l + 1, n_tiles, in_proj=(k == len(cast_srcs) - 1))
                 for k, src in enumerate(cast_srcs))
    whole = lambda w: _resident(w.shape, lambda i: (0, 0))
    tiles_per_seq = ya_t.shape[2] // ROW_TILE
    w = SC_WIDTH
    row = lambda width: pl.BlockSpec((ROW_TILE, width), lambda i: (i, 0))
    par = lambda rows: pl.BlockSpec((None, rows, w), lambda i: (l, 0, 0))
    mix_half = SC_WIDTH + LRU_WIDTH
    return pl.pallas_call(
        functools.partial(_out_ffn_kernel, final_norm=final_norm, tiles_per_seq=tiles_per_seq,
                          jobs=jobs, n_steps=n_tiles),
        grid=(n_tiles,),
        in_specs=[
            row(D_MODEL),
            pl.BlockSpec((None, DA_WIDTH, ROW_TILE),
                         lambda i: (i // tiles_per_seq, 0, i % tiles_per_seq)),
            pl.BlockSpec((ROW_TILE, REST_WIDTH), lambda i: (0, 0)),
            pl.BlockSpec((ROW_TILE, REST_WIDTH), lambda i: (jnp.minimum(i + 1, n_tiles - 1), 0)),
            par(SC_KERNEL), par(LRU_CONV), par(1), par(w), par(1), par(w), par(1), par(1),
            _resident((DA_WIDTH, D_MODEL), lambda i: (0, 0)),
            _resident((mix_half, D_MODEL), lambda i: (1, 0)),
            pl.BlockSpec((None, 1, D_MODEL), lambda i: (l, 0, 0)),
            whole(wg), whole(wu), whole(wd),
            pl.BlockSpec((1, D_MODEL), lambda i: (0, 0)),
        ] + [_cast_in_spec(j) for j in jobs],
        out_specs=[row(D_MODEL)] + [sp for j in jobs for sp in _cast_out_specs(j)],
        out_shape=[jax.ShapeDtypeStruct((n, D_MODEL), F32)]
        + [sh for j in jobs for sh in _cast_out_shapes(j)],
        scratch_shapes=[
            pltpu.VMEM((2, ROW_TILE, mix_half), BF16),
            pltpu.VMEM((SUBLANES, w), F32),
            pltpu.VMEM((SUBLANES, w), F32),
            pltpu.VMEM((SUBLANES, w), F32),
            pltpu.VMEM((w // LANES, ROW_TILE, LANES), F32),
            pltpu.VMEM((w // LANES, ROW_TILE, LANES), F32),
            pltpu.VMEM((ROW_TILE // SUBLANES, w), F32),
        ],
        compiler_params=pltpu.CompilerParams(
            dimension_semantics=("arbitrary",), vmem_limit_bytes=VMEM_LIMIT_BYTES),
        name="out_ffn",
    )(h2, ya_t, zr, zr, *mix_params, wo, wo, g2, wg, wu, wd, gf, *[j.src for j in jobs])


def _block_diag(w):
    depth, nb, blk, _ = w.shape
    eye = jnp.eye(nb, dtype=w.dtype)
    return jnp.einsum('lnij,nm->lnimj', w, eye).reshape(depth, nb * blk, nb * blk)


def kernel(x, rel_bias, ffn1_norm, ffn1_gate, ffn1_up, ffn1_down, mix_norm, w_in, w_out, lam_q1, lam_k1, lam_q2, lam_k2, subln_gain, sc_conv_w, lru_conv_w, lru_conv_b, lru_wa, lru_ba, lru_wx, lru_bx, lru_lambda, ffn2_norm, ffn2_gate, ffn2_up, ffn2_down, final_norm):
    b, s, d = x.shape
    depth = w_in.shape[0]
    assert d == D_MODEL and s % ATTN_TILE == 0 and s % ROW_TILE == 0

    bf = lambda w: w.astype(BF16)
    vec = lambda v: v.reshape(depth, 1, v.shape[-1])
    wg, wu, wd = bf(ffn1_gate[0]), bf(ffn1_up[0]), bf(ffn1_down[0])
    col_scale = jnp.where(jnp.arange(QKV_WIDTH) < DA_WIDTH, Q_SCALE, 1.0).astype(F32)
    wq, wr = bf(w_in[0, :, :QKV_WIDTH] * col_scale), bf(w_in[0, :, QKV_WIDTH:])
    wa, wx = bf(_block_diag(lru_wa)), bf(_block_diag(lru_wx))
    lam_params = jnp.stack([lam_q1, lam_k1, lam_q2, lam_k2], axis=1)
    g1, gm, g2 = vec(ffn1_norm), vec(mix_norm), vec(ffn2_norm)
    gs = subln_gain.reshape(depth, DA_VDIM, 1)
    lcb, lam = vec(lru_conv_b), vec(lru_lambda)
    ba, bx = vec(lru_ba.reshape(depth, -1)), vec(lru_bx.reshape(depth, -1))
    gf = final_norm.reshape(1, d)

    bias = _bias_tiles(rel_bias)

    x2 = x.reshape(b * s, d)
    for l in range(depth):
        lam_init = 0.8 - 0.6 * math.exp(-0.3 * l)
        last = l == depth - 1
        h2, zq, zr, wg2, wu2, wd2, wo = _ffn_in(x2, g1, wg, wu, wd, gm, wq, wr, l,
                                                (ffn2_gate, ffn2_up, ffn2_down, w_out))
        ya = _attention(zq.reshape(b, s, QKV_WIDTH), lam_params, gs, bias, l, lam_init)
        x2, *nxt = _out_ffn(h2, ya, zr, (sc_conv_w, lru_conv_w, lcb, wa, ba, wx, bx, lam),
                            wo, g2, wg2, wu2, wd2, gf, l, last,
                            () if last else (ffn1_gate, ffn1_up, ffn1_down, w_in))
        if not last:
            wg, wu, wd, wq, wr = nxt
    return x2.reshape(b, s, d)
```

```python
import functools
import math
from typing import Any, NamedTuple

import jax
import jax.numpy as jnp
from jax import lax
from jax.experimental import pallas as pl
from jax.experimental.pallas import tpu as pltpu

F32 = jnp.float32
BF16 = jnp.bfloat16

D_MODEL = 1024
D_FF = 2816
DA_WIDTH = 512
SC_WIDTH = 256
LRU_WIDTH = 256
DA_HEAD_DIM = 64
DA_HEADS = 4
DA_VDIM = 2 * DA_HEAD_DIM
NUM_BUCKETS = 32
MAX_DISTANCE = 128
SUBLN_EPS = 1e-5
SC_KERNEL = 3
LRU_BLOCKS = 4
LRU_BLOCK = 64
LRU_CONV = 4
LRU_C = 8.0
RMS_EPS = 1e-6
NEG_INF = -1e30
QKV_WIDTH = 3 * DA_WIDTH
REST_WIDTH = 3 * SC_WIDTH + 2 * LRU_WIDTH
IN_WIDTH = QKV_WIDTH + REST_WIDTH
LOG2_E = math.log2(math.e)
Q_SCALE = DA_HEAD_DIM ** -0.5 * LOG2_E

LANES = 128
SUBLANES = 8
BF16_SUBLANES = 16
VT_ROWS = DA_VDIM + BF16_SUBLANES
VMEM_LIMIT_BYTES = 56 * 1024 * 1024
ROW_TILE = 512
ATTN_TILE = 512
SWEEP_UNROLL = 8


def _rms(x, g, eps):
    return x * lax.rsqrt(jnp.mean(x * x, axis=-1, keepdims=True) + eps) * g


def _swiglu(xn, wg_ref, wu_ref, wd_ref):
    g = jnp.dot(xn, wg_ref[...], preferred_element_type=F32)
    u = jnp.dot(xn, wu_ref[...], preferred_element_type=F32)
    a = (g * jax.nn.sigmoid(g) * u).astype(BF16)
    return jnp.dot(a, wd_ref[...], preferred_element_type=F32)


def _resident(shape, index):
    return pl.BlockSpec(shape, index, pipeline_mode=pl.Buffered(1))


class _CastJob(NamedTuple):
    src: Any
    layer: int
    chunks: int
    in_proj: bool


def _cast_job(src, layer, n_steps, in_proj=False):
    rows = src.shape[1]
    chunks = max(c for c in range(1, n_steps + 1)
                 if rows % c == 0 and (rows // c) % BF16_SUBLANES == 0)
    return _CastJob(src, layer, chunks, in_proj)


def _cast_widths(job):
    return (QKV_WIDTH, REST_WIDTH) if job.in_proj else (job.src.shape[2],)


def _cast_in_spec(job):
    rows, last = job.src.shape[1] // job.chunks, job.chunks - 1
    return pl.BlockSpec((None, rows, job.src.shape[2]),
                        lambda i: (job.layer, jnp.minimum(i, last), 0))


def _cast_out_specs(job):
    rows, last = job.src.shape[1] // job.chunks, job.chunks - 1
    return [pl.BlockSpec((rows, w), lambda i: (jnp.minimum(i, last), 0)) for w in _cast_widths(job)]


def _cast_out_shapes(job):
    return [jax.ShapeDtypeStruct((job.src.shape[1], w), BF16) for w in _cast_widths(job)]


def _cast_chunks(jobs, in_refs, out_refs, n_steps):
    out_refs = list(out_refs)
    for job, in_ref in zip(jobs, in_refs):
        outs = [out_refs.pop(0) for _ in _cast_widths(job)]

        def body(job=job, in_ref=in_ref, outs=outs):
            c = in_ref[...]
            if job.in_proj:
                col = lax.broadcasted_iota(jnp.int32, (1, QKV_WIDTH), 1)
                scale = jnp.where(col < DA_WIDTH, Q_SCALE, 1.0)
                outs[0][...] = (c[:, :QKV_WIDTH] * scale).astype(BF16)
                outs[1][...] = c[:, QKV_WIDTH:].astype(BF16)
            else:
                outs[0][...] = c.astype(BF16)

        if job.chunks == n_steps:
            body()
        else:
            pl.when(pl.program_id(0) < job.chunks)(body)


def _ffn_in_kernel(*refs, jobs, n_steps):
    (x_ref, g1_ref, wg_ref, wu_ref, wd_ref, gm_ref, wq_ref, wr_ref), refs = refs[:8], refs[8:]
    cast_in, refs = refs[:len(jobs)], refs[len(jobs):]
    (h_ref, zq_ref, zr_ref), cast_out = refs[:3], refs[3:]
    _cast_chunks(jobs, cast_in, cast_out, n_steps)
    half = ROW_TILE // 2
    for rows in (slice(0, half), slice(half, ROW_TILE)):
        x = x_ref[rows, :]
        xn = _rms(x, g1_ref[...], RMS_EPS).astype(BF16)
        h = x + 0.5 * _swiglu(xn, wg_ref, wu_ref, wd_ref)
        h_ref[rows, :] = h
        u = _rms(h, gm_ref[...], RMS_EPS).astype(BF16)
        zq_ref[rows, :] = jnp.dot(u, wq_ref[...], preferred_element_type=F32).astype(BF16)
        zr_ref[rows, :] = jnp.dot(u, wr_ref[...], preferred_element_type=F32)


def _ffn_in(x2, g1, wg, wu, wd, gm, wq, wr, l, cast_srcs):
    n = x2.shape[0]
    n_steps = n // ROW_TILE
    jobs = tuple(_cast_job(src, l, n_steps) for src in cast_srcs)
    row = lambda w: pl.BlockSpec((ROW_TILE, w), lambda i: (i, 0))
    vec = pl.BlockSpec((None, 1, D_MODEL), lambda i: (l, 0, 0))
    whole = lambda w: _resident(w.shape, lambda i: (0, 0))
    return pl.pallas_call(
        functools.partial(_ffn_in_kernel, jobs=jobs, n_steps=n_steps),
        grid=(n_steps,),
        in_specs=[row(D_MODEL), vec, whole(wg), whole(wu), whole(wd), vec, whole(wq), whole(wr)]
        + [_cast_in_spec(j) for j in jobs],
        out_specs=[row(D_MODEL), row(QKV_WIDTH), row(REST_WIDTH)]
        + [sp for j in jobs for sp in _cast_out_specs(j)],
        out_shape=[
            jax.ShapeDtypeStruct((n, D_MODEL), F32),
            jax.ShapeDtypeStruct((n, QKV_WIDTH), BF16),
            jax.ShapeDtypeStruct((n, REST_WIDTH), F32),
        ] + [sh for j in jobs for sh in _cast_out_shapes(j)],
        compiler_params=pltpu.CompilerParams(
            dimension_semantics=("arbitrary",), vmem_limit_bytes=VMEM_LIMIT_BYTES),
        name="ffn_in",
    )(x2, g1, wg, wu, wd, gm, wq, wr, *[j.src for j in jobs])


def _bias_kernel(rb_ref, o_ref):
    h = pl.program_id(0)
    t = ATTN_TILE
    max_exact = NUM_BUCKETS // 2
    key = lax.broadcasted_iota(jnp.int32, (t, t), 0)
    qry = lax.broadcasted_iota(jnp.int32, (t, t), 1)
    for off in range(2):
        dist = qry - key + off * t
        n = jnp.maximum(dist, 0)
        nf = jnp.maximum(n, 1).astype(F32)
        large = max_exact + (jnp.log(nf / max_exact) / math.log(MAX_DISTANCE / max_exact)
                             * (NUM_BUCKETS - max_exact)).astype(jnp.int32)
        large = jnp.minimum(large, NUM_BUCKETS - 1)
        bucket = jnp.where(n < max_exact, n, large)
        for mp in range(2):
            hm = 2 * h + mp
            val = jnp.zeros((t, t), F32)
            for j in range(NUM_BUCKETS):
                val = jnp.where(bucket == j, rb_ref[j, hm], val)
            far = rb_ref[NUM_BUCKETS - 1, hm]
            o_ref[off, :, mp * t:(mp + 1) * t] = jnp.where(dist >= 0, (val - far) * LOG2_E,
                                                           NEG_INF)


def _bias_tiles(rel_bias):
    t = ATTN_TILE
    return pl.pallas_call(
        _bias_kernel,
        grid=(DA_HEADS,),
        in_specs=[pl.BlockSpec(memory_space=pltpu.SMEM)],
        out_specs=pl.BlockSpec((None, 2, t, 2 * t), lambda i: (i, 0, 0, 0)),
        out_shape=jax.ShapeDtypeStruct((DA_HEADS, 2, t, 2 * t), F32),
        compiler_params=pltpu.CompilerParams(dimension_semantics=("arbitrary",)),
        name="bias_tiles",
    )(rel_bias)


def _attn_kernel(lam_ref, g_ref, bias_ref, q_ref, k_ref, v_ref, o_ref,
                 qq_ref, vt_ref, s_ref, mx_ref, m_ref, acc_ref, *, lam_init, nq):
    t = ATTN_TILE
    chan = lax.broadcasted_iota(jnp.int32, (LANES, t), 0)
    for qi in range(nq):
        q_t = q_ref[qi * t:(qi + 1) * t, :].T
        zero = jnp.zeros_like(q_t)
        qq_ref[qi, :, 0:t] = jnp.where(chan < DA_HEAD_DIM, q_t, zero)
        qq_ref[qi, :, t:2 * t] = jnp.where(chan >= DA_HEAD_DIM, q_t, zero)

    lp = lam_ref[...]
    lam = (jnp.exp(jnp.sum(lp[0:1] * lp[1:2], axis=1, keepdims=True))
           - jnp.exp(jnp.sum(lp[2:3] * lp[3:4], axis=1, keepdims=True)) + lam_init)
    gain = g_ref[...] * (1.0 - lam_init)

    ones_row = lax.broadcasted_iota(jnp.int32, (VT_ROWS - DA_VDIM, t), 0) == 0
    for kj in range(nq):
        vt_ref[kj, 0:DA_VDIM, :] = v_ref[kj * t:(kj + 1) * t, :].T
        vt_ref[kj, DA_VDIM:VT_ROWS, :] = jnp.where(ones_row, 1.0, 0.0).astype(BF16)

    m_ref[...] = jnp.full(m_ref.shape, NEG_INF, F32)
    acc_ref[...] = jnp.zeros(acc_ref.shape, F32)

    def key_rows(kj):
        return pl.ds(kj * t, t) if isinstance(kj, int) else pl.ds(pl.multiple_of(kj * t, t), t)

    def scores(qi, kj, off, buf):
        s = jnp.dot(k_ref[key_rows(kj), :], qq_ref[qi], preferred_element_type=F32)
        if off is not None:
            s = s + bias_ref[off]
        s_ref[buf] = s
        mx_ref[buf] = jnp.max(s, axis=0, keepdims=True)

    def consume(qi, kj, off, buf):
        m_old = m_ref[qi]
        m_new = jnp.maximum(m_old, mx_ref[buf])
        alpha = jnp.exp2(m_old - m_new)
        p = jnp.exp2(s_ref[buf] - m_new)
        pv = jnp.dot(vt_ref[kj], p.astype(BF16), preferred_element_type=F32)
        acc_ref[qi] = alpha * acc_ref[qi] + pv
        m_ref[qi] = m_new

    def sweep(segments):
        segments = [seg for seg in segments if seg[0] > 0]
        if not segments:
            return
        scores(*segments[0][1](0), 0)
        done = 0
        for k, (n_steps, coords) in enumerate(segments):

            def group(i, carry, done=done, coords=coords):
                for u in range(SWEEP_UNROLL):
                    n = SWEEP_UNROLL * i + u
                    scores(*coords(n + 1), (done + u + 1) % 2)
                    consume(*coords(n), (done + u) % 2)
                return carry

            n_groups = (n_steps - 1) // SWEEP_UNROLL
            lax.fori_loop(0, n_groups, group, 0)
            for n in range(n_groups * SWEEP_UNROLL, n_steps):
                if n + 1 < n_steps:
                    scores(*coords(n + 1), (done + n + 1) % 2)
                elif k + 1 < len(segments):
                    scores(*segments[k + 1][1](0), (done + n + 1) % 2)
                consume(*coords(n), (done + n) % 2)
            done += n_steps

    def biased_coords(n):
        if isinstance(n, int):
            qi, off = (n + 1) // 2, n % 2
        else:
            qi, off = lax.shift_right_logical(n + 1, 1), lax.bitwise_and(n, 1)
        return qi, qi - off, off

    def far_coords(n):
        first = lambda qi: (qi - 1) * (qi - 2) // 2
        if isinstance(n, int):
            qi = max(c for c in range(2, nq) if first(c) <= n)
            return qi, n - first(qi), None
        qi = 2
        for c in range(3, nq):
            qi = qi + (n >= first(c)).astype(jnp.int32)
        return qi, n - lax.shift_right_logical((qi - 1) * (qi - 2), 1), None

    sweep([(2 * nq - 1, biased_coords), ((nq - 1) * (nq - 2) // 2, far_coords)])

    for qi in range(nq):
        on = acc_ref[qi, 0:DA_VDIM, :] * (1.0 / acc_ref[qi, DA_VDIM:DA_VDIM + 1, :])
        o = on[:, 0:t] - lam * on[:, t:2 * t]
        y = o * lax.rsqrt(jnp.mean(o * o, axis=0, keepdims=True) + SUBLN_EPS) * gain
        o_ref[:, qi * t:(qi + 1) * t] = y.astype(o_ref.dtype)


def _attention(zq3, lam_params, subln_g, bias, l, lam_init):
    b, s, _ = zq3.shape
    t = ATTN_TILE
    nq = s // t
    return pl.pallas_call(
        functools.partial(_attn_kernel, lam_init=lam_init, nq=nq),
        grid=(b, DA_HEADS),
        in_specs=[
            pl.BlockSpec((None, 4, DA_HEAD_DIM), lambda bi, h: (l, 0, 0)),
            pl.BlockSpec((None, DA_VDIM, 1), lambda bi, h: (l, 0, 0)),
            pl.BlockSpec((None, 2, t, 2 * t), lambda bi, h: (h, 0, 0, 0)),
            pl.BlockSpec((None, s, LANES), lambda bi, h: (bi, 0, h)),
            pl.BlockSpec((None, s, LANES), lambda bi, h: (bi, 0, DA_HEADS + h)),
            pl.BlockSpec((None, s, LANES), lambda bi, h: (bi, 0, 2 * DA_HEADS + h)),
        ],
        out_specs=pl.BlockSpec((None, DA_VDIM, s), lambda bi, h: (bi, h, 0)),
        out_shape=jax.ShapeDtypeStruct((b, DA_WIDTH, s), BF16),
        scratch_shapes=[
            pltpu.VMEM((nq, LANES, 2 * t), BF16),
            pltpu.VMEM((nq, VT_ROWS, t), BF16),
            pltpu.VMEM((2, t, 2 * t), F32),
            pltpu.VMEM((2, 1, 2 * t), F32),
            pltpu.VMEM((nq, 1, 2 * t), F32),
            pltpu.VMEM((nq, VT_ROWS, 2 * t), F32),
        ],
        compiler_params=pltpu.CompilerParams(
            dimension_semantics=("arbitrary", "arbitrary"),
            vmem_limit_bytes=VMEM_LIMIT_BYTES),
        name="diff_attn",
    )(lam_params, subln_g, bias, zq3, zq3, zq3)


def _shift_rows(x, halo, j):
    r = pltpu.roll(x, j, 0)
    head_row = lax.broadcasted_iota(jnp.int32, halo.shape, 0)
    head = jnp.where(head_row < j, pltpu.roll(halo, j, 0), r[:SUBLANES])
    return jnp.concatenate([head, r[SUBLANES:]], axis=0)


def _gelu_tanh(x):
    return 0.5 * x * (1.0 + jnp.tanh(math.sqrt(2.0 / math.pi) * (x + 0.044715 * (x * x * x))))


def _scan_step(a, b, s, pos):
    keep = pos >= s
    a_sh = jnp.where(keep, pltpu.roll(a, s, 0), 1.0)
    b_sh = jnp.where(keep, pltpu.roll(b, s, 0), 0.0)
    return a * a_sh, a * b_sh + b


def _mix_tile(z_ref, p, st, reset):
    rows = z_ref.shape[0]
    w = SC_WIDTH
    groups = rows // SUBLANES
    fresh = lambda x: jnp.where(reset, jnp.zeros_like(x), x)

    sc_b = z_ref[:, 0:w]
    cx = z_ref[:, w:2 * w] * z_ref[:, 2 * w:3 * w]
    halo = fresh(st.halo_sc[...])
    conv = p.scw[SC_KERNEL - 1:SC_KERNEL, :] * cx
    for j in range(1, SC_KERNEL):
        conv = conv + p.scw[SC_KERNEL - 1 - j:SC_KERNEL - j, :] * _shift_rows(cx, halo, j)
    st.halo_sc[...] = cx[rows - SUBLANES:, :]
    y_sc = sc_b * conv

    lx = z_ref[:, 3 * w:4 * w]
    lg = z_ref[:, 4 * w:5 * w]
    halo = fresh(st.halo_lx[...])
    xr = p.lcw[LRU_CONV - 1:LRU_CONV, :] * lx + p.lcb[...]
    for j in range(1, LRU_CONV):
        xr = xr + p.lcw[LRU_CONV - 1 - j:LRU_CONV - j, :] * _shift_rows(lx, halo, j)
    st.halo_lx[...] = lx[rows - SUBLANES:, :]

    xb = xr.astype(BF16)
    r = jax.nn.sigmoid(jnp.dot(xb, p.wa[...], preferred_element_type=F32) + p.ba[...])
    i = jax.nn.sigmoid(jnp.dot(xb, p.wx[...], preferred_element_type=F32) + p.bx[...])
    nl = -p.lam[...]
    softplus = jnp.maximum(nl, 0.0) + jnp.log1p(jnp.exp(-jnp.abs(nl)))
    log_a = (-LRU_C) * r * softplus
    a = jnp.exp(log_a)
    b = jnp.sqrt(-jnp.tanh(log_a) * (a * a + 1.0)) * (i * xr)

    pos = lax.bitwise_and(lax.broadcasted_iota(jnp.int32, (rows, w), 0), SUBLANES - 1)
    s = 1
    while s < SUBLANES:
        a, b = _scan_step(a, b, s, pos)
        s *= 2
    halves = w // LANES
    last = pl.ds(SUBLANES - 1, groups, stride=SUBLANES)
    for k in range(halves):
        st.a[k] = a[:, k * LANES:(k + 1) * LANES]
        st.b[k] = b[:, k * LANES:(k + 1) * LANES]
    at = jnp.concatenate([st.a[k, last, :] for k in range(halves)], axis=1)
    bt = jnp.concatenate([st.b[k, last, :] for k in range(halves)], axis=1)
    gpos = lax.broadcasted_iota(jnp.int32, (groups, w), 0)
    s = 1
    while s < groups:
        at, bt = _scan_step(at, bt, s, gpos)
        s *= 2
    h_in = fresh(st.h[0:1, :])
    h_end = bt + at * h_in
    st.h[0:1, :] = h_end[groups - 1:groups, :]
    st.c[...] = jnp.where(gpos >= 1, pltpu.roll(h_end, 1, 0), h_in)
    h = jnp.concatenate(
        [jnp.concatenate(
            [st.b[k, g * SUBLANES:(g + 1) * SUBLANES, :]
             + st.a[k, g * SUBLANES:(g + 1) * SUBLANES, :] * st.c[g:g + 1, k * LANES:(k + 1) * LANES]
             for g in range(groups)], axis=0) for k in range(halves)], axis=1)
    y_lru = _gelu_tanh(lg) * h
    return jnp.concatenate([y_sc, y_lru], axis=1).astype(BF16)


class _MixParams(NamedTuple):
    scw: Any
    lcw: Any
    lcb: Any
    wa: Any
    ba: Any
    wx: Any
    bx: Any
    lam: Any


class _MixState(NamedTuple):
    h: Any
    halo_sc: Any
    halo_lx: Any
    a: Any
    b: Any
    c: Any


def _out_ffn_kernel(*refs, final_norm, tiles_per_seq, jobs, n_steps):
    (h_ref, ya_ref, z0_ref, zn_ref, scw_ref, lcw_ref, lcb_ref, wa_ref, ba_ref, wx_ref, bx_ref,
     lam_ref, woa_ref, wom_ref, g2_ref, wg_ref, wu_ref, wd_ref, gf_ref), refs = refs[:19], refs[19:]
    cast_in, refs = refs[:len(jobs)], refs[len(jobs):]
    n_cast_out = sum(len(_cast_widths(j)) for j in jobs)
    o_ref, cast_out, refs = refs[0], refs[1:1 + n_cast_out], refs[1 + n_cast_out:]
    ym_ref, hs_ref, halo_sc_ref, halo_lx_ref, a_ref, b_ref, c_ref = refs
    _cast_chunks(jobs, cast_in, cast_out, n_steps)
    i = pl.program_id(0)
    p = _MixParams(scw_ref, lcw_ref, lcb_ref, wa_ref, ba_ref, wx_ref, bx_ref, lam_ref)
    st = _MixState(hs_ref, halo_sc_ref, halo_lx_ref, a_ref, b_ref, c_ref)

    @pl.when(i == 0)
    def _():
        ym_ref[0] = _mix_tile(z0_ref, p, st, i == 0)

    nxt = i + 1
    ym_cur = ym_ref[lax.rem(i, 2)]
    ym_ref[lax.rem(nxt, 2)] = _mix_tile(zn_ref, p, st, lax.rem(nxt, tiles_per_seq) == 0)

    h = (h_ref[...]
         + lax.dot_general(ya_ref[...], woa_ref[...], (((0,), (0,)), ((), ())),
                           preferred_element_type=F32)
         + jnp.dot(ym_cur, wom_ref[...], preferred_element_type=F32))
    hn = _rms(h, g2_ref[...], RMS_EPS).astype(BF16)
    x = h + 0.5 * _swiglu(hn, wg_ref, wu_ref, wd_ref)
    if final_norm:
        x = _rms(x, gf_ref[...], RMS_EPS)
    o_ref[...] = x


def _out_ffn(h2, ya_t, zr, mix_params, wo, g2, wg, wu, wd, gf, l, final_norm, cast_srcs):
    n = h2.shape[0]
    n_tiles = n // ROW_TILE
    jobs = tuple(_cast_job(src, l + 1, n_tiles, in_proj=(k == len(cast_srcs) - 1))
                 for k, src in enumerate(cast_srcs))
    whole = lambda w: _resident(w.shape, lambda i: (0, 0))
    tiles_per_seq = ya_t.shape[2] // ROW_TILE
    w = SC_WIDTH
    row = lambda width: pl.BlockSpec((ROW_TILE, width), lambda i: (i, 0))
    par = lambda rows: pl.BlockSpec((None, rows, w), lambda i: (l, 0, 0))
    mix_half = SC_WIDTH + LRU_WIDTH
    return pl.pallas_call(
        functools.partial(_out_ffn_kernel, final_norm=final_norm, tiles_per_seq=tiles_per_seq,
                          jobs=jobs, n_steps=n_tiles),
        grid=(n_tiles,),
        in_specs=[
            row(D_MODEL),
            pl.BlockSpec((None, DA_WIDTH, ROW_TILE),
                         lambda i: (i // tiles_per_seq, 0, i % tiles_per_seq)),
            pl.BlockSpec((ROW_TILE, REST_WIDTH), lambda i: (0, 0)),
            pl.BlockSpec((ROW_TILE, REST_WIDTH), lambda i: (jnp.minimum(i + 1, n_tiles - 1), 0)),
            par(SC_KERNEL), par(LRU_CONV), par(1), par(w), par(1), par(w), par(1), par(1),
            _resident((DA_WIDTH, D_MODEL), lambda i: (0, 0)),
            _resident((mix_half, D_MODEL), lambda i: (1, 0)),
            pl.BlockSpec((None, 1, D_MODEL), lambda i: (l, 0, 0)),
            whole(wg), whole(wu), whole(wd),
            pl.BlockSpec((1, D_MODEL), lambda i: (0, 0)),
        ] + [_cast_in_spec(j) for j in jobs],
        out_specs=[row(D_MODEL)] + [sp for j in jobs for sp in _cast_out_specs(j)],
        out_shape=[jax.ShapeDtypeStruct((n, D_MODEL), F32)]
        + [sh for j in jobs for sh in _cast_out_shapes(j)],
        scratch_shapes=[
            pltpu.VMEM((2, ROW_TILE, mix_half), BF16),
            pltpu.VMEM((SUBLANES, w), F32),
            pltpu.VMEM((SUBLANES, w), F32),
            pltpu.VMEM((SUBLANES, w), F32),
            pltpu.VMEM((w // LANES, ROW_TILE, LANES), F32),
            pltpu.VMEM((w // LANES, ROW_TILE, LANES), F32),
            pltpu.VMEM((ROW_TILE // SUBLANES, w), F32),
        ],
        compiler_params=pltpu.CompilerParams(
            dimension_semantics=("arbitrary",), vmem_limit_bytes=VMEM_LIMIT_BYTES),
        name="out_ffn",
    )(h2, ya_t, zr, zr, *mix_params, wo, wo, g2, wg, wu, wd, gf, *[j.src for j in jobs])


def _block_diag(w):
    depth, nb, blk, _ = w.shape
    eye = jnp.eye(nb, dtype=w.dtype)
    return jnp.einsum('lnij,nm->lnimj', w, eye).reshape(depth, nb * blk, nb * blk)


def kernel(x, rel_bias, ffn1_norm, ffn1_gate, ffn1_up, ffn1_down, mix_norm, w_in, w_out, lam_q1, lam_k1, lam_q2, lam_k2, subln_gain, sc_conv_w, lru_conv_w, lru_conv_b, lru_wa, lru_ba, lru_wx, lru_bx, lru_lambda, ffn2_norm, ffn2_gate, ffn2_up, ffn2_down, final_norm):
    b, s, d = x.shape
    depth = w_in.shape[0]
    assert d == D_MODEL and s % ATTN_TILE == 0 and s % ROW_TILE == 0

    bf = lambda w: w.astype(BF16)
    vec = lambda v: v.reshape(depth, 1, v.shape[-1])
    wg, wu, wd = bf(ffn1_gate[0]), bf(ffn1_up[0]), bf(ffn1_down[0])
    col_scale = jnp.where(jnp.arange(QKV_WIDTH) < DA_WIDTH, Q_SCALE, 1.0).astype(F32)
    wq, wr = bf(w_in[0, :, :QKV_WIDTH] * col_scale), bf(w_in[0, :, QKV_WIDTH:])
    wa, wx = bf(_block_diag(lru_wa)), bf(_block_diag(lru_wx))
    lam_params = jnp.stack([lam_q1, lam_k1, lam_q2, lam_k2], axis=1)
    g1, gm, g2 = vec(ffn1_norm), vec(mix_norm), vec(ffn2_norm)
    gs = subln_gain.reshape(depth, DA_VDIM, 1)
    lcb, lam = vec(lru_conv_b), vec(lru_lambda)
    ba, bx = vec(lru_ba.reshape(depth, -1)), vec(lru_bx.reshape(depth, -1))
    gf = final_norm.reshape(1, d)

    bias = _bias_tiles(rel_bias)

    x2 = x.reshape(b * s, d)
    for l in range(depth):
        lam_init = 0.8 - 0.6 * math.exp(-0.3 * l)
        last = l == depth - 1
        h2, zq, zr, wg2, wu2, wd2, wo = _ffn_in(x2, g1, wg, wu, wd, gm, wq, wr, l,
                                                (ffn2_gate, ffn2_up, ffn2_down, w_out))
        ya = _attention(zq.reshape(b, s, QKV_WIDTH), lam_params, gs, bias, l, lam_init)
        x2, *nxt = _out_ffn(h2, ya, zr, (sc_conv_w, lru_conv_w, lcb, wa, ba, wx, bx, lam),
                            wo, g2, wg2, wu2, wd2, gf, l, last,
                            () if last else (ffn1_gate, ffn1_up, ffn1_down, w_in))
        if not last:
            wg, wu, wd, wq, wr = nxt
    return x2.reshape(b, s, d)
```

```python
import functools
import math
from typing import Any, NamedTuple

import jax
import jax.numpy as jnp
from jax import lax
from jax.experimental import pallas as pl
from jax.experimental.pallas import tpu as pltpu

F32 = jnp.float32
BF16 = jnp.bfloat16

D_MODEL = 1024
D_FF = 2816
DA_WIDTH = 512
SC_WIDTH = 256
LRU_WIDTH = 256
DA_HEAD_DIM = 64
DA_HEADS = 4
DA_VDIM = 2 * DA_HEAD_DIM
NUM_BUCKETS = 32
MAX_DISTANCE = 128
SUBLN_EPS = 1e-5
SC_KERNEL = 3
LRU_BLOCKS = 4
LRU_BLOCK = 64
LRU_CONV = 4
LRU_C = 8.0
RMS_EPS = 1e-6
NEG_INF = -1e30
QKV_WIDTH = 3 * DA_WIDTH
REST_WIDTH = 3 * SC_WIDTH + 2 * LRU_WIDTH
IN_WIDTH = QKV_WIDTH + REST_WIDTH
LOG2_E = math.log2(math.e)
Q_SCALE = DA_HEAD_DIM ** -0.5 * LOG2_E

LANES = 128
SUBLANES = 8
BF16_SUBLANES = 16
VT_ROWS = DA_VDIM + BF16_SUBLANES
VMEM_LIMIT_BYTES = 56 * 1024 * 1024
ROW_TILE = 512
ATTN_TILE = 512
SWEEP_UNROLL = 10


def _rms(x, g, eps):
    return x * lax.rsqrt(jnp.mean(x * x, axis=-1, keepdims=True) + eps) * g


def _swiglu(xn, wg_ref, wu_ref, wd_ref):
    g = jnp.dot(xn, wg_ref[...], preferred_element_type=F32)
    u = jnp.dot(xn, wu_ref[...], preferred_element_type=F32)
    a = (g * jax.nn.sigmoid(g) * u).astype(BF16)
    return jnp.dot(a, wd_ref[...], preferred_element_type=F32)


def _resident(shape, index):
    return pl.BlockSpec(shape, index, pipeline_mode=pl.Buffered(1))


class _CastJob(NamedTuple):
    src: Any
    layer: int
    chunks: int
    in_proj: bool


def _cast_job(src, layer, n_steps, in_proj=False):
    rows = src.shape[1]
    chunks = max(c for c in range(1, n_steps + 1)
                 if rows % c == 0 and (rows // c) % BF16_SUBLANES == 0)
    return _CastJob(src, layer, chunks, in_proj)


def _cast_widths(job):
    return (QKV_WIDTH, REST_WIDTH) if job.in_proj else (job.src.shape[2],)


def _cast_in_spec(job):
    rows, last = job.src.shape[1] // job.chunks, job.chunks - 1
    return pl.BlockSpec((None, rows, job.src.shape[2]),
                        lambda i: (job.layer, jnp.minimum(i, last), 0))


def _cast_out_specs(job):
    rows, last = job.src.shape[1] // job.chunks, job.chunks - 1
    return [pl.BlockSpec((rows, w), lambda i: (jnp.minimum(i, last), 0)) for w in _cast_widths(job)]


def _cast_out_shapes(job):
    return [jax.ShapeDtypeStruct((job.src.shape[1], w), BF16) for w in _cast_widths(job)]


def _cast_chunks(jobs, in_refs, out_refs, n_steps):
    out_refs = list(out_refs)
    for job, in_ref in zip(jobs, in_refs):
        outs = [out_refs.pop(0) for _ in _cast_widths(job)]

        def body(job=job, in_ref=in_ref, outs=outs):
            c = in_ref[...]
            if job.in_proj:
                col = lax.broadcasted_iota(jnp.int32, (1, QKV_WIDTH), 1)
                scale = jnp.where(col < DA_WIDTH, Q_SCALE, 1.0)
                outs[0][...] = (c[:, :QKV_WIDTH] * scale).astype(BF16)
                outs[1][...] = c[:, QKV_WIDTH:].astype(BF16)
            else:
                outs[0][...] = c.astype(BF16)

        if job.chunks == n_steps:
            body()
        else:
            pl.when(pl.program_id(0) < job.chunks)(body)


def _ffn_in_kernel(*refs, jobs, n_steps):
    (x_ref, g1_ref, wg_ref, wu_ref, wd_ref, gm_ref, wq_ref, wr_ref), refs = refs[:8], refs[8:]
    cast_in, refs = refs[:len(jobs)], refs[len(jobs):]
    (h_ref, zq_ref, zr_ref), cast_out = refs[:3], refs[3:]
    _cast_chunks(jobs, cast_in, cast_out, n_steps)
    half = ROW_TILE // 2
    for rows in (slice(0, half), slice(half, ROW_TILE)):
        x = x_ref[rows, :]
        xn = _rms(x, g1_ref[...], RMS_EPS).astype(BF16)
        h = x + 0.5 * _swiglu(xn, wg_ref, wu_ref, wd_ref)
        h_ref[rows, :] = h
        u = _rms(h, gm_ref[...], RMS_EPS).astype(BF16)
        zq_ref[rows, :] = jnp.dot(u, wq_ref[...], preferred_element_type=F32).astype(BF16)
        zr_ref[rows, :] = jnp.dot(u, wr_ref[...], preferred_element_type=F32)


def _ffn_in(x2, g1, wg, wu, wd, gm, wq, wr, l, cast_srcs):
    n = x2.shape[0]
    n_steps = n // ROW_TILE
    jobs = tuple(_cast_job(src, l, n_steps) for src in cast_srcs)
    row = lambda w: pl.BlockSpec((ROW_TILE, w), lambda i: (i, 0))
    vec = pl.BlockSpec((None, 1, D_MODEL), lambda i: (l, 0, 0))
    whole = lambda w: _resident(w.shape, lambda i: (0, 0))
    return pl.pallas_call(
        functools.partial(_ffn_in_kernel, jobs=jobs, n_steps=n_steps),
        grid=(n_steps,),
        in_specs=[row(D_MODEL), vec, whole(wg), whole(wu), whole(wd), vec, whole(wq), whole(wr)]
        + [_cast_in_spec(j) for j in jobs],
        out_specs=[row(D_MODEL), row(QKV_WIDTH), row(REST_WIDTH)]
        + [sp for j in jobs for sp in _cast_out_specs(j)],
        out_shape=[
            jax.ShapeDtypeStruct((n, D_MODEL), F32),
            jax.ShapeDtypeStruct((n, QKV_WIDTH), BF16),
            jax.ShapeDtypeStruct((n, REST_WIDTH), F32),
        ] + [sh for j in jobs for sh in _cast_out_shapes(j)],
        compiler_params=pltpu.CompilerParams(
            dimension_semantics=("arbitrary",), vmem_limit_bytes=VMEM_LIMIT_BYTES),
        name="ffn_in",
    )(x2, g1, wg, wu, wd, gm, wq, wr, *[j.src for j in jobs])


def _bias_kernel(rb_ref, o_ref):
    h = pl.program_id(0)
    t = ATTN_TILE
    max_exact = NUM_BUCKETS // 2
    key = lax.broadcasted_iota(jnp.int32, (t, t), 0)
    qry = lax.broadcasted_iota(jnp.int32, (t, t), 1)
    for off in range(2):
        dist = qry - key + off * t
        n = jnp.maximum(dist, 0)
        nf = jnp.maximum(n, 1).astype(F32)
        large = max_exact + (jnp.log(nf / max_exact) / math.log(MAX_DISTANCE / max_exact)
                             * (NUM_BUCKETS - max_exact)).astype(jnp.int32)
        large = jnp.minimum(large, NUM_BUCKETS - 1)
        bucket = jnp.where(n < max_exact, n, large)
        for mp in range(2):
            hm = 2 * h + mp
            val = jnp.zeros((t, t), F32)
            for j in range(NUM_BUCKETS):
                val = jnp.where(bucket == j, rb_ref[j, hm], val)
            far = rb_ref[NUM_BUCKETS - 1, hm]
            o_ref[off, :, mp * t:(mp + 1) * t] = jnp.where(dist >= 0, (val - far) * LOG2_E,
                                                           NEG_INF)


def _bias_tiles(rel_bias):
    t = ATTN_TILE
    return pl.pallas_call(
        _bias_kernel,
        grid=(DA_HEADS,),
        in_specs=[pl.BlockSpec(memory_space=pltpu.SMEM)],
        out_specs=pl.BlockSpec((None, 2, t, 2 * t), lambda i: (i, 0, 0, 0)),
        out_shape=jax.ShapeDtypeStruct((DA_HEADS, 2, t, 2 * t), F32),
        compiler_params=pltpu.CompilerParams(dimension_semantics=("arbitrary",)),
        name="bias_tiles",
    )(rel_bias)


def _attn_kernel(lam_ref, g_ref, bias_ref, q_ref, k_ref, v_ref, o_ref,
                 qq_ref, vt_ref, s_ref, mx_ref, m_ref, acc_ref, *, lam_init, nq):
    t = ATTN_TILE
    chan = lax.broadcasted_iota(jnp.int32, (LANES, t), 0)
    for qi in range(nq):
        q_t = q_ref[qi * t:(qi + 1) * t, :].T
        zero = jnp.zeros_like(q_t)
        qq_ref[qi, :, 0:t] = jnp.where(chan < DA_HEAD_DIM, q_t, zero)
        qq_ref[qi, :, t:2 * t] = jnp.where(chan >= DA_HEAD_DIM, q_t, zero)

    lp = lam_ref[...]
    lam = (jnp.exp(jnp.sum(lp[0:1] * lp[1:2], axis=1, keepdims=True))
           - jnp.exp(jnp.sum(lp[2:3] * lp[3:4], axis=1, keepdims=True)) + lam_init)
    gain = g_ref[...] * (1.0 - lam_init)

    ones_row = lax.broadcasted_iota(jnp.int32, (VT_ROWS - DA_VDIM, t), 0) == 0
    for kj in range(nq):
        vt_ref[kj, 0:DA_VDIM, :] = v_ref[kj * t:(kj + 1) * t, :].T
        vt_ref[kj, DA_VDIM:VT_ROWS, :] = jnp.where(ones_row, 1.0, 0.0).astype(BF16)

    m_ref[...] = jnp.full(m_ref.shape, NEG_INF, F32)
    acc_ref[...] = jnp.zeros(acc_ref.shape, F32)

    def key_rows(kj):
        return pl.ds(kj * t, t) if isinstance(kj, int) else pl.ds(pl.multiple_of(kj * t, t), t)

    def scores(qi, kj, off, buf):
        s = jnp.dot(k_ref[key_rows(kj), :], qq_ref[qi], preferred_element_type=F32)
        if off is not None:
            s = s + bias_ref[off]
        s_ref[buf] = s
        mx_ref[buf] = jnp.max(s, axis=0, keepdims=True)

    def consume(qi, kj, off, buf):
        m_old = m_ref[qi]
        m_new = jnp.maximum(m_old, mx_ref[buf])
        alpha = jnp.exp2(m_old - m_new)
        p = jnp.exp2(s_ref[buf] - m_new)
        pv = jnp.dot(vt_ref[kj], p.astype(BF16), preferred_element_type=F32)
        acc_ref[qi] = alpha * acc_ref[qi] + pv
        m_ref[qi] = m_new

    def sweep(segments):
        segments = [seg for seg in segments if seg[0] > 0]
        if not segments:
            return
        scores(*segments[0][1](0), 0)
        done = 0
        for k, (n_steps, coords) in enumerate(segments):

            def group(i, carry, done=done, coords=coords):
                for u in range(SWEEP_UNROLL):
                    n = SWEEP_UNROLL * i + u
                    scores(*coords(n + 1), (done + u + 1) % 2)
                    consume(*coords(n), (done + u) % 2)
                return carry

            n_groups = (n_steps - 1) // SWEEP_UNROLL
            lax.fori_loop(0, n_groups, group, 0)
            for n in range(n_groups * SWEEP_UNROLL, n_steps):
                if n + 1 < n_steps:
                    scores(*coords(n + 1), (done + n + 1) % 2)
                elif k + 1 < len(segments):
                    scores(*segments[k + 1][1](0), (done + n + 1) % 2)
                consume(*coords(n), (done + n) % 2)
            done += n_steps

    def biased_coords(n):
        if isinstance(n, int):
            qi, off = (n + 1) // 2, n % 2
        else:
            qi, off = lax.shift_right_logical(n + 1, 1), lax.bitwise_and(n, 1)
        return qi, qi - off, off

    def far_coords(n):
        first = lambda qi: (qi - 1) * (qi - 2) // 2
        if isinstance(n, int):
            qi = max(c for c in range(2, nq) if first(c) <= n)
            return qi, n - first(qi), None
        qi = 2
        for c in range(3, nq):
            qi = qi + (n >= first(c)).astype(jnp.int32)
        return qi, n - lax.shift_right_logical((qi - 1) * (qi - 2), 1), None

    sweep([(2 * nq - 1, biased_coords), ((nq - 1) * (nq - 2) // 2, far_coords)])

    for qi in range(nq):
        on = acc_ref[qi, 0:DA_VDIM, :] * (1.0 / acc_ref[qi, DA_VDIM:DA_VDIM + 1, :])
        o = on[:, 0:t] - lam * on[:, t:2 * t]
        y = o * lax.rsqrt(jnp.mean(o * o, axis=0, keepdims=True) + SUBLN_EPS) * gain
        o_ref[:, qi * t:(qi + 1) * t] = y.astype(o_ref.dtype)


def _attention(zq3, lam_params, subln_g, bias, l, lam_init):
    b, s, _ = zq3.shape
    t = ATTN_TILE
    nq = s // t
    return pl.pallas_call(
        functools.partial(_attn_kernel, lam_init=lam_init, nq=nq),
        grid=(b, DA_HEADS),
        in_specs=[
            pl.BlockSpec((None, 4, DA_HEAD_DIM), lambda bi, h: (l, 0, 0)),
            pl.BlockSpec((None, DA_VDIM, 1), lambda bi, h: (l, 0, 0)),
            pl.BlockSpec((None, 2, t, 2 * t), lambda bi, h: (h, 0, 0, 0)),
            pl.BlockSpec((None, s, LANES), lambda bi, h: (bi, 0, h)),
            pl.BlockSpec((None, s, LANES), lambda bi, h: (bi, 0, DA_HEADS + h)),
            pl.BlockSpec((None, s, LANES), lambda bi, h: (bi, 0, 2 * DA_HEADS + h)),
        ],
        out_specs=pl.BlockSpec((None, DA_VDIM, s), lambda bi, h: (bi, h, 0)),
        out_shape=jax.ShapeDtypeStruct((b, DA_WIDTH, s), BF16),
        scratch_shapes=[
            pltpu.VMEM((nq, LANES, 2 * t), BF16),
            pltpu.VMEM((nq, VT_ROWS, t), BF16),
            pltpu.VMEM((2, t, 2 * t), F32),
            pltpu.VMEM((2, 1, 2 * t), F32),
            pltpu.VMEM((nq, 1, 2 * t), F32),
            pltpu.VMEM((nq, VT_ROWS, 2 * t), F32),
        ],
        compiler_params=pltpu.CompilerParams(
            dimension_semantics=("arbitrary", "arbitrary"),
            vmem_limit_bytes=VMEM_LIMIT_BYTES),
        name="diff_attn",
    )(lam_params, subln_g, bias, zq3, zq3, zq3)


def _shift_rows(x, halo, j):
    r = pltpu.roll(x, j, 0)
    head_row = lax.broadcasted_iota(jnp.int32, halo.shape, 0)
    head = jnp.where(head_row < j, pltpu.roll(halo, j, 0), r[:SUBLANES])
    return jnp.concatenate([head, r[SUBLANES:]], axis=0)


def _gelu_tanh(x):
    return 0.5 * x * (1.0 + jnp.tanh(math.sqrt(2.0 / math.pi) * (x + 0.044715 * (x * x * x))))


def _scan_step(a, b, s, pos):
    keep = pos >= s
    a_sh = jnp.where(keep, pltpu.roll(a, s, 0), 1.0)
    b_sh = jnp.where(keep, pltpu.roll(b, s, 0), 0.0)
    return a * a_sh, a * b_sh + b


def _mix_tile(z_ref, p, st, reset):
    rows = z_ref.shape[0]
    w = SC_WIDTH
    groups = rows // SUBLANES
    fresh = lambda x: jnp.where(reset, jnp.zeros_like(x), x)

    sc_b = z_ref[:, 0:w]
    cx = z_ref[:, w:2 * w] * z_ref[:, 2 * w:3 * w]
    halo = fresh(st.halo_sc[...])
    conv = p.scw[SC_KERNEL - 1:SC_KERNEL, :] * cx
    for j in range(1, SC_KERNEL):
        conv = conv + p.scw[SC_KERNEL - 1 - j:SC_KERNEL - j, :] * _shift_rows(cx, halo, j)
    st.halo_sc[...] = cx[rows - SUBLANES:, :]
    y_sc = sc_b * conv

    lx = z_ref[:, 3 * w:4 * w]
    lg = z_ref[:, 4 * w:5 * w]
    halo = fresh(st.halo_lx[...])
    xr = p.lcw[LRU_CONV - 1:LRU_CONV, :] * lx + p.lcb[...]
    for j in range(1, LRU_CONV):
        xr = xr + p.lcw[LRU_CONV - 1 - j:LRU_CONV - j, :] * _shift_rows(lx, halo, j)
    st.halo_lx[...] = lx[rows - SUBLANES:, :]

    xb = xr.astype(BF16)
    r = jax.nn.sigmoid(jnp.dot(xb, p.wa[...], preferred_element_type=F32) + p.ba[...])
    i = jax.nn.sigmoid(jnp.dot(xb, p.wx[...], preferred_element_type=F32) + p.bx[...])
    nl = -p.lam[...]
    softplus = jnp.maximum(nl, 0.0) + jnp.log1p(jnp.exp(-jnp.abs(nl)))
    log_a = (-LRU_C) * r * softplus
    a = jnp.exp(log_a)
    b = jnp.sqrt(-jnp.tanh(log_a) * (a * a + 1.0)) * (i * xr)

    pos = lax.bitwise_and(lax.broadcasted_iota(jnp.int32, (rows, w), 0), SUBLANES - 1)
    s = 1
    while s < SUBLANES:
        a, b = _scan_step(a, b, s, pos)
        s *= 2
    halves = w // LANES
    last = pl.ds(SUBLANES - 1, groups, stride=SUBLANES)
    for k in range(halves):
        st.a[k] = a[:, k * LANES:(k + 1) * LANES]
        st.b[k] = b[:, k * LANES:(k + 1) * LANES]
    at = jnp.concatenate([st.a[k, last, :] for k in range(halves)], axis=1)
    bt = jnp.concatenate([st.b[k, last, :] for k in range(halves)], axis=1)
    gpos = lax.broadcasted_iota(jnp.int32, (groups, w), 0)
    s = 1
    while s < groups:
        at, bt = _scan_step(at, bt, s, gpos)
        s *= 2
    h_in = fresh(st.h[0:1, :])
    h_end = bt + at * h_in
    st.h[0:1, :] = h_end[groups - 1:groups, :]
    st.c[...] = jnp.where(gpos >= 1, pltpu.roll(h_end, 1, 0), h_in)
    h = jnp.concatenate(
        [jnp.concatenate(
            [st.b[k, g * SUBLANES:(g + 1) * SUBLANES, :]
             + st.a[k, g * SUBLANES:(g + 1) * SUBLANES, :] * st.c[g:g + 1, k * LANES:(k + 1) * LANES]
             for g in range(groups)], axis=0) for k in range(halves)], axis=1)
    y_lru = _gelu_tanh(lg) * h
    return jnp.concatenate([y_sc, y_lru], axis=1).astype(BF16)


class _MixParams(NamedTuple):
    scw: Any
    lcw: Any
    lcb: Any
    wa: Any
    ba: Any
    wx: Any
    bx: Any
    lam: Any


class _MixState(NamedTuple):
    h: Any
    halo_sc: Any
    halo_lx: Any
    a: Any
    b: Any
    c: Any


def _out_ffn_kernel(*refs, final_norm, tiles_per_seq, jobs, n_steps):
    (h_ref, ya_ref, z0_ref, zn_ref, scw_ref, lcw_ref, lcb_ref, wa_ref, ba_ref, wx_ref, bx_ref,
     lam_ref, woa_ref, wom_ref, g2_ref, wg_ref, wu_ref, wd_ref, gf_ref), refs = refs[:19], refs[19:]
    cast_in, refs = refs[:len(jobs)], refs[len(jobs):]
    n_cast_out = sum(len(_cast_widths(j)) for j in jobs)
    o_ref, cast_out, refs = refs[0], refs[1:1 + n_cast_out], refs[1 + n_cast_out:]
    ym_ref, hs_ref, halo_sc_ref, halo_lx_ref, a_ref, b_ref, c_ref = refs
    _cast_chunks(jobs, cast_in, cast_out, n_steps)
    i = pl.program_id(0)
    p = _MixParams(scw_ref, lcw_ref, lcb_ref, wa_ref, ba_ref, wx_ref, bx_ref, lam_ref)
    st = _MixState(hs_ref, halo_sc_ref, halo_lx_ref, a_ref, b_ref, c_ref)

    @pl.when(i == 0)
    def _():
        ym_ref[0] = _mix_tile(z0_ref, p, st, i == 0)

    nxt = i + 1
    ym_cur = ym_ref[lax.rem(i, 2)]
    ym_ref[lax.rem(nxt, 2)] = _mix_tile(zn_ref, p, st, lax.rem(nxt, tiles_per_seq) == 0)

    h = (h_ref[...]
         + lax.dot_general(ya_ref[...], woa_ref[...], (((0,), (0,)), ((), ())),
                           preferred_element_type=F32)
         + jnp.dot(ym_cur, wom_ref[...], preferred_element_type=F32))
    hn = _rms(h, g2_ref[...], RMS_EPS).astype(BF16)
    x = h + 0.5 * _swiglu(hn, wg_ref, wu_ref, wd_ref)
    if final_norm:
        x = _rms(x, gf_ref[...], RMS_EPS)
    o_ref[...] = x


def _out_ffn(h2, ya_t, zr, mix_params, wo, g2, wg, wu, wd, gf, l, final_norm, cast_srcs):
    n = h2.shape[0]
    n_tiles = n // ROW_TILE
    jobs = tuple(_cast_job(src, l + 1, n_tiles, in_proj=(k == len(cast_srcs) - 1))
                 for k, src in enumerate(cast_srcs))
    whole = lambda w: _resident(w.shape, lambda i: (0, 0))
    tiles_per_seq = ya_t.shape[2] // ROW_TILE
    w = SC_WIDTH
    row = lambda width: pl.BlockSpec((ROW_TILE, width), lambda i: (i, 0))
    par = lambda rows: pl.BlockSpec((None, rows, w), lambda i: (l, 0, 0))
    mix_half = SC_WIDTH + LRU_WIDTH
    return pl.pallas_call(
        functools.partial(_out_ffn_kernel, final_norm=final_norm, tiles_per_seq=tiles_per_seq,
                          jobs=jobs, n_steps=n_tiles),
        grid=(n_tiles,),
        in_specs=[
            row(D_MODEL),
            pl.BlockSpec((None, DA_WIDTH, ROW_TILE),
                         lambda i: (i // tiles_per_seq, 0, i % tiles_per_seq)),
            pl.BlockSpec((ROW_TILE, REST_WIDTH), lambda i: (0, 0)),
            pl.BlockSpec((ROW_TILE, REST_WIDTH), lambda i: (jnp.minimum(i + 1, n_tiles - 1), 0)),
            par(SC_KERNEL), par(LRU_CONV), par(1), par(w), par(1), par(w), par(1), par(1),
            _resident((DA_WIDTH, D_MODEL), lambda i: (0, 0)),
            _resident((mix_half, D_MODEL), lambda i: (1, 0)),
            pl.BlockSpec((None, 1, D_MODEL), lambda i: (l, 0, 0)),
            whole(wg), whole(wu), whole(wd),
            pl.BlockSpec((1, D_MODEL), lambda i: (0, 0)),
        ] + [_cast_in_spec(j) for j in jobs],
        out_specs=[row(D_MODEL)] + [sp for j in jobs for sp in _cast_out_specs(j)],
        out_shape=[jax.ShapeDtypeStruct((n, D_MODEL), F32)]
        + [sh for j in jobs for sh in _cast_out_shapes(j)],
        scratch_shapes=[
            pltpu.VMEM((2, ROW_TILE, mix_half), BF16),
            pltpu.VMEM((SUBLANES, w), F32),
            pltpu.VMEM((SUBLANES, w), F32),
            pltpu.VMEM((SUBLANES, w), F32),
            pltpu.VMEM((w // LANES, ROW_TILE, LANES), F32),
            pltpu.VMEM((w // LANES, ROW_TILE, LANES), F32),
            pltpu.VMEM((ROW_TILE // SUBLANES, w), F32),
        ],
        compiler_params=pltpu.CompilerParams(
            dimension_semantics=("arbitrary",), vmem_limit_bytes=VMEM_LIMIT_BYTES),
        name="out_ffn",
    )(h2, ya_t, zr, zr, *mix_params, wo, wo, g2, wg, wu, wd, gf, *[j.src for j in jobs])


def _block_diag(w):
    depth, nb, blk, _ = w.shape
    eye = jnp.eye(nb, dtype=w.dtype)
    return jnp.einsum('lnij,nm->lnimj', w, eye).reshape(depth, nb * blk, nb * blk)


def kernel(x, rel_bias, ffn1_norm, ffn1_gate, ffn1_up, ffn1_down, mix_norm, w_in, w_out, lam_q1, lam_k1, lam_q2, lam_k2, subln_gain, sc_conv_w, lru_conv_w, lru_conv_b, lru_wa, lru_ba, lru_wx, lru_bx, lru_lambda, ffn2_norm, ffn2_gate, ffn2_up, ffn2_down, final_norm):
    b, s, d = x.shape
    depth = w_in.shape[0]
    assert d == D_MODEL and s % ATTN_TILE == 0 and s % ROW_TILE == 0

    bf = lambda w: w.astype(BF16)
    vec = lambda v: v.reshape(depth, 1, v.shape[-1])
    wg, wu, wd = bf(ffn1_gate[0]), bf(ffn1_up[0]), bf(ffn1_down[0])
    col_scale = jnp.where(jnp.arange(QKV_WIDTH) < DA_WIDTH, Q_SCALE, 1.0).astype(F32)
    wq, wr = bf(w_in[0, :, :QKV_WIDTH] * col_scale), bf(w_in[0, :, QKV_WIDTH:])
    wa, wx = bf(_block_diag(lru_wa)), bf(_block_diag(lru_wx))
    lam_params = jnp.stack([lam_q1, lam_k1, lam_q2, lam_k2], axis=1)
    g1, gm, g2 = vec(ffn1_norm), vec(mix_norm), vec(ffn2_norm)
    gs = subln_gain.reshape(depth, DA_VDIM, 1)
    lcb, lam = vec(lru_conv_b), vec(lru_lambda)
    ba, bx = vec(lru_ba.reshape(depth, -1)), vec(lru_bx.reshape(depth, -1))
    gf = final_norm.reshape(1, d)

    bias = _bias_tiles(rel_bias)

    x2 = x.reshape(b * s, d)
    for l in range(depth):
        lam_init = 0.8 - 0.6 * math.exp(-0.3 * l)
        last = l == depth - 1
        h2, zq, zr, wg2, wu2, wd2, wo = _ffn_in(x2, g1, wg, wu, wd, gm, wq, wr, l,
                                                (ffn2_gate, ffn2_up, ffn2_down, w_out))
        ya = _attention(zq.reshape(b, s, QKV_WIDTH), lam_params, gs, bias, l, lam_init)
        x2, *nxt = _out_ffn(h2, ya, zr, (sc_conv_w, lru_conv_w, lcb, wa, ba, wx, bx, lam),
                            wo, g2, wg2, wu2, wd2, gf, l, last,
                            () if last else (ffn1_gate, ffn1_up, ffn1_down, w_in))
        if not last:
            wg, wu, wd, wq, wr = nxt
    return x2.reshape(b, s, d)
```

```python
import functools
import math
from typing import Any, NamedTuple

import jax
import jax.numpy as jnp
from jax import lax
from jax.experimental import pallas as pl
from jax.experimental.pallas import tpu as pltpu

F32 = jnp.float32
BF16 = jnp.bfloat16

D_MODEL = 1024
D_FF = 2816
DA_WIDTH = 512
SC_WIDTH = 256
LRU_WIDTH = 256
DA_HEAD_DIM = 64
DA_HEADS = 4
DA_VDIM = 2 * DA_HEAD_DIM
NUM_BUCKETS = 32
MAX_DISTANCE = 128
SUBLN_EPS = 1e-5
SC_KERNEL = 3
LRU_BLOCKS = 4
LRU_BLOCK = 64
LRU_CONV = 4
LRU_C = 8.0
RMS_EPS = 1e-6
NEG_INF = -1e30
QKV_WIDTH = 3 * DA_WIDTH
REST_WIDTH = 3 * SC_WIDTH + 2 * LRU_WIDTH
IN_WIDTH = QKV_WIDTH + REST_WIDTH
LOG2_E = math.log2(math.e)
Q_SCALE = DA_HEAD_DIM ** -0.5 * LOG2_E

LANES = 128
SUBLANES = 8
BF16_SUBLANES = 16
VT_ROWS = DA_VDIM + BF16_SUBLANES
VMEM_LIMIT_BYTES = 56 * 1024 * 1024
ROW_TILE = 512
ATTN_TILE = 512
SWEEP_UNROLL = 8


def _rms(x, g, eps):
    return x * lax.rsqrt(jnp.mean(x * x, axis=-1, keepdims=True) + eps) * g


def _swiglu(xn, wg_ref, wu_ref, wd_ref):
    g = jnp.dot(xn, wg_ref[...], preferred_element_type=F32)
    u = jnp.dot(xn, wu_ref[...], preferred_element_type=F32)
    a = (g * jax.nn.sigmoid(g) * u).astype(BF16)
    return jnp.dot(a, wd_ref[...], preferred_element_type=F32)


def _resident(shape, index):
    return pl.BlockSpec(shape, index, pipeline_mode=pl.Buffered(1))


class _CastJob(NamedTuple):
    src: Any
    layer: int
    chunks: int
    in_proj: bool


def _cast_job(src, layer, n_steps, in_proj=False):
    rows = src.shape[1]
    chunks = max(c for c in range(1, n_steps + 1)
                 if rows % c == 0 and (rows // c) % BF16_SUBLANES == 0)
    return _CastJob(src, layer, chunks, in_proj)


def _cast_widths(job):
    return (QKV_WIDTH, REST_WIDTH) if job.in_proj else (job.src.shape[2],)


def _cast_in_spec(job):
    rows, last = job.src.shape[1] // job.chunks, job.chunks - 1
    return pl.BlockSpec((None, rows, job.src.shape[2]),
                        lambda i: (job.layer, jnp.minimum(i, last), 0))


def _cast_out_specs(job):
    rows, last = job.src.shape[1] // job.chunks, job.chunks - 1
    return [pl.BlockSpec((rows, w), lambda i: (jnp.minimum(i, last), 0)) for w in _cast_widths(job)]


def _cast_out_shapes(job):
    return [jax.ShapeDtypeStruct((job.src.shape[1], w), BF16) for w in _cast_widths(job)]


def _cast_chunks(jobs, in_refs, out_refs, n_steps):
    out_refs = list(out_refs)
    for job, in_ref in zip(jobs, in_refs):
        outs = [out_refs.pop(0) for _ in _cast_widths(job)]

        def body(job=job, in_ref=in_ref, outs=outs):
            c = in_ref[...]
            if job.in_proj:
                col = lax.broadcasted_iota(jnp.int32, (1, QKV_WIDTH), 1)
                scale = jnp.where(col < DA_WIDTH, Q_SCALE, 1.0)
                outs[0][...] = (c[:, :QKV_WIDTH] * scale).astype(BF16)
                outs[1][...] = c[:, QKV_WIDTH:].astype(BF16)
            else:
                outs[0][...] = c.astype(BF16)

        if job.chunks == n_steps:
            body()
        else:
            pl.when(pl.program_id(0) < job.chunks)(body)


def _ffn_in_kernel(*refs, jobs, n_steps):
    (x_ref, g1_ref, wg_ref, wu_ref, wd_ref, gm_ref, wq_ref, wr_ref), refs = refs[:8], refs[8:]
    cast_in, refs = refs[:len(jobs)], refs[len(jobs):]
    (h_ref, zq_ref, zr_ref), cast_out = refs[:3], refs[3:]
    _cast_chunks(jobs, cast_in, cast_out, n_steps)
    half = ROW_TILE // 2
    for rows in (slice(0, half), slice(half, ROW_TILE)):
        x = x_ref[rows, :]
        xn = _rms(x, g1_ref[...], RMS_EPS).astype(BF16)
        h = x + 0.5 * _swiglu(xn, wg_ref, wu_ref, wd_ref)
        h_ref[rows, :] = h
        u = _rms(h, gm_ref[...], RMS_EPS).astype(BF16)
        zq_ref[rows, :] = jnp.dot(u, wq_ref[...], preferred_element_type=F32).astype(BF16)
        zr_ref[rows, :] = jnp.dot(u, wr_ref[...], preferred_element_type=F32)


def _ffn_in(x2, g1, wg, wu, wd, gm, wq, wr, l, cast_srcs):
    n = x2.shape[0]
    n_steps = n // ROW_TILE
    jobs = tuple(_cast_job(src, l, n_steps) for src in cast_srcs)
    row = lambda w: pl.BlockSpec((ROW_TILE, w), lambda i: (i, 0))
    vec = pl.BlockSpec((None, 1, D_MODEL), lambda i: (l, 0, 0))
    whole = lambda w: _resident(w.shape, lambda i: (0, 0))
    return pl.pallas_call(
        functools.partial(_ffn_in_kernel, jobs=jobs, n_steps=n_steps),
        grid=(n_steps,),
        in_specs=[row(D_MODEL), vec, whole(wg), whole(wu), whole(wd), vec, whole(wq), whole(wr)]
        + [_cast_in_spec(j) for j in jobs],
        out_specs=[row(D_MODEL), row(QKV_WIDTH), row(REST_WIDTH)]
        + [sp for j in jobs for sp in _cast_out_specs(j)],
        out_shape=[
            jax.ShapeDtypeStruct((n, D_MODEL), F32),
            jax.ShapeDtypeStruct((n, QKV_WIDTH), BF16),
            jax.ShapeDtypeStruct((n, REST_WIDTH), F32),
        ] + [sh for j in jobs for sh in _cast_out_shapes(j)],
        compiler_params=pltpu.CompilerParams(
            dimension_semantics=("arbitrary",), vmem_limit_bytes=VMEM_LIMIT_BYTES),
        name="ffn_in",
    )(x2, g1, wg, wu, wd, gm, wq, wr, *[j.src for j in jobs])


def _bias_kernel(rb_ref, o_ref):
    h = pl.program_id(0)
    t = ATTN_TILE
    sub = MAX_DISTANCE
    max_exact = NUM_BUCKETS // 2
    key = lax.broadcasted_iota(jnp.int32, (sub, sub), 0)
    qry = lax.broadcasted_iota(jnp.int32, (sub, sub), 1)

    def buckets(delta):
        dist = qry - key + delta
        n = jnp.maximum(dist, 0)
        nf = jnp.maximum(n, 1).astype(F32)
        large = max_exact + (jnp.log(nf / max_exact) / math.log(MAX_DISTANCE / max_exact)
                             * (NUM_BUCKETS - max_exact)).astype(jnp.int32)
        large = jnp.minimum(large, NUM_BUCKETS - 1)
        return jnp.where(n < max_exact, n, large), dist

    near = {delta: buckets(delta) for delta in (0, sub)}
    for mp in range(2):
        hm = 2 * h + mp
        far = rb_ref[NUM_BUCKETS - 1, hm]
        pattern = {}
        for delta, (bucket, dist) in near.items():
            val = jnp.zeros((sub, sub), F32)
            for j in range(NUM_BUCKETS):
                val = jnp.where(bucket == j, rb_ref[j, hm], val)
            pattern[delta] = jnp.where(dist >= 0, (val - far) * LOG2_E, NEG_INF)
        for off in range(2):
            for bk in range(t // sub):
                for bq in range(t // sub):
                    delta = (bq - bk) * sub + off * t
                    if delta in pattern:
                        blk = pattern[delta]
                    else:
                        blk = jnp.full((sub, sub), NEG_INF if delta < 0 else 0.0, F32)
                    o_ref[off, bk * sub:(bk + 1) * sub,
                          mp * t + bq * sub:mp * t + (bq + 1) * sub] = blk


def _bias_tiles(rel_bias):
    t = ATTN_TILE
    return pl.pallas_call(
        _bias_kernel,
        grid=(DA_HEADS,),
        in_specs=[pl.BlockSpec(memory_space=pltpu.SMEM)],
        out_specs=pl.BlockSpec((None, 2, t, 2 * t), lambda i: (i, 0, 0, 0)),
        out_shape=jax.ShapeDtypeStruct((DA_HEADS, 2, t, 2 * t), F32),
        compiler_params=pltpu.CompilerParams(dimension_semantics=("arbitrary",)),
        name="bias_tiles",
    )(rel_bias)


def _attn_kernel(lam_ref, g_ref, bias_ref, q_ref, k_ref, v_ref, o_ref,
                 qq_ref, vt_ref, s_ref, mx_ref, m_ref, acc_ref, *, lam_init, nq):
    t = ATTN_TILE
    chan = lax.broadcasted_iota(jnp.int32, (LANES, t), 0)
    for qi in range(nq):
        q_t = q_ref[qi * t:(qi + 1) * t, :].T
        zero = jnp.zeros_like(q_t)
        qq_ref[qi, :, 0:t] = jnp.where(chan < DA_HEAD_DIM, q_t, zero)
        qq_ref[qi, :, t:2 * t] = jnp.where(chan >= DA_HEAD_DIM, q_t, zero)

    lp = lam_ref[...]
    lam = (jnp.exp(jnp.sum(lp[0:1] * lp[1:2], axis=1, keepdims=True))
           - jnp.exp(jnp.sum(lp[2:3] * lp[3:4], axis=1, keepdims=True)) + lam_init)
    gain = g_ref[...] * (1.0 - lam_init)

    ones_row = lax.broadcasted_iota(jnp.int32, (VT_ROWS - DA_VDIM, t), 0) == 0
    for kj in range(nq):
        vt_ref[kj, 0:DA_VDIM, :] = v_ref[kj * t:(kj + 1) * t, :].T
        vt_ref[kj, DA_VDIM:VT_ROWS, :] = jnp.where(ones_row, 1.0, 0.0).astype(BF16)

    m_ref[...] = jnp.full(m_ref.shape, NEG_INF, F32)
    acc_ref[...] = jnp.zeros(acc_ref.shape, F32)

    def key_rows(kj):
        return pl.ds(kj * t, t) if isinstance(kj, int) else pl.ds(pl.multiple_of(kj * t, t), t)

    def scores(qi, kj, off, buf):
        s = jnp.dot(k_ref[key_rows(kj), :], qq_ref[qi], preferred_element_type=F32)
        if off is not None:
            s = s + bias_ref[off]
        s_ref[buf] = s
        mx_ref[buf] = jnp.max(s, axis=0, keepdims=True)

    def consume(qi, kj, off, buf):
        m_old = m_ref[qi]
        m_new = jnp.maximum(m_old, mx_ref[buf])
        alpha = jnp.exp2(m_old - m_new)
        p = jnp.exp2(s_ref[buf] - m_new)
        pv = jnp.dot(vt_ref[kj], p.astype(BF16), preferred_element_type=F32)
        acc_ref[qi] = alpha * acc_ref[qi] + pv
        m_ref[qi] = m_new

    def sweep(segments):
        segments = [seg for seg in segments if seg[0] > 0]
        if not segments:
            return
        scores(*segments[0][1](0), 0)
        done = 0
        for k, (n_steps, coords) in enumerate(segments):

            def group(i, carry, done=done, coords=coords):
                for u in range(SWEEP_UNROLL):
                    n = SWEEP_UNROLL * i + u
                    scores(*coords(n + 1), (done + u + 1) % 2)
                    consume(*coords(n), (done + u) % 2)
                return carry

            n_groups = (n_steps - 1) // SWEEP_UNROLL
            lax.fori_loop(0, n_groups, group, 0)
            for n in range(n_groups * SWEEP_UNROLL, n_steps):
                if n + 1 < n_steps:
                    scores(*coords(n + 1), (done + n + 1) % 2)
                elif k + 1 < len(segments):
                    scores(*segments[k + 1][1](0), (done + n + 1) % 2)
                consume(*coords(n), (done + n) % 2)
            done += n_steps

    def biased_coords(n):
        if isinstance(n, int):
            qi, off = (n + 1) // 2, n % 2
        else:
            qi, off = lax.shift_right_logical(n + 1, 1), lax.bitwise_and(n, 1)
        return qi, qi - off, off

    def far_coords(n):
        first = lambda qi: (qi - 1) * (qi - 2) // 2
        if isinstance(n, int):
            qi = max(c for c in range(2, nq) if first(c) <= n)
            return qi, n - first(qi), None
        qi = 2
        for c in range(3, nq):
            qi = qi + (n >= first(c)).astype(jnp.int32)
        return qi, n - lax.shift_right_logical((qi - 1) * (qi - 2), 1), None

    sweep([(2 * nq - 1, biased_coords), ((nq - 1) * (nq - 2) // 2, far_coords)])

    for qi in range(nq):
        on = acc_ref[qi, 0:DA_VDIM, :] * (1.0 / acc_ref[qi, DA_VDIM:DA_VDIM + 1, :])
        o = on[:, 0:t] - lam * on[:, t:2 * t]
        y = o * lax.rsqrt(jnp.mean(o * o, axis=0, keepdims=True) + SUBLN_EPS) * gain
        o_ref[:, qi * t:(qi + 1) * t] = y.astype(o_ref.dtype)


def _attention(zq3, lam_params, subln_g, bias, l, lam_init):
    b, s, _ = zq3.shape
    t = ATTN_TILE
    nq = s // t
    return pl.pallas_call(
        functools.partial(_attn_kernel, lam_init=lam_init, nq=nq),
        grid=(b, DA_HEADS),
        in_specs=[
            pl.BlockSpec((None, 4, DA_HEAD_DIM), lambda bi, h: (l, 0, 0)),
            pl.BlockSpec((None, DA_VDIM, 1), lambda bi, h: (l, 0, 0)),
            pl.BlockSpec((None, 2, t, 2 * t), lambda bi, h: (h, 0, 0, 0)),
            pl.BlockSpec((None, s, LANES), lambda bi, h: (bi, 0, h)),
            pl.BlockSpec((None, s, LANES), lambda bi, h: (bi, 0, DA_HEADS + h)),
            pl.BlockSpec((None, s, LANES), lambda bi, h: (bi, 0, 2 * DA_HEADS + h)),
        ],
        out_specs=pl.BlockSpec((None, DA_VDIM, s), lambda bi, h: (bi, h, 0)),
        out_shape=jax.ShapeDtypeStruct((b, DA_WIDTH, s), BF16),
        scratch_shapes=[
            pltpu.VMEM((nq, LANES, 2 * t), BF16),
            pltpu.VMEM((nq, VT_ROWS, t), BF16),
            pltpu.VMEM((2, t, 2 * t), F32),
            pltpu.VMEM((2, 1, 2 * t), F32),
            pltpu.VMEM((nq, 1, 2 * t), F32),
            pltpu.VMEM((nq, VT_ROWS, 2 * t), F32),
        ],
        compiler_params=pltpu.CompilerParams(
            dimension_semantics=("arbitrary", "arbitrary"),
            vmem_limit_bytes=VMEM_LIMIT_BYTES),
        name="diff_attn",
    )(lam_params, subln_g, bias, zq3, zq3, zq3)


def _shift_rows(x, halo, j):
    r = pltpu.roll(x, j, 0)
    head_row = lax.broadcasted_iota(jnp.int32, halo.shape, 0)
    head = jnp.where(head_row < j, pltpu.roll(halo, j, 0), r[:SUBLANES])
    return jnp.concatenate([head, r[SUBLANES:]], axis=0)


def _gelu_tanh(x):
    return 0.5 * x * (1.0 + jnp.tanh(math.sqrt(2.0 / math.pi) * (x + 0.044715 * (x * x * x))))


def _scan_step(a, b, s, pos):
    keep = pos >= s
    a_sh = jnp.where(keep, pltpu.roll(a, s, 0), 1.0)
    b_sh = jnp.where(keep, pltpu.roll(b, s, 0), 0.0)
    return a * a_sh, a * b_sh + b


def _mix_tile(z_ref, p, st, reset):
    rows = z_ref.shape[0]
    w = SC_WIDTH
    groups = rows // SUBLANES
    fresh = lambda x: jnp.where(reset, jnp.zeros_like(x), x)

    sc_b = z_ref[:, 0:w]
    cx = z_ref[:, w:2 * w] * z_ref[:, 2 * w:3 * w]
    halo = fresh(st.halo_sc[...])
    conv = p.scw[SC_KERNEL - 1:SC_KERNEL, :] * cx
    for j in range(1, SC_KERNEL):
        conv = conv + p.scw[SC_KERNEL - 1 - j:SC_KERNEL - j, :] * _shift_rows(cx, halo, j)
    st.halo_sc[...] = cx[rows - SUBLANES:, :]
    y_sc = sc_b * conv

    lx = z_ref[:, 3 * w:4 * w]
    lg = z_ref[:, 4 * w:5 * w]
    halo = fresh(st.halo_lx[...])
    xr = p.lcw[LRU_CONV - 1:LRU_CONV, :] * lx + p.lcb[...]
    for j in range(1, LRU_CONV):
        xr = xr + p.lcw[LRU_CONV - 1 - j:LRU_CONV - j, :] * _shift_rows(lx, halo, j)
    st.halo_lx[...] = lx[rows - SUBLANES:, :]

    xb = xr.astype(BF16)
    r = jax.nn.sigmoid(jnp.dot(xb, p.wa[...], preferred_element_type=F32) + p.ba[...])
    i = jax.nn.sigmoid(jnp.dot(xb, p.wx[...], preferred_element_type=F32) + p.bx[...])
    nl = -p.lam[...]
    softplus = jnp.maximum(nl, 0.0) + jnp.log1p(jnp.exp(-jnp.abs(nl)))
    log_a = (-LRU_C) * r * softplus
    a = jnp.exp(log_a)
    b = jnp.sqrt(-jnp.tanh(log_a) * (a * a + 1.0)) * (i * xr)

    pos = lax.bitwise_and(lax.broadcasted_iota(jnp.int32, (rows, w), 0), SUBLANES - 1)
    s = 1
    while s < SUBLANES:
        a, b = _scan_step(a, b, s, pos)
        s *= 2
    halves = w // LANES
    last = pl.ds(SUBLANES - 1, groups, stride=SUBLANES)
    for k in range(halves):
        st.a[k] = a[:, k * LANES:(k + 1) * LANES]
        st.b[k] = b[:, k * LANES:(k + 1) * LANES]
    at = jnp.concatenate([st.a[k, last, :] for k in range(halves)], axis=1)
    bt = jnp.concatenate([st.b[k, last, :] for k in range(halves)], axis=1)
    gpos = lax.broadcasted_iota(jnp.int32, (groups, w), 0)
    s = 1
    while s < groups:
        at, bt = _scan_step(at, bt, s, gpos)
        s *= 2
    h_in = fresh(st.h[0:1, :])
    h_end = bt + at * h_in
    st.h[0:1, :] = h_end[groups - 1:groups, :]
    st.c[...] = jnp.where(gpos >= 1, pltpu.roll(h_end, 1, 0), h_in)
    h = jnp.concatenate(
        [jnp.concatenate(
            [st.b[k, g * SUBLANES:(g + 1) * SUBLANES, :]
             + st.a[k, g * SUBLANES:(g + 1) * SUBLANES, :] * st.c[g:g + 1, k * LANES:(k + 1) * LANES]
             for g in range(groups)], axis=0) for k in range(halves)], axis=1)
    y_lru = _gelu_tanh(lg) * h
    return jnp.concatenate([y_sc, y_lru], axis=1).astype(BF16)


class _MixParams(NamedTuple):
    scw: Any
    lcw: Any
    lcb: Any
    wa: Any
    ba: Any
    wx: Any
    bx: Any
    lam: Any


class _MixState(NamedTuple):
    h: Any
    halo_sc: Any
    halo_lx: Any
    a: Any
    b: Any
    c: Any


def _out_ffn_kernel(*refs, final_norm, tiles_per_seq, jobs, n_steps):
    (h_ref, ya_ref, z0_ref, zn_ref, scw_ref, lcw_ref, lcb_ref, wa_ref, ba_ref, wx_ref, bx_ref,
     lam_ref, woa_ref, wom_ref, g2_ref, wg_ref, wu_ref, wd_ref, gf_ref), refs = refs[:19], refs[19:]
    cast_in, refs = refs[:len(jobs)], refs[len(jobs):]
    n_cast_out = sum(len(_cast_widths(j)) for j in jobs)
    o_ref, cast_out, refs = refs[0], refs[1:1 + n_cast_out], refs[1 + n_cast_out:]
    ym_ref, hs_ref, halo_sc_ref, halo_lx_ref, a_ref, b_ref, c_ref = refs
    _cast_chunks(jobs, cast_in, cast_out, n_steps)
    i = pl.program_id(0)
    p = _MixParams(scw_ref, lcw_ref, lcb_ref, wa_ref, ba_ref, wx_ref, bx_ref, lam_ref)
    st = _MixState(hs_ref, halo_sc_ref, halo_lx_ref, a_ref, b_ref, c_ref)

    @pl.when(i == 0)
    def _():
        ym_ref[0] = _mix_tile(z0_ref, p, st, i == 0)

    nxt = i + 1
    ym_cur = ym_ref[lax.rem(i, 2)]
    ym_ref[lax.rem(nxt, 2)] = _mix_tile(zn_ref, p, st, lax.rem(nxt, tiles_per_seq) == 0)

    h = (h_ref[...]
         + lax.dot_general(ya_ref[...], woa_ref[...], (((0,), (0,)), ((), ())),
                           preferred_element_type=F32)
         + jnp.dot(ym_cur, wom_ref[...], preferred_element_type=F32))
    hn = _rms(h, g2_ref[...], RMS_EPS).astype(BF16)
    x = h + 0.5 * _swiglu(hn, wg_ref, wu_ref, wd_ref)
    if final_norm:
        x = _rms(x, gf_ref[...], RMS_EPS)
    o_ref[...] = x


def _out_ffn(h2, ya_t, zr, mix_params, wo, g2, wg, wu, wd, gf, l, final_norm, cast_srcs):
    n = h2.shape[0]
    n_tiles = n // ROW_TILE
    jobs = tuple(_cast_job(src, l + 1, n_tiles, in_proj=(k == len(cast_srcs) - 1))
                 for k, src in enumerate(cast_srcs))
    whole = lambda w: _resident(w.shape, lambda i: (0, 0))
    tiles_per_seq = ya_t.shape[2] // ROW_TILE
    w = SC_WIDTH
    row = lambda width: pl.BlockSpec((ROW_TILE, width), lambda i: (i, 0))
    par = lambda rows: pl.BlockSpec((None, rows, w), lambda i: (l, 0, 0))
    mix_half = SC_WIDTH + LRU_WIDTH
    return pl.pallas_call(
        functools.partial(_out_ffn_kernel, final_norm=final_norm, tiles_per_seq=tiles_per_seq,
                          jobs=jobs, n_steps=n_tiles),
        grid=(n_tiles,),
        in_specs=[
            row(D_MODEL),
            pl.BlockSpec((None, DA_WIDTH, ROW_TILE),
                         lambda i: (i // tiles_per_seq, 0, i % tiles_per_seq)),
            pl.BlockSpec((ROW_TILE, REST_WIDTH), lambda i: (0, 0)),
            pl.BlockSpec((ROW_TILE, REST_WIDTH), lambda i: (jnp.minimum(i + 1, n_tiles - 1), 0)),
            par(SC_KERNEL), par(LRU_CONV), par(1), par(w), par(1), par(w), par(1), par(1),
            _resident((DA_WIDTH, D_MODEL), lambda i: (0, 0)),
            _resident((mix_half, D_MODEL), lambda i: (1, 0)),
            pl.BlockSpec((None, 1, D_MODEL), lambda i: (l, 0, 0)),
            whole(wg), whole(wu), whole(wd),
            pl.BlockSpec((1, D_MODEL), lambda i: (0, 0)),
        ] + [_cast_in_spec(j) for j in jobs],
        out_specs=[row(D_MODEL)] + [sp for j in jobs for sp in _cast_out_specs(j)],
        out_shape=[jax.ShapeDtypeStruct((n, D_MODEL), F32)]
        + [sh for j in jobs for sh in _cast_out_shapes(j)],
        scratch_shapes=[
            pltpu.VMEM((2, ROW_TILE, mix_half), BF16),
            pltpu.VMEM((SUBLANES, w), F32),
            pltpu.VMEM((SUBLANES, w), F32),
            pltpu.VMEM((SUBLANES, w), F32),
            pltpu.VMEM((w // LANES, ROW_TILE, LANES), F32),
            pltpu.VMEM((w // LANES, ROW_TILE, LANES), F32),
            pltpu.VMEM((ROW_TILE // SUBLANES, w), F32),
        ],
        compiler_params=pltpu.CompilerParams(
            dimension_semantics=("arbitrary",), vmem_limit_bytes=VMEM_LIMIT_BYTES),
        name="out_ffn",
    )(h2, ya_t, zr, zr, *mix_params, wo, wo, g2, wg, wu, wd, gf, *[j.src for j in jobs])


def _block_diag(w):
    depth, nb, blk, _ = w.shape
    eye = jnp.eye(nb, dtype=w.dtype)
    return jnp.einsum('lnij,nm->lnimj', w, eye).reshape(depth, nb * blk, nb * blk)


def kernel(x, rel_bias, ffn1_norm, ffn1_gate, ffn1_up, ffn1_down, mix_norm, w_in, w_out, lam_q1, lam_k1, lam_q2, lam_k2, subln_gain, sc_conv_w, lru_conv_w, lru_conv_b, lru_wa, lru_ba, lru_wx, lru_bx, lru_lambda, ffn2_norm, ffn2_gate, ffn2_up, ffn2_down, final_norm):
    b, s, d = x.shape
    depth = w_in.shape[0]
    assert d == D_MODEL and s % ATTN_TILE == 0 and s % ROW_TILE == 0

    bf = lambda w: w.astype(BF16)
    vec = lambda v: v.reshape(depth, 1, v.shape[-1])
    wg, wu, wd = bf(ffn1_gate[0]), bf(ffn1_up[0]), bf(ffn1_down[0])
    col_scale = jnp.where(jnp.arange(QKV_WIDTH) < DA_WIDTH, Q_SCALE, 1.0).astype(F32)
    wq, wr = bf(w_in[0, :, :QKV_WIDTH] * col_scale), bf(w_in[0, :, QKV_WIDTH:])
    wa, wx = bf(_block_diag(lru_wa)), bf(_block_diag(lru_wx))
    lam_params = jnp.stack([lam_q1, lam_k1, lam_q2, lam_k2], axis=1)
    g1, gm, g2 = vec(ffn1_norm), vec(mix_norm), vec(ffn2_norm)
    gs = subln_gain.reshape(depth, DA_VDIM, 1)
    lcb, lam = vec(lru_conv_b), vec(lru_lambda)
    ba, bx = vec(lru_ba.reshape(depth, -1)), vec(lru_bx.reshape(depth, -1))
    gf = final_norm.reshape(1, d)

    bias = _bias_tiles(rel_bias)

    x2 = x.reshape(b * s, d)
    for l in range(depth):
        lam_init = 0.8 - 0.6 * math.exp(-0.3 * l)
        last = l == depth - 1
        h2, zq, zr, wg2, wu2, wd2, wo = _ffn_in(x2, g1, wg, wu, wd, gm, wq, wr, l,
                                                (ffn2_gate, ffn2_up, ffn2_down, w_out))
        ya = _attention(zq.reshape(b, s, QKV_WIDTH), lam_params, gs, bias, l, lam_init)
        x2, *nxt = _out_ffn(h2, ya, zr, (sc_conv_w, lru_conv_w, lcb, wa, ba, wx, bx, lam),
                            wo, g2, wg2, wu2, wd2, gf, l, last,
                            () if last else (ffn1_gate, ffn1_up, ffn1_down, w_in))
        if not last:
            wg, wu, wd, wq, wr = nxt
    return x2.reshape(b, s, d)
```

```python
import functools
import math
from typing import Any, NamedTuple

import jax
import jax.numpy as jnp
from jax import lax
from jax.experimental import pallas as pl
from jax.experimental.pallas import tpu as pltpu

F32 = jnp.float32
BF16 = jnp.bfloat16

D_MODEL = 1024
D_FF = 2816
DA_WIDTH = 512
SC_WIDTH = 256
LRU_WIDTH = 256
DA_HEAD_DIM = 64
DA_HEADS = 4
DA_VDIM = 2 * DA_HEAD_DIM
NUM_BUCKETS = 32
MAX_DISTANCE = 128
SUBLN_EPS = 1e-5
SC_KERNEL = 3
LRU_CONV = 4
LRU_C = 8.0
RMS_EPS = 1e-6
NEG_INF = -1e30
QKV_WIDTH = 3 * DA_WIDTH
REST_WIDTH = 3 * SC_WIDTH + 2 * LRU_WIDTH
LOG2_E = math.log2(math.e)
Q_SCALE = DA_HEAD_DIM ** -0.5 * LOG2_E

LANES = 128
SUBLANES = 8
BF16_SUBLANES = 16
VT_ROWS = DA_VDIM + BF16_SUBLANES
VMEM_LIMIT_BYTES = 56 * 1024 * 1024
ROW_TILE = 512
ATTN_TILE = 512
SWEEP_UNROLL = 8


def _rms(x, g, eps):
    return x * lax.rsqrt(jnp.mean(x * x, axis=-1, keepdims=True) + eps) * g


def _swiglu(xn, wg_ref, wu_ref, wd_ref):
    g = jnp.dot(xn, wg_ref[...], preferred_element_type=F32)
    u = jnp.dot(xn, wu_ref[...], preferred_element_type=F32)
    a = (g * jax.nn.sigmoid(g) * u).astype(BF16)
    return jnp.dot(a, wd_ref[...], preferred_element_type=F32)


def _resident(shape, index):
    return pl.BlockSpec(shape, index, pipeline_mode=pl.Buffered(1))


class _CastJob(NamedTuple):
    src: Any
    layer: int
    chunks: int
    in_proj: bool


def _cast_job(src, layer, n_steps, in_proj=False):
    rows = src.shape[1]
    chunks = max(c for c in range(1, n_steps + 1)
                 if rows % c == 0 and (rows // c) % BF16_SUBLANES == 0)
    return _CastJob(src, layer, chunks, in_proj)


def _cast_widths(job):
    return (QKV_WIDTH, REST_WIDTH) if job.in_proj else (job.src.shape[2],)


def _cast_in_spec(job):
    rows, last = job.src.shape[1] // job.chunks, job.chunks - 1
    return pl.BlockSpec((None, rows, job.src.shape[2]),
                        lambda i: (job.layer, jnp.minimum(i, last), 0))


def _cast_out_specs(job):
    rows, last = job.src.shape[1] // job.chunks, job.chunks - 1
    return [pl.BlockSpec((rows, w), lambda i: (jnp.minimum(i, last), 0)) for w in _cast_widths(job)]


def _cast_out_shapes(job):
    return [jax.ShapeDtypeStruct((job.src.shape[1], w), BF16) for w in _cast_widths(job)]


def _cast_chunks(jobs, in_refs, out_refs, n_steps):
    out_refs = list(out_refs)
    for job, in_ref in zip(jobs, in_refs):
        outs = [out_refs.pop(0) for _ in _cast_widths(job)]

        def body(job=job, in_ref=in_ref, outs=outs):
            c = in_ref[...]
            if job.in_proj:
                col = lax.broadcasted_iota(jnp.int32, (1, QKV_WIDTH), 1)
                scale = jnp.where(col < DA_WIDTH, Q_SCALE, 1.0)
                outs[0][...] = (c[:, :QKV_WIDTH] * scale).astype(BF16)
                outs[1][...] = c[:, QKV_WIDTH:].astype(BF16)
            else:
                outs[0][...] = c.astype(BF16)

        if job.chunks == n_steps:
            body()
        else:
            pl.when(pl.program_id(0) < job.chunks)(body)


def _ffn_in_kernel(*refs, jobs, n_steps):
    (x_ref, g1_ref, wg_ref, wu_ref, wd_ref, gm_ref, wq_ref, wr_ref), refs = refs[:8], refs[8:]
    cast_in, refs = refs[:len(jobs)], refs[len(jobs):]
    (h_ref, zq_ref, zr_ref), cast_out = refs[:3], refs[3:]
    _cast_chunks(jobs, cast_in, cast_out, n_steps)
    half = ROW_TILE // 2
    for rows in (slice(0, half), slice(half, ROW_TILE)):
        x = x_ref[rows, :]
        xn = _rms(x, g1_ref[...], RMS_EPS).astype(BF16)
        h = x + 0.5 * _swiglu(xn, wg_ref, wu_ref, wd_ref)
        h_ref[rows, :] = h
        u = _rms(h, gm_ref[...], RMS_EPS).astype(BF16)
        zq_ref[rows, :] = jnp.dot(u, wq_ref[...], preferred_element_type=F32).astype(BF16)
        zr_ref[rows, :] = jnp.dot(u, wr_ref[...], preferred_element_type=F32)


def _ffn_in(x2, g1, wg, wu, wd, gm, wq, wr, l, cast_srcs):
    n = x2.shape[0]
    n_steps = n // ROW_TILE
    jobs = tuple(_cast_job(src, l, n_steps) for src in cast_srcs)
    row = lambda w: pl.BlockSpec((ROW_TILE, w), lambda i: (i, 0))
    vec = pl.BlockSpec((None, 1, D_MODEL), lambda i: (l, 0, 0))
    whole = lambda w: _resident(w.shape, lambda i: (0, 0))
    return pl.pallas_call(
        functools.partial(_ffn_in_kernel, jobs=jobs, n_steps=n_steps),
        grid=(n_steps,),
        in_specs=[row(D_MODEL), vec, whole(wg), whole(wu), whole(wd), vec, whole(wq), whole(wr)]
        + [_cast_in_spec(j) for j in jobs],
        out_specs=[row(D_MODEL), row(QKV_WIDTH), row(REST_WIDTH)]
        + [sp for j in jobs for sp in _cast_out_specs(j)],
        out_shape=[
            jax.ShapeDtypeStruct((n, D_MODEL), F32),
            jax.ShapeDtypeStruct((n, QKV_WIDTH), BF16),
            jax.ShapeDtypeStruct((n, REST_WIDTH), F32),
        ] + [sh for j in jobs for sh in _cast_out_shapes(j)],
        compiler_params=pltpu.CompilerParams(
            dimension_semantics=("arbitrary",), vmem_limit_bytes=VMEM_LIMIT_BYTES),
        name="ffn_in",
    )(x2, g1, wg, wu, wd, gm, wq, wr, *[j.src for j in jobs])


def _bias_kernel(rb_ref, o_ref):
    h = pl.program_id(0)
    t = ATTN_TILE
    sub = MAX_DISTANCE
    max_exact = NUM_BUCKETS // 2
    key = lax.broadcasted_iota(jnp.int32, (sub, sub), 0)
    qry = lax.broadcasted_iota(jnp.int32, (sub, sub), 1)

    def buckets(delta):
        dist = qry - key + delta
        n = jnp.maximum(dist, 0)
        nf = jnp.maximum(n, 1).astype(F32)
        large = max_exact + (jnp.log(nf / max_exact) / math.log(MAX_DISTANCE / max_exact)
                             * (NUM_BUCKETS - max_exact)).astype(jnp.int32)
        large = jnp.minimum(large, NUM_BUCKETS - 1)
        return jnp.where(n < max_exact, n, large), dist

    near = {delta: buckets(delta) for delta in (0, sub)}
    for mp in range(2):
        hm = 2 * h + mp
        far = rb_ref[NUM_BUCKETS - 1, hm]
        pattern = {}
        for delta, (bucket, dist) in near.items():
            val = jnp.zeros((sub, sub), F32)
            for j in range(NUM_BUCKETS):
                val = jnp.where(bucket == j, rb_ref[j, hm], val)
            pattern[delta] = jnp.where(dist >= 0, (val - far) * LOG2_E, NEG_INF)
        for off in range(2):
            for bk in range(t // sub):
                for bq in range(t // sub):
                    delta = (bq - bk) * sub + off * t
                    if delta in pattern:
                        blk = pattern[delta]
                    else:
                        blk = jnp.full((sub, sub), NEG_INF if delta < 0 else 0.0, F32)
                    o_ref[off, bk * sub:(bk + 1) * sub,
                          mp * t + bq * sub:mp * t + (bq + 1) * sub] = blk


def _bias_tiles(rel_bias):
    t = ATTN_TILE
    return pl.pallas_call(
        _bias_kernel,
        grid=(DA_HEADS,),
        in_specs=[pl.BlockSpec(memory_space=pltpu.SMEM)],
        out_specs=pl.BlockSpec((None, 2, t, 2 * t), lambda i: (i, 0, 0, 0)),
        out_shape=jax.ShapeDtypeStruct((DA_HEADS, 2, t, 2 * t), F32),
        compiler_params=pltpu.CompilerParams(dimension_semantics=("arbitrary",)),
        name="bias_tiles",
    )(rel_bias)


def _attn_kernel(lam_ref, g_ref, bias_ref, q_ref, k_ref, v_ref, o_ref,
                 qq_ref, vt_ref, s_ref, mx_ref, m_ref, acc_ref, *, lam_init, nq):
    t = ATTN_TILE
    chan = lax.broadcasted_iota(jnp.int32, (LANES, t), 0)
    for qi in range(nq):
        q_t = q_ref[qi * t:(qi + 1) * t, :].T
        zero = jnp.zeros_like(q_t)
        qq_ref[qi, :, 0:t] = jnp.where(chan < DA_HEAD_DIM, q_t, zero)
        qq_ref[qi, :, t:2 * t] = jnp.where(chan >= DA_HEAD_DIM, q_t, zero)

    lp = lam_ref[...]
    lam = (jnp.exp(jnp.sum(lp[0:1] * lp[1:2], axis=1, keepdims=True))
           - jnp.exp(jnp.sum(lp[2:3] * lp[3:4], axis=1, keepdims=True)) + lam_init)
    gain = g_ref[...] * (1.0 - lam_init)

    ones_row = lax.broadcasted_iota(jnp.int32, (VT_ROWS - DA_VDIM, t), 0) == 0
    for kj in range(nq):
        vt_ref[kj, 0:DA_VDIM, :] = v_ref[kj * t:(kj + 1) * t, :].T
        vt_ref[kj, DA_VDIM:VT_ROWS, :] = jnp.where(ones_row, 1.0, 0.0).astype(BF16)

    m_ref[...] = jnp.full(m_ref.shape, NEG_INF, F32)
    acc_ref[...] = jnp.zeros(acc_ref.shape, F32)

    def key_rows(kj):
        return pl.ds(kj * t, t) if isinstance(kj, int) else pl.ds(pl.multiple_of(kj * t, t), t)

    def scores(qi, kj, off, buf):
        s = jnp.dot(k_ref[key_rows(kj), :], qq_ref[qi], preferred_element_type=F32)
        if isinstance(off, int) and off == 1:
            lo = t - MAX_DISTANCE
            s = jnp.concatenate([s[:lo], s[lo:] + bias_ref[1, lo:, :]], axis=0)
        elif off is not None:
            s = s + bias_ref[off]
        s_ref[buf] = s
        mx_ref[buf] = jnp.max(s, axis=0, keepdims=True)

    def consume(qi, kj, off, buf):
        m_old = m_ref[qi]
        m_new = jnp.maximum(m_old, mx_ref[buf])
        alpha = jnp.exp2(m_old - m_new)
        p = jnp.exp2(s_ref[buf] - m_new)
        pv = jnp.dot(vt_ref[kj], p.astype(BF16), preferred_element_type=F32)
        acc_ref[qi] = alpha * acc_ref[qi] + pv
        m_ref[qi] = m_new

    def sweep(segments):
        segments = [seg for seg in segments if seg[0] > 0]
        if not segments:
            return
        scores(*segments[0][1](0), 0)
        done = 0
        for k, (n_steps, coords) in enumerate(segments):

            def group(i, carry, done=done, coords=coords):
                for u in range(SWEEP_UNROLL):
                    n = SWEEP_UNROLL * i + u
                    scores(*coords(n + 1), (done + u + 1) % 2)
                    consume(*coords(n), (done + u) % 2)
                return carry

            n_groups = (n_steps - 1) // SWEEP_UNROLL
            if n_groups == 1:
                group(0, 0)
            else:
                lax.fori_loop(0, n_groups, group, 0)
            for n in range(n_groups * SWEEP_UNROLL, n_steps):
                if n + 1 < n_steps:
                    scores(*coords(n + 1), (done + n + 1) % 2)
                elif k + 1 < len(segments):
                    scores(*segments[k + 1][1](0), (done + n + 1) % 2)
                consume(*coords(n), (done + n) % 2)
            done += n_steps

    def biased_coords(n):
        if isinstance(n, int):
            qi, off = (n + 1) // 2, n % 2
        else:
            qi, off = lax.shift_right_logical(n + 1, 1), lax.bitwise_and(n, 1)
        return qi, qi - off, off

    def far_coords(n):
        first = lambda qi: (qi - 1) * (qi - 2) // 2
        if isinstance(n, int):
            qi = max(c for c in range(2, nq) if first(c) <= n)
            return qi, n - first(qi), None
        qi = 2
        for c in range(3, nq):
            qi = qi + (n >= first(c)).astype(jnp.int32)
        return qi, n - lax.shift_right_logical((qi - 1) * (qi - 2), 1), None

    sweep([(2 * nq - 1, biased_coords), ((nq - 1) * (nq - 2) // 2, far_coords)])

    for qi in range(nq):
        on = acc_ref[qi, 0:DA_VDIM, :] * (1.0 / acc_ref[qi, DA_VDIM:DA_VDIM + 1, :])
        o = on[:, 0:t] - lam * on[:, t:2 * t]
        y = o * lax.rsqrt(jnp.mean(o * o, axis=0, keepdims=True) + SUBLN_EPS) * gain
        o_ref[:, qi * t:(qi + 1) * t] = y.astype(o_ref.dtype)


def _attention(zq3, lam_params, subln_g, bias, l, lam_init):
    b, s, _ = zq3.shape
    t = ATTN_TILE
    nq = s // t
    return pl.pallas_call(
        functools.partial(_attn_kernel, lam_init=lam_init, nq=nq),
        grid=(b, DA_HEADS),
        in_specs=[
            pl.BlockSpec((None, 4, DA_HEAD_DIM), lambda bi, h: (l, 0, 0)),
            pl.BlockSpec((None, DA_VDIM, 1), lambda bi, h: (l, 0, 0)),
            pl.BlockSpec((None, 2, t, 2 * t), lambda bi, h: (h, 0, 0, 0)),
            pl.BlockSpec((None, s, LANES), lambda bi, h: (bi, 0, h)),
            pl.BlockSpec((None, s, LANES), lambda bi, h: (bi, 0, DA_HEADS + h)),
            pl.BlockSpec((None, s, LANES), lambda bi, h: (bi, 0, 2 * DA_HEADS + h)),
        ],
        out_specs=pl.BlockSpec((None, DA_VDIM, s), lambda bi, h: (bi, h, 0)),
        out_shape=jax.ShapeDtypeStruct((b, DA_WIDTH, s), BF16),
        scratch_shapes=[
            pltpu.VMEM((nq, LANES, 2 * t), BF16),
            pltpu.VMEM((nq, VT_ROWS, t), BF16),
            pltpu.VMEM((2, t, 2 * t), F32),
            pltpu.VMEM((2, 1, 2 * t), F32),
            pltpu.VMEM((nq, 1, 2 * t), F32),
            pltpu.VMEM((nq, VT_ROWS, 2 * t), F32),
        ],
        compiler_params=pltpu.CompilerParams(
            dimension_semantics=("arbitrary", "arbitrary"),
            vmem_limit_bytes=VMEM_LIMIT_BYTES),
        name="diff_attn",
    )(lam_params, subln_g, bias, zq3, zq3, zq3)


def _shift_rows(x, halo, j):
    r = pltpu.roll(x, j, 0)
    head_row = lax.broadcasted_iota(jnp.int32, halo.shape, 0)
    head = jnp.where(head_row < j, pltpu.roll(halo, j, 0), r[:SUBLANES])
    return jnp.concatenate([head, r[SUBLANES:]], axis=0)


def _gelu_tanh(x):
    return 0.5 * x * (1.0 + jnp.tanh(math.sqrt(2.0 / math.pi) * (x + 0.044715 * (x * x * x))))


def _scan_step(a, b, s, pos):
    keep = pos >= s
    a_sh = jnp.where(keep, pltpu.roll(a, s, 0), 1.0)
    b_sh = jnp.where(keep, pltpu.roll(b, s, 0), 0.0)
    return a * a_sh, a * b_sh + b


def _mix_tile(z_ref, p, st, reset):
    rows = z_ref.shape[0]
    w = SC_WIDTH
    groups = rows // SUBLANES
    fresh = lambda x: jnp.where(reset, jnp.zeros_like(x), x)

    sc_b = z_ref[:, 0:w]
    cx = z_ref[:, w:2 * w] * z_ref[:, 2 * w:3 * w]
    halo = fresh(st.halo_sc[...])
    conv = p.scw[SC_KERNEL - 1:SC_KERNEL, :] * cx
    for j in range(1, SC_KERNEL):
        conv = conv + p.scw[SC_KERNEL - 1 - j:SC_KERNEL - j, :] * _shift_rows(cx, halo, j)
    st.halo_sc[...] = cx[rows - SUBLANES:, :]
    y_sc = sc_b * conv

    lx = z_ref[:, 3 * w:4 * w]
    lg = z_ref[:, 4 * w:5 * w]
    halo = fresh(st.halo_lx[...])
    xr = p.lcw[LRU_CONV - 1:LRU_CONV, :] * lx + p.lcb[...]
    for j in range(1, LRU_CONV):
        xr = xr + p.lcw[LRU_CONV - 1 - j:LRU_CONV - j, :] * _shift_rows(lx, halo, j)
    st.halo_lx[...] = lx[rows - SUBLANES:, :]

    xb = xr.astype(BF16)
    r = jax.nn.sigmoid(jnp.dot(xb, p.wa[...], preferred_element_type=F32) + p.ba[...])
    i = jax.nn.sigmoid(jnp.dot(xb, p.wx[...], preferred_element_type=F32) + p.bx[...])
    nl = -p.lam[...]
    softplus = jnp.maximum(nl, 0.0) + jnp.log1p(jnp.exp(-jnp.abs(nl)))
    log_a = (-LRU_C) * r * softplus
    a = jnp.exp(log_a)
    b = jnp.sqrt(-jnp.tanh(log_a) * (a * a + 1.0)) * (i * xr)

    pos = lax.bitwise_and(lax.broadcasted_iota(jnp.int32, (rows, w), 0), SUBLANES - 1)
    s = 1
    while s < SUBLANES:
        a, b = _scan_step(a, b, s, pos)
        s *= 2
    halves = w // LANES
    last = pl.ds(SUBLANES - 1, groups, stride=SUBLANES)
    for k in range(halves):
        st.a[k] = a[:, k * LANES:(k + 1) * LANES]
        st.b[k] = b[:, k * LANES:(k + 1) * LANES]
    at = jnp.concatenate([st.a[k, last, :] for k in range(halves)], axis=1)
    bt = jnp.concatenate([st.b[k, last, :] for k in range(halves)], axis=1)
    gpos = lax.broadcasted_iota(jnp.int32, (groups, w), 0)
    s = 1
    while s < groups:
        at, bt = _scan_step(at, bt, s, gpos)
        s *= 2
    h_in = fresh(st.h[0:1, :])
    h_end = bt + at * h_in
    st.h[0:1, :] = h_end[groups - 1:groups, :]
    st.c[...] = jnp.where(gpos >= 1, pltpu.roll(h_end, 1, 0), h_in)
    h = jnp.concatenate(
        [jnp.concatenate(
            [st.b[k, g * SUBLANES:(g + 1) * SUBLANES, :]
             + st.a[k, g * SUBLANES:(g + 1) * SUBLANES, :] * st.c[g:g + 1, k * LANES:(k + 1) * LANES]
             for g in range(groups)], axis=0) for k in range(halves)], axis=1)
    y_lru = _gelu_tanh(lg) * h
    return jnp.concatenate([y_sc, y_lru], axis=1).astype(BF16)


class _MixParams(NamedTuple):
    scw: Any
    lcw: Any
    lcb: Any
    wa: Any
    ba: Any
    wx: Any
    bx: Any
    lam: Any


class _MixState(NamedTuple):
    h: Any
    halo_sc: Any
    halo_lx: Any
    a: Any
    b: Any
    c: Any


def _out_ffn_kernel(*refs, final_norm, tiles_per_seq, jobs, n_steps):
    (h_ref, ya_ref, z0_ref, zn_ref, scw_ref, lcw_ref, lcb_ref, wa_ref, ba_ref, wx_ref, bx_ref,
     lam_ref, woa_ref, wom_ref, g2_ref, wg_ref, wu_ref, wd_ref, gf_ref), refs = refs[:19], refs[19:]
    cast_in, refs = refs[:len(jobs)], refs[len(jobs):]
    n_cast_out = sum(len(_cast_widths(j)) for j in jobs)
    o_ref, cast_out, refs = refs[0], refs[1:1 + n_cast_out], refs[1 + n_cast_out:]
    ym_ref, hs_ref, halo_sc_ref, halo_lx_ref, a_ref, b_ref, c_ref = refs
    _cast_chunks(jobs, cast_in, cast_out, n_steps)
    i = pl.program_id(0)
    p = _MixParams(scw_ref, lcw_ref, lcb_ref, wa_ref, ba_ref, wx_ref, bx_ref, lam_ref)
    st = _MixState(hs_ref, halo_sc_ref, halo_lx_ref, a_ref, b_ref, c_ref)

    @pl.when(i == 0)
    def _():
        ym_ref[0] = _mix_tile(z0_ref, p, st, i == 0)

    nxt = i + 1
    ym_cur = ym_ref[lax.rem(i, 2)]
    ym_ref[lax.rem(nxt, 2)] = _mix_tile(zn_ref, p, st, lax.rem(nxt, tiles_per_seq) == 0)

    h = (h_ref[...]
         + lax.dot_general(ya_ref[...], woa_ref[...], (((0,), (0,)), ((), ())),
                           preferred_element_type=F32)
         + jnp.dot(ym_cur, wom_ref[...], preferred_element_type=F32))
    hn = _rms(h, g2_ref[...], RMS_EPS).astype(BF16)
    x = h + 0.5 * _swiglu(hn, wg_ref, wu_ref, wd_ref)
    if final_norm:
        x = _rms(x, gf_ref[...], RMS_EPS)
    o_ref[...] = x


def _out_ffn(h2, ya_t, zr, mix_params, wo, g2, wg, wu, wd, gf, l, final_norm, cast_srcs):
    n = h2.shape[0]
    n_tiles = n // ROW_TILE
    jobs = tuple(_cast_job(src, l + 1, n_tiles, in_proj=(k == len(cast_srcs) - 1))
                 for k, src in enumerate(cast_srcs))
    whole = lambda w: _resident(w.shape, lambda i: (0, 0))
    tiles_per_seq = ya_t.shape[2] // ROW_TILE
    w = SC_WIDTH
    row = lambda width: pl.BlockSpec((ROW_TILE, width), lambda i: (i, 0))
    par = lambda rows: pl.BlockSpec((None, rows, w), lambda i: (l, 0, 0))
    mix_half = SC_WIDTH + LRU_WIDTH
    return pl.pallas_call(
        functools.partial(_out_ffn_kernel, final_norm=final_norm, tiles_per_seq=tiles_per_seq,
                          jobs=jobs, n_steps=n_tiles),
        grid=(n_tiles,),
        in_specs=[
            row(D_MODEL),
            pl.BlockSpec((None, DA_WIDTH, ROW_TILE),
                         lambda i: (i // tiles_per_seq, 0, i % tiles_per_seq)),
            pl.BlockSpec((ROW_TILE, REST_WIDTH), lambda i: (0, 0)),
            pl.BlockSpec((ROW_TILE, REST_WIDTH), lambda i: (jnp.minimum(i + 1, n_tiles - 1), 0)),
            par(SC_KERNEL), par(LRU_CONV), par(1), par(w), par(1), par(w), par(1), par(1),
            _resident((DA_WIDTH, D_MODEL), lambda i: (0, 0)),
            _resident((mix_half, D_MODEL), lambda i: (1, 0)),
            pl.BlockSpec((None, 1, D_MODEL), lambda i: (l, 0, 0)),
            whole(wg), whole(wu), whole(wd),
            pl.BlockSpec((1, D_MODEL), lambda i: (0, 0)),
        ] + [_cast_in_spec(j) for j in jobs],
        out_specs=[row(D_MODEL)] + [sp for j in jobs for sp in _cast_out_specs(j)],
        out_shape=[jax.ShapeDtypeStruct((n, D_MODEL), F32)]
        + [sh for j in jobs for sh in _cast_out_shapes(j)],
        scratch_shapes=[
            pltpu.VMEM((2, ROW_TILE, mix_half), BF16),
            pltpu.VMEM((SUBLANES, w), F32),
            pltpu.VMEM((SUBLANES, w), F32),
            pltpu.VMEM((SUBLANES, w), F32),
            pltpu.VMEM((w // LANES, ROW_TILE, LANES), F32),
            pltpu.VMEM((w // LANES, ROW_TILE, LANES), F32),
            pltpu.VMEM((ROW_TILE // SUBLANES, w), F32),
        ],
        compiler_params=pltpu.CompilerParams(
            dimension_semantics=("arbitrary",), vmem_limit_bytes=VMEM_LIMIT_BYTES),
        name="out_ffn",
    )(h2, ya_t, zr, zr, *mix_params, wo, wo, g2, wg, wu, wd, gf, *[j.src for j in jobs])


def _block_diag(w):
    depth, nb, blk, _ = w.shape
    eye = jnp.eye(nb, dtype=w.dtype)
    return jnp.einsum('lnij,nm->lnimj', w, eye).reshape(depth, nb * blk, nb * blk)


def kernel(x, rel_bias, ffn1_norm, ffn1_gate, ffn1_up, ffn1_down, mix_norm, w_in, w_out, lam_q1, lam_k1, lam_q2, lam_k2, subln_gain, sc_conv_w, lru_conv_w, lru_conv_b, lru_wa, lru_ba, lru_wx, lru_bx, lru_lambda, ffn2_norm, ffn2_gate, ffn2_up, ffn2_down, final_norm):
    b, s, d = x.shape
    depth = w_in.shape[0]
    assert d == D_MODEL and s % ATTN_TILE == 0 and s % ROW_TILE == 0

    bf = lambda w: w.astype(BF16)
    vec = lambda v: v.reshape(depth, 1, v.shape[-1])
    wg, wu, wd = bf(ffn1_gate[0]), bf(ffn1_up[0]), bf(ffn1_down[0])
    col_scale = jnp.where(jnp.arange(QKV_WIDTH) < DA_WIDTH, Q_SCALE, 1.0).astype(F32)
    wq, wr = bf(w_in[0, :, :QKV_WIDTH] * col_scale), bf(w_in[0, :, QKV_WIDTH:])
    wa, wx = bf(_block_diag(lru_wa)), bf(_block_diag(lru_wx))
    lam_params = jnp.stack([lam_q1, lam_k1, lam_q2, lam_k2], axis=1)
    g1, gm, g2 = vec(ffn1_norm), vec(mix_norm), vec(ffn2_norm)
    gs = subln_gain.reshape(depth, DA_VDIM, 1)
    lcb, lam = vec(lru_conv_b), vec(lru_lambda)
    ba, bx = vec(lru_ba.reshape(depth, -1)), vec(lru_bx.reshape(depth, -1))
    gf = final_norm.reshape(1, d)

    bias = _bias_tiles(rel_bias)

    x2 = x.reshape(b * s, d)
    for l in range(depth):
        lam_init = 0.8 - 0.6 * math.exp(-0.3 * l)
        last = l == depth - 1
        h2, zq, zr, wg2, wu2, wd2, wo = _ffn_in(x2, g1, wg, wu, wd, gm, wq, wr, l,
                                                (ffn2_gate, ffn2_up, ffn2_down, w_out))
        ya = _attention(zq.reshape(b, s, QKV_WIDTH), lam_params, gs, bias, l, lam_init)
        x2, *nxt = _out_ffn(h2, ya, zr, (sc_conv_w, lru_conv_w, lcb, wa, ba, wx, bx, lam),
                            wo, g2, wg2, wu2, wd2, gf, l, last,
                            () if last else (ffn1_gate, ffn1_up, ffn1_down, w_in))
        if not last:
            wg, wu, wd, wq, wr = nxt
    return x2.reshape(b, s, d)
```

```python
import functools
import math
from typing import Any, NamedTuple

import jax
import jax.numpy as jnp
from jax import lax
from jax.experimental import pallas as pl
from jax.experimental.pallas import tpu as pltpu

F32 = jnp.float32
BF16 = jnp.bfloat16

D_MODEL = 1024
D_FF = 2816
DA_WIDTH = 512
SC_WIDTH = 256
LRU_WIDTH = 256
DA_HEAD_DIM = 64
DA_HEADS = 4
DA_VDIM = 2 * DA_HEAD_DIM
NUM_BUCKETS = 32
MAX_DISTANCE = 128
SUBLN_EPS = 1e-5
SC_KERNEL = 3
LRU_CONV = 4
LRU_C = 8.0
RMS_EPS = 1e-6
NEG_INF = -1e30
QKV_WIDTH = 3 * DA_WIDTH
REST_WIDTH = 3 * SC_WIDTH + 2 * LRU_WIDTH
LOG2_E = math.log2(math.e)
Q_SCALE = DA_HEAD_DIM ** -0.5 * LOG2_E

LANES = 128
SUBLANES = 8
BF16_SUBLANES = 16
VT_ROWS = DA_VDIM + BF16_SUBLANES
VMEM_LIMIT_BYTES = 56 * 1024 * 1024
ROW_TILE = 512
ATTN_TILE = 512
SWEEP_UNROLL = 8


def _rms(x, g, eps):
    return x * lax.rsqrt(jnp.mean(x * x, axis=-1, keepdims=True) + eps) * g


_HALVES = (slice(0, ROW_TILE // 2), slice(ROW_TILE // 2, ROW_TILE))


def _gate_up(xn, wg_ref, wu_ref):
    g = [jnp.dot(v, wg_ref[...], preferred_element_type=F32) for v in xn]
    u = [jnp.dot(v, wu_ref[...], preferred_element_type=F32) for v in xn]
    return g, u


def _down(g, u, wd_ref):
    a = (g * jax.nn.sigmoid(g) * u).astype(BF16)
    return jnp.dot(a, wd_ref[...], preferred_element_type=F32)


def _resident(shape, index):
    return pl.BlockSpec(shape, index, pipeline_mode=pl.Buffered(1))


class _CastJob(NamedTuple):
    src: Any
    layer: int
    chunks: int
    in_proj: bool


def _cast_job(src, layer, n_steps, in_proj=False):
    rows = src.shape[1]
    chunks = max(c for c in range(1, n_steps + 1)
                 if rows % c == 0 and (rows // c) % BF16_SUBLANES == 0)
    return _CastJob(src, layer, chunks, in_proj)


def _cast_widths(job):
    return (QKV_WIDTH, REST_WIDTH) if job.in_proj else (job.src.shape[2],)


def _cast_in_spec(job):
    rows, last = job.src.shape[1] // job.chunks, job.chunks - 1
    return pl.BlockSpec((None, rows, job.src.shape[2]),
                        lambda i: (job.layer, jnp.minimum(i, last), 0))


def _cast_out_specs(job):
    rows, last = job.src.shape[1] // job.chunks, job.chunks - 1
    return [pl.BlockSpec((rows, w), lambda i: (jnp.minimum(i, last), 0)) for w in _cast_widths(job)]


def _cast_out_shapes(job):
    return [jax.ShapeDtypeStruct((job.src.shape[1], w), BF16) for w in _cast_widths(job)]


def _cast_chunks(jobs, in_refs, out_refs, n_steps):
    out_refs = list(out_refs)
    for job, in_ref in zip(jobs, in_refs):
        outs = [out_refs.pop(0) for _ in _cast_widths(job)]

        def body(job=job, in_ref=in_ref, outs=outs):
            c = in_ref[...]
            if job.in_proj:
                col = lax.broadcasted_iota(jnp.int32, (1, QKV_WIDTH), 1)
                scale = jnp.where(col < DA_WIDTH, Q_SCALE, 1.0)
                outs[0][...] = (c[:, :QKV_WIDTH] * scale).astype(BF16)
                outs[1][...] = c[:, QKV_WIDTH:].astype(BF16)
            else:
                outs[0][...] = c.astype(BF16)

        if job.chunks == n_steps:
            body()
        else:
            pl.when(pl.program_id(0) < job.chunks)(body)


def _ffn_in_kernel(*refs, jobs, n_steps):
    (x_ref, g1_ref, wg_ref, wu_ref, wd_ref, gm_ref, wq_ref, wr_ref), refs = refs[:8], refs[8:]
    cast_in, refs = refs[:len(jobs)], refs[len(jobs):]
    (h_ref, zq_ref, zr_ref), cast_out = refs[:3], refs[3:]
    _cast_chunks(jobs, cast_in, cast_out, n_steps)
    x = [x_ref[r, :] for r in _HALVES]
    xn = [_rms(v, g1_ref[...], RMS_EPS).astype(BF16) for v in x]
    g, u = _gate_up(xn, wg_ref, wu_ref)
    un = []
    for k, r in enumerate(_HALVES):
        h = x[k] + 0.5 * _down(g[k], u[k], wd_ref)
        h_ref[r, :] = h
        un.append(_rms(h, gm_ref[...], RMS_EPS).astype(BF16))
    for k, r in enumerate(_HALVES):
        zq_ref[r, :] = jnp.dot(un[k], wq_ref[...], preferred_element_type=F32).astype(BF16)
        zr_ref[r, :] = jnp.dot(un[k], wr_ref[...], preferred_element_type=F32)


def _ffn_in(x2, g1, wg, wu, wd, gm, wq, wr, l, cast_srcs):
    n = x2.shape[0]
    n_steps = n // ROW_TILE
    jobs = tuple(_cast_job(src, l, n_steps) for src in cast_srcs)
    row = lambda w: pl.BlockSpec((ROW_TILE, w), lambda i: (i, 0))
    vec = pl.BlockSpec((None, 1, D_MODEL), lambda i: (l, 0, 0))
    whole = lambda w: _resident(w.shape, lambda i: (0, 0))
    return pl.pallas_call(
        functools.partial(_ffn_in_kernel, jobs=jobs, n_steps=n_steps),
        grid=(n_steps,),
        in_specs=[row(D_MODEL), vec, whole(wg), whole(wu), whole(wd), vec, whole(wq), whole(wr)]
        + [_cast_in_spec(j) for j in jobs],
        out_specs=[row(D_MODEL), row(QKV_WIDTH), row(REST_WIDTH)]
        + [sp for j in jobs for sp in _cast_out_specs(j)],
        out_shape=[
            jax.ShapeDtypeStruct((n, D_MODEL), F32),
            jax.ShapeDtypeStruct((n, QKV_WIDTH), BF16),
            jax.ShapeDtypeStruct((n, REST_WIDTH), F32),
        ] + [sh for j in jobs for sh in _cast_out_shapes(j)],
        compiler_params=pltpu.CompilerParams(
            dimension_semantics=("arbitrary",), vmem_limit_bytes=VMEM_LIMIT_BYTES),
        name="ffn_in",
    )(x2, g1, wg, wu, wd, gm, wq, wr, *[j.src for j in jobs])


def _bias_kernel(rb_ref, o_ref):
    h = pl.program_id(0)
    t = ATTN_TILE
    sub = MAX_DISTANCE
    max_exact = NUM_BUCKETS // 2
    key = lax.broadcasted_iota(jnp.int32, (sub, sub), 0)
    qry = lax.broadcasted_iota(jnp.int32, (sub, sub), 1)

    def buckets(delta):
        dist = qry - key + delta
        n = jnp.maximum(dist, 0)
        nf = jnp.maximum(n, 1).astype(F32)
        large = max_exact + (jnp.log(nf / max_exact) / math.log(MAX_DISTANCE / max_exact)
                             * (NUM_BUCKETS - max_exact)).astype(jnp.int32)
        large = jnp.minimum(large, NUM_BUCKETS - 1)
        return jnp.where(n < max_exact, n, large), dist

    near = {delta: buckets(delta) for delta in (0, sub)}
    for mp in range(2):
        hm = 2 * h + mp
        far = rb_ref[NUM_BUCKETS - 1, hm]
        pattern = {}
        for delta, (bucket, dist) in near.items():
            val = jnp.zeros((sub, sub), F32)
            for j in range(NUM_BUCKETS):
                val = jnp.where(bucket == j, rb_ref[j, hm], val)
            pattern[delta] = jnp.where(dist >= 0, (val - far) * LOG2_E, NEG_INF)
        for off in range(2):
            for bk in range(t // sub):
                for bq in range(t // sub):
                    delta = (bq - bk) * sub + off * t
                    if delta in pattern:
                        blk = pattern[delta]
                    else:
                        blk = jnp.full((sub, sub), NEG_INF if delta < 0 else 0.0, F32)
                    o_ref[off, bk * sub:(bk + 1) * sub,
                          mp * t + bq * sub:mp * t + (bq + 1) * sub] = blk


def _bias_tiles(rel_bias):
    t = ATTN_TILE
    return pl.pallas_call(
        _bias_kernel,
        grid=(DA_HEADS,),
        in_specs=[pl.BlockSpec(memory_space=pltpu.SMEM)],
        out_specs=pl.BlockSpec((None, 2, t, 2 * t), lambda i: (i, 0, 0, 0)),
        out_shape=jax.ShapeDtypeStruct((DA_HEADS, 2, t, 2 * t), F32),
        compiler_params=pltpu.CompilerParams(dimension_semantics=("arbitrary",)),
        name="bias_tiles",
    )(rel_bias)


def _attn_kernel(lam_ref, g_ref, bias_ref, q_ref, k_ref, v_ref, o_ref,
                 qq_ref, vt_ref, s_ref, mx_ref, m_ref, acc_ref, *, lam_init, nq):
    t = ATTN_TILE
    chan = lax.broadcasted_iota(jnp.int32, (LANES, t), 0)
    for qi in range(nq):
        q_t = q_ref[qi * t:(qi + 1) * t, :].T
        zero = jnp.zeros_like(q_t)
        qq_ref[qi, :, 0:t] = jnp.where(chan < DA_HEAD_DIM, q_t, zero)
        qq_ref[qi, :, t:2 * t] = jnp.where(chan >= DA_HEAD_DIM, q_t, zero)

    lp = lam_ref[...]
    lam = (jnp.exp(jnp.sum(lp[0:1] * lp[1:2], axis=1, keepdims=True))
           - jnp.exp(jnp.sum(lp[2:3] * lp[3:4], axis=1, keepdims=True)) + lam_init)
    gain = g_ref[...] * (1.0 - lam_init)

    ones_row = lax.broadcasted_iota(jnp.int32, (VT_ROWS - DA_VDIM, t), 0) == 0
    for kj in range(nq):
        vt_ref[kj, 0:DA_VDIM, :] = v_ref[kj * t:(kj + 1) * t, :].T
        vt_ref[kj, DA_VDIM:VT_ROWS, :] = jnp.where(ones_row, 1.0, 0.0).astype(BF16)

    m_ref[...] = jnp.full(m_ref.shape, NEG_INF, F32)
    acc_ref[...] = jnp.zeros(acc_ref.shape, F32)

    def key_rows(kj):
        return pl.ds(kj * t, t) if isinstance(kj, int) else pl.ds(pl.multiple_of(kj * t, t), t)

    def scores(qi, kj, off, buf):
        s = jnp.dot(k_ref[key_rows(kj), :], qq_ref[qi], preferred_element_type=F32)
        if isinstance(off, int) and off == 1:
            lo = t - MAX_DISTANCE
            s = jnp.concatenate([s[:lo], s[lo:] + bias_ref[1, lo:, :]], axis=0)
        elif off is not None:
            s = s + bias_ref[off]
        s_ref[buf] = s
        mx_ref[buf] = jnp.max(s, axis=0, keepdims=True)

    def consume(qi, kj, off, buf):
        m_old = m_ref[qi]
        m_new = jnp.maximum(m_old, mx_ref[buf])
        alpha = jnp.exp2(m_old - m_new)
        p = jnp.exp2(s_ref[buf] - m_new)
        pv = jnp.dot(vt_ref[kj], p.astype(BF16), preferred_element_type=F32)
        acc_ref[qi] = alpha * acc_ref[qi] + pv
        m_ref[qi] = m_new

    def sweep(segments):
        segments = [seg for seg in segments if seg[0] > 0]
        if not segments:
            return
        scores(*segments[0][1](0), 0)
        done = 0
        for k, (n_steps, coords) in enumerate(segments):

            def group(i, carry, done=done, coords=coords):
                for u in range(SWEEP_UNROLL):
                    n = SWEEP_UNROLL * i + u
                    scores(*coords(n + 1), (done + u + 1) % 2)
                    consume(*coords(n), (done + u) % 2)
                return carry

            n_groups = (n_steps - 1) // SWEEP_UNROLL
            if n_groups == 1:
                group(0, 0)
            else:
                lax.fori_loop(0, n_groups, group, 0)
            for n in range(n_groups * SWEEP_UNROLL, n_steps):
                if n + 1 < n_steps:
                    scores(*coords(n + 1), (done + n + 1) % 2)
                elif k + 1 < len(segments):
                    scores(*segments[k + 1][1](0), (done + n + 1) % 2)
                consume(*coords(n), (done + n) % 2)
            done += n_steps

    def biased_coords(n):
        if isinstance(n, int):
            qi, off = (n + 1) // 2, n % 2
        else:
            qi, off = lax.shift_right_logical(n + 1, 1), lax.bitwise_and(n, 1)
        return qi, qi - off, off

    def far_coords(n):
        first = lambda qi: (qi - 1) * (qi - 2) // 2
        if isinstance(n, int):
            qi = max(c for c in range(2, nq) if first(c) <= n)
            return qi, n - first(qi), None
        qi = 2
        for c in range(3, nq):
            qi = qi + (n >= first(c)).astype(jnp.int32)
        return qi, n - lax.shift_right_logical((qi - 1) * (qi - 2), 1), None

    sweep([(2 * nq - 1, biased_coords), ((nq - 1) * (nq - 2) // 2, far_coords)])

    for qi in range(nq):
        on = acc_ref[qi, 0:DA_VDIM, :] * (1.0 / acc_ref[qi, DA_VDIM:DA_VDIM + 1, :])
        o = on[:, 0:t] - lam * on[:, t:2 * t]
        y = o * lax.rsqrt(jnp.mean(o * o, axis=0, keepdims=True) + SUBLN_EPS) * gain
        o_ref[:, qi * t:(qi + 1) * t] = y.astype(o_ref.dtype)


def _attention(zq3, lam_params, subln_g, bias, l, lam_init):
    b, s, _ = zq3.shape
    t = ATTN_TILE
    nq = s // t
    return pl.pallas_call(
        functools.partial(_attn_kernel, lam_init=lam_init, nq=nq),
        grid=(b, DA_HEADS),
        in_specs=[
            pl.BlockSpec((None, 4, DA_HEAD_DIM), lambda bi, h: (l, 0, 0)),
            pl.BlockSpec((None, DA_VDIM, 1), lambda bi, h: (l, 0, 0)),
            pl.BlockSpec((None, 2, t, 2 * t), lambda bi, h: (h, 0, 0, 0)),
            pl.BlockSpec((None, s, LANES), lambda bi, h: (bi, 0, h)),
            pl.BlockSpec((None, s, LANES), lambda bi, h: (bi, 0, DA_HEADS + h)),
            pl.BlockSpec((None, s, LANES), lambda bi, h: (bi, 0, 2 * DA_HEADS + h)),
        ],
        out_specs=pl.BlockSpec((None, DA_VDIM, s), lambda bi, h: (bi, h, 0)),
        out_shape=jax.ShapeDtypeStruct((b, DA_WIDTH, s), BF16),
        scratch_shapes=[
            pltpu.VMEM((nq, LANES, 2 * t), BF16),
            pltpu.VMEM((nq, VT_ROWS, t), BF16),
            pltpu.VMEM((2, t, 2 * t), F32),
            pltpu.VMEM((2, 1, 2 * t), F32),
            pltpu.VMEM((nq, 1, 2 * t), F32),
            pltpu.VMEM((nq, VT_ROWS, 2 * t), F32),
        ],
        compiler_params=pltpu.CompilerParams(
            dimension_semantics=("arbitrary", "arbitrary"),
            vmem_limit_bytes=VMEM_LIMIT_BYTES),
        name="diff_attn",
    )(lam_params, subln_g, bias, zq3, zq3, zq3)


def _shift_rows(x, halo, j):
    r = pltpu.roll(x, j, 0)
    head_row = lax.broadcasted_iota(jnp.int32, halo.shape, 0)
    head = jnp.where(head_row < j, pltpu.roll(halo, j, 0), r[:SUBLANES])
    return jnp.concatenate([head, r[SUBLANES:]], axis=0)


def _gelu_tanh(x):
    return 0.5 * x * (1.0 + jnp.tanh(math.sqrt(2.0 / math.pi) * (x + 0.044715 * (x * x * x))))


def _scan_step(a, b, s, pos):
    keep = pos >= s
    a_sh = jnp.where(keep, pltpu.roll(a, s, 0), 1.0)
    b_sh = jnp.where(keep, pltpu.roll(b, s, 0), 0.0)
    return a * a_sh, a * b_sh + b


def _mix_tile(z_ref, p, st, reset):
    rows = z_ref.shape[0]
    w = SC_WIDTH
    groups = rows // SUBLANES
    fresh = lambda x: jnp.where(reset, jnp.zeros_like(x), x)

    sc_b = z_ref[:, 0:w]
    cx = z_ref[:, w:2 * w] * z_ref[:, 2 * w:3 * w]
    halo = fresh(st.halo_sc[...])
    conv = p.scw[SC_KERNEL - 1:SC_KERNEL, :] * cx
    for j in range(1, SC_KERNEL):
        conv = conv + p.scw[SC_KERNEL - 1 - j:SC_KERNEL - j, :] * _shift_rows(cx, halo, j)
    st.halo_sc[...] = cx[rows - SUBLANES:, :]
    y_sc = sc_b * conv

    lx = z_ref[:, 3 * w:4 * w]
    lg = z_ref[:, 4 * w:5 * w]
    halo = fresh(st.halo_lx[...])
    xr = p.lcw[LRU_CONV - 1:LRU_CONV, :] * lx + p.lcb[...]
    for j in range(1, LRU_CONV):
        xr = xr + p.lcw[LRU_CONV - 1 - j:LRU_CONV - j, :] * _shift_rows(lx, halo, j)
    st.halo_lx[...] = lx[rows - SUBLANES:, :]

    xb = xr.astype(BF16)
    r = jax.nn.sigmoid(jnp.dot(xb, p.wa[...], preferred_element_type=F32) + p.ba[...])
    i = jax.nn.sigmoid(jnp.dot(xb, p.wx[...], preferred_element_type=F32) + p.bx[...])
    nl = -p.lam[...]
    softplus = jnp.maximum(nl, 0.0) + jnp.log1p(jnp.exp(-jnp.abs(nl)))
    log_a = (-LRU_C) * r * softplus
    a = jnp.exp(log_a)
    b = jnp.sqrt(-jnp.tanh(log_a) * (a * a + 1.0)) * (i * xr)

    pos = lax.bitwise_and(lax.broadcasted_iota(jnp.int32, (rows, w), 0), SUBLANES - 1)
    s = 1
    while s < SUBLANES:
        a, b = _scan_step(a, b, s, pos)
        s *= 2
    halves = w // LANES
    last = pl.ds(SUBLANES - 1, groups, stride=SUBLANES)
    for k in range(halves):
        st.a[k] = a[:, k * LANES:(k + 1) * LANES]
        st.b[k] = b[:, k * LANES:(k + 1) * LANES]
    at = jnp.concatenate([st.a[k, last, :] for k in range(halves)], axis=1)
    bt = jnp.concatenate([st.b[k, last, :] for k in range(halves)], axis=1)
    gpos = lax.broadcasted_iota(jnp.int32, (groups, w), 0)
    s = 1
    while s < groups:
        at, bt = _scan_step(at, bt, s, gpos)
        s *= 2
    h_in = fresh(st.h[0:1, :])
    h_end = bt + at * h_in
    st.h[0:1, :] = h_end[groups - 1:groups, :]
    st.c[...] = jnp.where(gpos >= 1, pltpu.roll(h_end, 1, 0), h_in)
    h = jnp.concatenate(
        [jnp.concatenate(
            [st.b[k, g * SUBLANES:(g + 1) * SUBLANES, :]
             + st.a[k, g * SUBLANES:(g + 1) * SUBLANES, :] * st.c[g:g + 1, k * LANES:(k + 1) * LANES]
             for g in range(groups)], axis=0) for k in range(halves)], axis=1)
    y_lru = _gelu_tanh(lg) * h
    return jnp.concatenate([y_sc, y_lru], axis=1).astype(BF16)


class _MixParams(NamedTuple):
    scw: Any
    lcw: Any
    lcb: Any
    wa: Any
    ba: Any
    wx: Any
    bx: Any
    lam: Any


class _MixState(NamedTuple):
    h: Any
    halo_sc: Any
    halo_lx: Any
    a: Any
    b: Any
    c: Any


def _out_ffn_kernel(*refs, final_norm, tiles_per_seq, jobs, n_steps):
    (h_ref, ya_ref, z0_ref, zn_ref, scw_ref, lcw_ref, lcb_ref, wa_ref, ba_ref, wx_ref, bx_ref,
     lam_ref, woa_ref, wom_ref, g2_ref, wg_ref, wu_ref, wd_ref, gf_ref), refs = refs[:19], refs[19:]
    cast_in, refs = refs[:len(jobs)], refs[len(jobs):]
    n_cast_out = sum(len(_cast_widths(j)) for j in jobs)
    o_ref, cast_out, refs = refs[0], refs[1:1 + n_cast_out], refs[1 + n_cast_out:]
    ym_ref, hs_ref, halo_sc_ref, halo_lx_ref, a_ref, b_ref, c_ref = refs
    _cast_chunks(jobs, cast_in, cast_out, n_steps)
    i = pl.program_id(0)
    p = _MixParams(scw_ref, lcw_ref, lcb_ref, wa_ref, ba_ref, wx_ref, bx_ref, lam_ref)
    st = _MixState(hs_ref, halo_sc_ref, halo_lx_ref, a_ref, b_ref, c_ref)

    @pl.when(i == 0)
    def _():
        ym_ref[0] = _mix_tile(z0_ref, p, st, i == 0)

    nxt = i + 1
    ym_cur = ym_ref[lax.rem(i, 2)]
    ym_ref[lax.rem(nxt, 2)] = _mix_tile(zn_ref, p, st, lax.rem(nxt, tiles_per_seq) == 0)

    h = [h_ref[r, :]
         + lax.dot_general(ya_ref[:, r], woa_ref[...], (((0,), (0,)), ((), ())),
                           preferred_element_type=F32)
         + jnp.dot(ym_cur[r, :], wom_ref[...], preferred_element_type=F32) for r in _HALVES]
    hn = [_rms(v, g2_ref[...], RMS_EPS).astype(BF16) for v in h]
    g, u = _gate_up(hn, wg_ref, wu_ref)
    for k, r in enumerate(_HALVES):
        x = h[k] + 0.5 * _down(g[k], u[k], wd_ref)
        if final_norm:
            x = _rms(x, gf_ref[...], RMS_EPS)
        o_ref[r, :] = x


def _out_ffn(h2, ya_t, zr, mix_params, wo, g2, wg, wu, wd, gf, l, final_norm, cast_srcs):
    n = h2.shape[0]
    n_tiles = n // ROW_TILE
    jobs = tuple(_cast_job(src, l + 1, n_tiles, in_proj=(k == len(cast_srcs) - 1))
                 for k, src in enumerate(cast_srcs))
    whole = lambda w: _resident(w.shape, lambda i: (0, 0))
    tiles_per_seq = ya_t.shape[2] // ROW_TILE
    w = SC_WIDTH
    row = lambda width: pl.BlockSpec((ROW_TILE, width), lambda i: (i, 0))
    par = lambda rows: pl.BlockSpec((None, rows, w), lambda i: (l, 0, 0))
    mix_half = SC_WIDTH + LRU_WIDTH
    return pl.pallas_call(
        functools.partial(_out_ffn_kernel, final_norm=final_norm, tiles_per_seq=tiles_per_seq,
                          jobs=jobs, n_steps=n_tiles),
        grid=(n_tiles,),
        in_specs=[
            row(D_MODEL),
            pl.BlockSpec((None, DA_WIDTH, ROW_TILE),
                         lambda i: (i // tiles_per_seq, 0, i % tiles_per_seq)),
            pl.BlockSpec((ROW_TILE, REST_WIDTH), lambda i: (0, 0)),
            pl.BlockSpec((ROW_TILE, REST_WIDTH), lambda i: (jnp.minimum(i + 1, n_tiles - 1), 0)),
            par(SC_KERNEL), par(LRU_CONV), par(1), par(w), par(1), par(w), par(1), par(1),
            _resident((DA_WIDTH, D_MODEL), lambda i: (0, 0)),
            _resident((mix_half, D_MODEL), lambda i: (1, 0)),
            pl.BlockSpec((None, 1, D_MODEL), lambda i: (l, 0, 0)),
            whole(wg), whole(wu), whole(wd),
            pl.BlockSpec((1, D_MODEL), lambda i: (0, 0)),
        ] + [_cast_in_spec(j) for j in jobs],
        out_specs=[row(D_MODEL)] + [sp for j in jobs for sp in _cast_out_specs(j)],
        out_shape=[jax.ShapeDtypeStruct((n, D_MODEL), F32)]
        + [sh for j in jobs for sh in _cast_out_shapes(j)],
        scratch_shapes=[
            pltpu.VMEM((2, ROW_TILE, mix_half), BF16),
            pltpu.VMEM((SUBLANES, w), F32),
            pltpu.VMEM((SUBLANES, w), F32),
            pltpu.VMEM((SUBLANES, w), F32),
            pltpu.VMEM((w // LANES, ROW_TILE, LANES), F32),
            pltpu.VMEM((w // LANES, ROW_TILE, LANES), F32),
            pltpu.VMEM((ROW_TILE // SUBLANES, w), F32),
        ],
        compiler_params=pltpu.CompilerParams(
            dimension_semantics=("arbitrary",), vmem_limit_bytes=VMEM_LIMIT_BYTES),
        name="out_ffn",
    )(h2, ya_t, zr, zr, *mix_params, wo, wo, g2, wg, wu, wd, gf, *[j.src for j in jobs])


def _block_diag(w):
    depth, nb, blk, _ = w.shape
    eye = jnp.eye(nb, dtype=w.dtype)
    return jnp.einsum('lnij,nm->lnimj', w, eye).reshape(depth, nb * blk, nb * blk)


def kernel(x, rel_bias, ffn1_norm, ffn1_gate, ffn1_up, ffn1_down, mix_norm, w_in, w_out, lam_q1, lam_k1, lam_q2, lam_k2, subln_gain, sc_conv_w, lru_conv_w, lru_conv_b, lru_wa, lru_ba, lru_wx, lru_bx, lru_lambda, ffn2_norm, ffn2_gate, ffn2_up, ffn2_down, final_norm):
    b, s, d = x.shape
    depth = w_in.shape[0]
    assert d == D_MODEL and s % ATTN_TILE == 0 and s % ROW_TILE == 0

    bf = lambda w: w.astype(BF16)
    vec = lambda v: v.reshape(depth, 1, v.shape[-1])
    wg, wu, wd = bf(ffn1_gate[0]), bf(ffn1_up[0]), bf(ffn1_down[0])
    col_scale = jnp.where(jnp.arange(QKV_WIDTH) < DA_WIDTH, Q_SCALE, 1.0).astype(F32)
    wq, wr = bf(w_in[0, :, :QKV_WIDTH] * col_scale), bf(w_in[0, :, QKV_WIDTH:])
    wa, wx = bf(_block_diag(lru_wa)), bf(_block_diag(lru_wx))
    lam_params = jnp.stack([lam_q1, lam_k1, lam_q2, lam_k2], axis=1)
    g1, gm, g2 = vec(ffn1_norm), vec(mix_norm), vec(ffn2_norm)
    gs = subln_gain.reshape(depth, DA_VDIM, 1)
    lcb, lam = vec(lru_conv_b), vec(lru_lambda)
    ba, bx = vec(lru_ba.reshape(depth, -1)), vec(lru_bx.reshape(depth, -1))
    gf = final_norm.reshape(1, d)

    bias = _bias_tiles(rel_bias)

    x2 = x.reshape(b * s, d)
    for l in range(depth):
        lam_init = 0.8 - 0.6 * math.exp(-0.3 * l)
        last = l == depth - 1
        h2, zq, zr, wg2, wu2, wd2, wo = _ffn_in(x2, g1, wg, wu, wd, gm, wq, wr, l,
                                                (ffn2_gate, ffn2_up, ffn2_down, w_out))
        ya = _attention(zq.reshape(b, s, QKV_WIDTH), lam_params, gs, bias, l, lam_init)
        x2, *nxt = _out_ffn(h2, ya, zr, (sc_conv_w, lru_conv_w, lcb, wa, ba, wx, bx, lam),
                            wo, g2, wg2, wu2, wd2, gf, l, last,
                            () if last else (ffn1_gate, ffn1_up, ffn1_down, w_in))
        if not last:
            wg, wu, wd, wq, wr = nxt
    return x2.reshape(b, s, d)
```

```python
import functools
import math
from typing import Any, NamedTuple

import jax
import jax.numpy as jnp
from jax import lax
from jax.experimental import pallas as pl
from jax.experimental.pallas import tpu as pltpu

F32 = jnp.float32
BF16 = jnp.bfloat16

D_MODEL = 1024
D_FF = 2816
DA_WIDTH = 512
SC_WIDTH = 256
LRU_WIDTH = 256
DA_HEAD_DIM = 64
DA_HEADS = 4
DA_VDIM = 2 * DA_HEAD_DIM
NUM_BUCKETS = 32
MAX_DISTANCE = 128
SUBLN_EPS = 1e-5
SC_KERNEL = 3
LRU_CONV = 4
LRU_C = 8.0
RMS_EPS = 1e-6
NEG_INF = -1e30
QKV_WIDTH = 3 * DA_WIDTH
REST_WIDTH = 3 * SC_WIDTH + 2 * LRU_WIDTH
LOG2_E = math.log2(math.e)
Q_SCALE = DA_HEAD_DIM ** -0.5 * LOG2_E

LANES = 128
SUBLANES = 8
BF16_SUBLANES = 16
VT_ROWS = DA_VDIM + BF16_SUBLANES
VMEM_LIMIT_BYTES = 56 * 1024 * 1024
ROW_TILE = 512
ATTN_TILE = 512
SWEEP_UNROLL = 8


def _rms(x, g, eps):
    return x * lax.rsqrt(jnp.mean(x * x, axis=-1, keepdims=True) + eps) * g


_HALVES = (slice(0, ROW_TILE // 2), slice(ROW_TILE // 2, ROW_TILE))


def _gate_up(xn, wg_ref, wu_ref):
    g = [jnp.dot(v, wg_ref[...], preferred_element_type=F32) for v in xn]
    u = [jnp.dot(v, wu_ref[...], preferred_element_type=F32) for v in xn]
    return g, u


def _down(g, u, wd_ref):
    a = (g * jax.nn.sigmoid(g) * u).astype(BF16)
    return jnp.dot(a, wd_ref[...], preferred_element_type=F32)


def _resident(shape, index):
    return pl.BlockSpec(shape, index, pipeline_mode=pl.Buffered(1))


class _CastJob(NamedTuple):
    src: Any
    layer: int
    chunks: int
    in_proj: bool


def _cast_job(src, layer, n_steps, in_proj=False):
    rows = src.shape[1]
    chunks = max(c for c in range(1, n_steps + 1)
                 if rows % c == 0 and (rows // c) % BF16_SUBLANES == 0)
    return _CastJob(src, layer, chunks, in_proj)


def _cast_widths(job):
    return (QKV_WIDTH, REST_WIDTH) if job.in_proj else (job.src.shape[2],)


def _cast_in_spec(job):
    rows, last = job.src.shape[1] // job.chunks, job.chunks - 1
    return pl.BlockSpec((None, rows, job.src.shape[2]),
                        lambda i: (job.layer, jnp.minimum(i, last), 0))


def _cast_out_specs(job):
    rows, last = job.src.shape[1] // job.chunks, job.chunks - 1
    return [pl.BlockSpec((rows, w), lambda i: (jnp.minimum(i, last), 0)) for w in _cast_widths(job)]


def _cast_out_shapes(job):
    return [jax.ShapeDtypeStruct((job.src.shape[1], w), BF16) for w in _cast_widths(job)]


def _cast_chunks(jobs, in_refs, out_refs, n_steps):
    out_refs = list(out_refs)
    for job, in_ref in zip(jobs, in_refs):
        outs = [out_refs.pop(0) for _ in _cast_widths(job)]

        def body(job=job, in_ref=in_ref, outs=outs):
            c = in_ref[...]
            if job.in_proj:
                col = lax.broadcasted_iota(jnp.int32, (1, QKV_WIDTH), 1)
                scale = jnp.where(col < DA_WIDTH, Q_SCALE, 1.0)
                outs[0][...] = (c[:, :QKV_WIDTH] * scale).astype(BF16)
                outs[1][...] = c[:, QKV_WIDTH:].astype(BF16)
            else:
                outs[0][...] = c.astype(BF16)

        if job.chunks == n_steps:
            body()
        else:
            pl.when(pl.program_id(0) < job.chunks)(body)


def _ffn_in_kernel(*refs, jobs, n_steps):
    (x_ref, g1_ref, wg_ref, wu_ref, wd_ref, gm_ref, wq_ref, wr_ref), refs = refs[:8], refs[8:]
    cast_in, refs = refs[:len(jobs)], refs[len(jobs):]
    (h_ref, zq_ref, zr_ref), cast_out = refs[:3], refs[3:]
    _cast_chunks(jobs, cast_in, cast_out, n_steps)
    x = [x_ref[r, :] for r in _HALVES]
    xn = [_rms(v, g1_ref[...], RMS_EPS).astype(BF16) for v in x]
    g, u = _gate_up(xn, wg_ref, wu_ref)
    un = []
    for k, r in enumerate(_HALVES):
        h = x[k] + 0.5 * _down(g[k], u[k], wd_ref)
        h_ref[r, :] = h
        un.append(_rms(h, gm_ref[...], RMS_EPS).astype(BF16))
    for k, r in enumerate(_HALVES):
        zq_ref[r, :] = jnp.dot(un[k], wq_ref[...], preferred_element_type=F32).astype(BF16)
        zr_ref[r, :] = jnp.dot(un[k], wr_ref[...], preferred_element_type=F32)


def _ffn_in(x2, g1, wg, wu, wd, gm, wq, wr, l, cast_srcs):
    n = x2.shape[0]
    n_steps = n // ROW_TILE
    jobs = tuple(_cast_job(src, l, n_steps) for src in cast_srcs)
    row = lambda w: pl.BlockSpec((ROW_TILE, w), lambda i: (i, 0))
    vec = pl.BlockSpec((None, 1, D_MODEL), lambda i: (l, 0, 0))
    whole = lambda w: _resident(w.shape, lambda i: (0, 0))
    return pl.pallas_call(
        functools.partial(_ffn_in_kernel, jobs=jobs, n_steps=n_steps),
        grid=(n_steps,),
        in_specs=[row(D_MODEL), vec, whole(wg), whole(wu), whole(wd), vec, whole(wq), whole(wr)]
        + [_cast_in_spec(j) for j in jobs],
        out_specs=[row(D_MODEL), row(QKV_WIDTH), row(REST_WIDTH)]
        + [sp for j in jobs for sp in _cast_out_specs(j)],
        out_shape=[
            jax.ShapeDtypeStruct((n, D_MODEL), F32),
            jax.ShapeDtypeStruct((n, QKV_WIDTH), BF16),
            jax.ShapeDtypeStruct((n, REST_WIDTH), F32),
        ] + [sh for j in jobs for sh in _cast_out_shapes(j)],
        compiler_params=pltpu.CompilerParams(
            dimension_semantics=("arbitrary",), vmem_limit_bytes=VMEM_LIMIT_BYTES),
        name="ffn_in",
    )(x2, g1, wg, wu, wd, gm, wq, wr, *[j.src for j in jobs])


def _bias_kernel(rb_ref, o_ref):
    h = pl.program_id(0)
    t = ATTN_TILE
    sub = MAX_DISTANCE
    max_exact = NUM_BUCKETS // 2
    key = lax.broadcasted_iota(jnp.int32, (sub, sub), 0)
    qry = lax.broadcasted_iota(jnp.int32, (sub, sub), 1)

    def buckets(delta):
        dist = qry - key + delta
        n = jnp.maximum(dist, 0)
        nf = jnp.maximum(n, 1).astype(F32)
        large = max_exact + (jnp.log(nf / max_exact) / math.log(MAX_DISTANCE / max_exact)
                             * (NUM_BUCKETS - max_exact)).astype(jnp.int32)
        large = jnp.minimum(large, NUM_BUCKETS - 1)
        return jnp.where(n < max_exact, n, large), dist

    near = {delta: buckets(delta) for delta in (0, sub)}
    for mp in range(2):
        hm = 2 * h + mp
        far = rb_ref[NUM_BUCKETS - 1, hm]
        pattern = {}
        for delta, (bucket, dist) in near.items():
            val = jnp.zeros((sub, sub), F32)
            for j in range(NUM_BUCKETS):
                val = jnp.where(bucket == j, rb_ref[j, hm], val)
            pattern[delta] = jnp.where(dist >= 0, (val - far) * LOG2_E, NEG_INF)
        for off in range(2):
            for bk in range(t // sub):
                for bq in range(t // sub):
                    delta = (bq - bk) * sub + off * t
                    if delta in pattern:
                        blk = pattern[delta]
                    else:
                        blk = jnp.full((sub, sub), NEG_INF if delta < 0 else 0.0, F32)
                    o_ref[off, bk * sub:(bk + 1) * sub,
                          mp * t + bq * sub:mp * t + (bq + 1) * sub] = blk


def _bias_tiles(rel_bias):
    t = ATTN_TILE
    return pl.pallas_call(
        _bias_kernel,
        grid=(DA_HEADS,),
        in_specs=[pl.BlockSpec(memory_space=pltpu.SMEM)],
        out_specs=pl.BlockSpec((None, 2, t, 2 * t), lambda i: (i, 0, 0, 0)),
        out_shape=jax.ShapeDtypeStruct((DA_HEADS, 2, t, 2 * t), F32),
        compiler_params=pltpu.CompilerParams(dimension_semantics=("arbitrary",)),
        name="bias_tiles",
    )(rel_bias)


def _attn_kernel(lam_ref, g_ref, bias_ref, q_ref, k_ref, v_ref, o_ref,
                 qq_ref, vt_ref, s_ref, mx_ref, m_ref, acc_ref, *, lam_init, nq):
    t = ATTN_TILE
    chan = lax.broadcasted_iota(jnp.int32, (LANES, t), 0)
    for qi in range(nq):
        q_t = q_ref[qi * t:(qi + 1) * t, :].T
        zero = jnp.zeros_like(q_t)
        qq_ref[qi, :, 0:t] = jnp.where(chan < DA_HEAD_DIM, q_t, zero)
        qq_ref[qi, :, t:2 * t] = jnp.where(chan >= DA_HEAD_DIM, q_t, zero)

    lp = lam_ref[...]
    lam = (jnp.exp(jnp.sum(lp[0:1] * lp[1:2], axis=1, keepdims=True))
           - jnp.exp(jnp.sum(lp[2:3] * lp[3:4], axis=1, keepdims=True)) + lam_init)
    gain = g_ref[...] * (1.0 - lam_init)

    ones_row = lax.broadcasted_iota(jnp.int32, (VT_ROWS - DA_VDIM, t), 0) == 0
    for kj in range(nq):
        vt_ref[kj, 0:DA_VDIM, :] = v_ref[kj * t:(kj + 1) * t, :].T
        vt_ref[kj, DA_VDIM:VT_ROWS, :] = jnp.where(ones_row, 1.0, 0.0).astype(BF16)

    m_ref[...] = jnp.full(m_ref.shape, NEG_INF, F32)
    acc_ref[...] = jnp.zeros(acc_ref.shape, F32)

    def key_rows(kj):
        return pl.ds(kj * t, t) if isinstance(kj, int) else pl.ds(pl.multiple_of(kj * t, t), t)

    def scores(qi, kj, off, buf):
        s = jnp.dot(k_ref[key_rows(kj), :], qq_ref[qi], preferred_element_type=F32)
        if isinstance(off, int) and off == 1:
            lo = t - MAX_DISTANCE
            s = jnp.concatenate([s[:lo], s[lo:] + bias_ref[1, lo:, :]], axis=0)
        elif off is not None:
            s = s + bias_ref[off]
        s_ref[buf] = s
        mx_ref[buf] = jnp.max(s, axis=0, keepdims=True)

    def consume(qi, kj, off, buf):
        m_old = m_ref[qi]
        m_new = jnp.maximum(m_old, mx_ref[buf])
        alpha = jnp.exp2(m_old - m_new)
        p = jnp.exp2(s_ref[buf] - m_new)
        pv = jnp.dot(vt_ref[kj], p.astype(BF16), preferred_element_type=F32)
        acc_ref[qi] = alpha * acc_ref[qi] + pv
        m_ref[qi] = m_new

    def sweep(segments):
        segments = [seg for seg in segments if seg[0] > 0]
        if not segments:
            return
        scores(*segments[0][1](0), 0)
        done = 0
        for k, (n_steps, coords) in enumerate(segments):

            def group(i, carry, done=done, coords=coords):
                for u in range(SWEEP_UNROLL):
                    n = SWEEP_UNROLL * i + u
                    scores(*coords(n + 1), (done + u + 1) % 2)
                    consume(*coords(n), (done + u) % 2)
                return carry

            n_groups = (n_steps - 1) // SWEEP_UNROLL
            if n_groups == 1:
                group(0, 0)
            else:
                lax.fori_loop(0, n_groups, group, 0)
            for n in range(n_groups * SWEEP_UNROLL, n_steps):
                if n + 1 < n_steps:
                    scores(*coords(n + 1), (done + n + 1) % 2)
                elif k + 1 < len(segments):
                    scores(*segments[k + 1][1](0), (done + n + 1) % 2)
                consume(*coords(n), (done + n) % 2)
            done += n_steps

    def biased_coords(n):
        if isinstance(n, int):
            qi, off = (n + 1) // 2, n % 2
        else:
            qi, off = lax.shift_right_logical(n + 1, 1), lax.bitwise_and(n, 1)
        return qi, qi - off, off

    def far_coords(n):
        first = lambda qi: (qi - 1) * (qi - 2) // 2
        if isinstance(n, int):
            qi = max(c for c in range(2, nq) if first(c) <= n)
            return qi, n - first(qi), None
        qi = 2
        for c in range(3, nq):
            qi = qi + (n >= first(c)).astype(jnp.int32)
        return qi, n - lax.shift_right_logical((qi - 1) * (qi - 2), 1), None

    sweep([(2 * nq - 1, biased_coords), ((nq - 1) * (nq - 2) // 2, far_coords)])

    for qi in range(nq):
        on = acc_ref[qi, 0:DA_VDIM, :] * (1.0 / acc_ref[qi, DA_VDIM:DA_VDIM + 1, :])
        o = on[:, 0:t] - lam * on[:, t:2 * t]
        y = o * lax.rsqrt(jnp.mean(o * o, axis=0, keepdims=True) + SUBLN_EPS) * gain
        o_ref[:, qi * t:(qi + 1) * t] = y.astype(o_ref.dtype)


def _attention(zq3, lam_params, subln_g, bias, l, lam_init):
    b, s, _ = zq3.shape
    t = ATTN_TILE
    nq = s // t
    return pl.pallas_call(
        functools.partial(_attn_kernel, lam_init=lam_init, nq=nq),
        grid=(b, DA_HEADS),
        in_specs=[
            pl.BlockSpec((None, 4, DA_HEAD_DIM), lambda bi, h: (l, 0, 0)),
            pl.BlockSpec((None, DA_VDIM, 1), lambda bi, h: (l, 0, 0)),
            pl.BlockSpec((None, 2, t, 2 * t), lambda bi, h: (h, 0, 0, 0)),
            pl.BlockSpec((None, s, LANES), lambda bi, h: (bi, 0, h)),
            pl.BlockSpec((None, s, LANES), lambda bi, h: (bi, 0, DA_HEADS + h)),
            pl.BlockSpec((None, s, LANES), lambda bi, h: (bi, 0, 2 * DA_HEADS + h)),
        ],
        out_specs=pl.BlockSpec((None, DA_VDIM, s), lambda bi, h: (bi, h, 0)),
        out_shape=jax.ShapeDtypeStruct((b, DA_WIDTH, s), BF16),
        scratch_shapes=[
            pltpu.VMEM((nq, LANES, 2 * t), BF16),
            pltpu.VMEM((nq, VT_ROWS, t), BF16),
            pltpu.VMEM((2, t, 2 * t), F32),
            pltpu.VMEM((2, 1, 2 * t), F32),
            pltpu.VMEM((nq, 1, 2 * t), F32),
            pltpu.VMEM((nq, VT_ROWS, 2 * t), F32),
        ],
        compiler_params=pltpu.CompilerParams(
            dimension_semantics=("arbitrary", "arbitrary"),
            vmem_limit_bytes=VMEM_LIMIT_BYTES),
        name="diff_attn",
    )(lam_params, subln_g, bias, zq3, zq3, zq3)


def _shift_rows(x, halo, j):
    r = pltpu.roll(x, j, 0)
    head_row = lax.broadcasted_iota(jnp.int32, halo.shape, 0)
    head = jnp.where(head_row < j, pltpu.roll(halo, j, 0), r[:SUBLANES])
    return jnp.concatenate([head, r[SUBLANES:]], axis=0)


def _gelu_tanh(x):
    return 0.5 * x * (1.0 + jnp.tanh(math.sqrt(2.0 / math.pi) * (x + 0.044715 * (x * x * x))))


def _scan_step(a, b, s, pos):
    keep = pos >= s
    a_sh = jnp.where(keep, pltpu.roll(a, s, 0), 1.0)
    b_sh = jnp.where(keep, pltpu.roll(b, s, 0), 0.0)
    return a * a_sh, a * b_sh + b


def _mix_slab(z_ref, p, st, reset, k):
    rows = z_ref.shape[0]
    w = SC_WIDTH
    groups = rows // SUBLANES
    fresh = lambda x: jnp.where(reset, jnp.zeros_like(x), x)
    pos = lax.bitwise_and(lax.broadcasted_iota(jnp.int32, (rows, LANES), 0), SUBLANES - 1)
    gpos = lax.broadcasted_iota(jnp.int32, (groups, LANES), 0)
    last = pl.ds(SUBLANES - 1, groups, stride=SUBLANES)
    cs = slice(k * LANES, (k + 1) * LANES)
    col = lambda j: slice(j * w + k * LANES, j * w + (k + 1) * LANES)

    cx = z_ref[:, col(1)] * z_ref[:, col(2)]
    halo = fresh(st.halo_sc[:, cs])
    conv = p.scw[SC_KERNEL - 1:SC_KERNEL, cs] * cx
    for j in range(1, SC_KERNEL):
        conv = conv + p.scw[SC_KERNEL - 1 - j:SC_KERNEL - j, cs] * _shift_rows(cx, halo, j)
    st.halo_sc[:, cs] = cx[rows - SUBLANES:, :]
    y_sc = z_ref[:, col(0)] * conv

    lx = z_ref[:, col(3)]
    halo = fresh(st.halo_lx[:, cs])
    xr = p.lcw[LRU_CONV - 1:LRU_CONV, cs] * lx + p.lcb[:, cs]
    for j in range(1, LRU_CONV):
        xr = xr + p.lcw[LRU_CONV - 1 - j:LRU_CONV - j, cs] * _shift_rows(lx, halo, j)
    st.halo_lx[:, cs] = lx[rows - SUBLANES:, :]

    xb = xr.astype(BF16)
    r = jax.nn.sigmoid(jnp.dot(xb, p.wa[cs, cs], preferred_element_type=F32) + p.ba[:, cs])
    i = jax.nn.sigmoid(jnp.dot(xb, p.wx[cs, cs], preferred_element_type=F32) + p.bx[:, cs])
    nl = -p.lam[:, cs]
    softplus = jnp.maximum(nl, 0.0) + jnp.log1p(jnp.exp(-jnp.abs(nl)))
    log_a = (-LRU_C) * r * softplus
    a = jnp.exp(log_a)
    b = jnp.sqrt(-jnp.tanh(log_a) * (a * a + 1.0)) * (i * xr)

    s = 1
    while s < SUBLANES:
        a, b = _scan_step(a, b, s, pos)
        s *= 2
    st.a[k] = a
    st.b[k] = b
    at = st.a[k, last, :]
    bt = st.b[k, last, :]
    s = 1
    while s < groups:
        at, bt = _scan_step(at, bt, s, gpos)
        s *= 2
    h_in = fresh(st.h[0:1, cs])
    h_end = bt + at * h_in
    st.h[0:1, cs] = h_end[groups - 1:groups, :]
    st.c[:, cs] = jnp.where(gpos >= 1, pltpu.roll(h_end, 1, 0), h_in)
    h = jnp.concatenate(
        [st.b[k, g * SUBLANES:(g + 1) * SUBLANES, :]
         + st.a[k, g * SUBLANES:(g + 1) * SUBLANES, :] * st.c[g:g + 1, cs]
         for g in range(groups)], axis=0)
    y_lru = _gelu_tanh(z_ref[:, col(4)]) * h
    return y_sc.astype(BF16), y_lru.astype(BF16)


def _mix_tile(z_ref, p, st, reset, ym_ref, slot):
    w = SC_WIDTH
    for k in range(w // LANES):
        y_sc, y_lru = _mix_slab(z_ref, p, st, reset, k)
        ym_ref[slot, :, k * LANES:(k + 1) * LANES] = y_sc
        ym_ref[slot, :, w + k * LANES:w + (k + 1) * LANES] = y_lru


class _MixParams(NamedTuple):
    scw: Any
    lcw: Any
    lcb: Any
    wa: Any
    ba: Any
    wx: Any
    bx: Any
    lam: Any


class _MixState(NamedTuple):
    h: Any
    halo_sc: Any
    halo_lx: Any
    a: Any
    b: Any
    c: Any


def _out_ffn_kernel(*refs, final_norm, tiles_per_seq, jobs, n_steps):
    (h_ref, ya_ref, z0_ref, zn_ref, scw_ref, lcw_ref, lcb_ref, wa_ref, ba_ref, wx_ref, bx_ref,
     lam_ref, woa_ref, wom_ref, g2_ref, wg_ref, wu_ref, wd_ref, gf_ref), refs = refs[:19], refs[19:]
    cast_in, refs = refs[:len(jobs)], refs[len(jobs):]
    n_cast_out = sum(len(_cast_widths(j)) for j in jobs)
    o_ref, cast_out, refs = refs[0], refs[1:1 + n_cast_out], refs[1 + n_cast_out:]
    ym_ref, hs_ref, halo_sc_ref, halo_lx_ref, a_ref, b_ref, c_ref = refs
    _cast_chunks(jobs, cast_in, cast_out, n_steps)
    i = pl.program_id(0)
    p = _MixParams(scw_ref, lcw_ref, lcb_ref, wa_ref, ba_ref, wx_ref, bx_ref, lam_ref)
    st = _MixState(hs_ref, halo_sc_ref, halo_lx_ref, a_ref, b_ref, c_ref)

    @pl.when(i == 0)
    def _():
        _mix_tile(z0_ref, p, st, i == 0, ym_ref, 0)

    nxt = i + 1
    ym_cur = ym_ref[lax.rem(i, 2)]
    _mix_tile(zn_ref, p, st, lax.rem(nxt, tiles_per_seq) == 0, ym_ref, lax.rem(nxt, 2))

    h = [h_ref[r, :]
         + lax.dot_general(ya_ref[:, r], woa_ref[...], (((0,), (0,)), ((), ())),
                           preferred_element_type=F32)
         + jnp.dot(ym_cur[r, :], wom_ref[...], preferred_element_type=F32) for r in _HALVES]
    hn = [_rms(v, g2_ref[...], RMS_EPS).astype(BF16) for v in h]
    g, u = _gate_up(hn, wg_ref, wu_ref)
    for k, r in enumerate(_HALVES):
        x = h[k] + 0.5 * _down(g[k], u[k], wd_ref)
        if final_norm:
            x = _rms(x, gf_ref[...], RMS_EPS)
        o_ref[r, :] = x


def _out_ffn(h2, ya_t, zr, mix_params, wo, g2, wg, wu, wd, gf, l, final_norm, cast_srcs):
    n = h2.shape[0]
    n_tiles = n // ROW_TILE
    jobs = tuple(_cast_job(src, l + 1, n_tiles, in_proj=(k == len(cast_srcs) - 1))
                 for k, src in enumerate(cast_srcs))
    whole = lambda w: _resident(w.shape, lambda i: (0, 0))
    tiles_per_seq = ya_t.shape[2] // ROW_TILE
    w = SC_WIDTH
    row = lambda width: pl.BlockSpec((ROW_TILE, width), lambda i: (i, 0))
    par = lambda rows: pl.BlockSpec((None, rows, w), lambda i: (l, 0, 0))
    mix_half = SC_WIDTH + LRU_WIDTH
    return pl.pallas_call(
        functools.partial(_out_ffn_kernel, final_norm=final_norm, tiles_per_seq=tiles_per_seq,
                          jobs=jobs, n_steps=n_tiles),
        grid=(n_tiles,),
        in_specs=[
            row(D_MODEL),
            pl.BlockSpec((None, DA_WIDTH, ROW_TILE),
                         lambda i: (i // tiles_per_seq, 0, i % tiles_per_seq)),
            pl.BlockSpec((ROW_TILE, REST_WIDTH), lambda i: (0, 0)),
            pl.BlockSpec((ROW_TILE, REST_WIDTH), lambda i: (jnp.minimum(i + 1, n_tiles - 1), 0)),
            par(SC_KERNEL), par(LRU_CONV), par(1), par(w), par(1), par(w), par(1), par(1),
            _resident((DA_WIDTH, D_MODEL), lambda i: (0, 0)),
            _resident((mix_half, D_MODEL), lambda i: (1, 0)),
            pl.BlockSpec((None, 1, D_MODEL), lambda i: (l, 0, 0)),
            whole(wg), whole(wu), whole(wd),
            pl.BlockSpec((1, D_MODEL), lambda i: (0, 0)),
        ] + [_cast_in_spec(j) for j in jobs],
        out_specs=[row(D_MODEL)] + [sp for j in jobs for sp in _cast_out_specs(j)],
        out_shape=[jax.ShapeDtypeStruct((n, D_MODEL), F32)]
        + [sh for j in jobs for sh in _cast_out_shapes(j)],
        scratch_shapes=[
            pltpu.VMEM((2, ROW_TILE, mix_half), BF16),
            pltpu.VMEM((SUBLANES, w), F32),
            pltpu.VMEM((SUBLANES, w), F32),
            pltpu.VMEM((SUBLANES, w), F32),
            pltpu.VMEM((w // LANES, ROW_TILE, LANES), F32),
            pltpu.VMEM((w // LANES, ROW_TILE, LANES), F32),
            pltpu.VMEM((ROW_TILE // SUBLANES, w), F32),
        ],
        compiler_params=pltpu.CompilerParams(
            dimension_semantics=("arbitrary",), vmem_limit_bytes=VMEM_LIMIT_BYTES),
        name="out_ffn",
    )(h2, ya_t, zr, zr, *mix_params, wo, wo, g2, wg, wu, wd, gf, *[j.src for j in jobs])


def _block_diag(w):
    depth, nb, blk, _ = w.shape
    eye = jnp.eye(nb, dtype=w.dtype)
    return jnp.einsum('lnij,nm->lnimj', w, eye).reshape(depth, nb * blk, nb * blk)


def kernel(x, rel_bias, ffn1_norm, ffn1_gate, ffn1_up, ffn1_down, mix_norm, w_in, w_out, lam_q1, lam_k1, lam_q2, lam_k2, subln_gain, sc_conv_w, lru_conv_w, lru_conv_b, lru_wa, lru_ba, lru_wx, lru_bx, lru_lambda, ffn2_norm, ffn2_gate, ffn2_up, ffn2_down, final_norm):
    b, s, d = x.shape
    depth = w_in.shape[0]
    assert d == D_MODEL and s % ATTN_TILE == 0 and s % ROW_TILE == 0

    bf = lambda w: w.astype(BF16)
    vec = lambda v: v.reshape(depth, 1, v.shape[-1])
    wg, wu, wd = bf(ffn1_gate[0]), bf(ffn1_up[0]), bf(ffn1_down[0])
    col_scale = jnp.where(jnp.arange(QKV_WIDTH) < DA_WIDTH, Q_SCALE, 1.0).astype(F32)
    wq, wr = bf(w_in[0, :, :QKV_WIDTH] * col_scale), bf(w_in[0, :, QKV_WIDTH:])
    wa, wx = bf(_block_diag(lru_wa)), bf(_block_diag(lru_wx))
    lam_params = jnp.stack([lam_q1, lam_k1, lam_q2, lam_k2], axis=1)
    g1, gm, g2 = vec(ffn1_norm), vec(mix_norm), vec(ffn2_norm)
    gs = subln_gain.reshape(depth, DA_VDIM, 1)
    lcb, lam = vec(lru_conv_b), vec(lru_lambda)
    ba, bx = vec(lru_ba.reshape(depth, -1)), vec(lru_bx.reshape(depth, -1))
    gf = final_norm.reshape(1, d)

    bias = _bias_tiles(rel_bias)

    x2 = x.reshape(b * s, d)
    for l in range(depth):
        lam_init = 0.8 - 0.6 * math.exp(-0.3 * l)
        last = l == depth - 1
        h2, zq, zr, wg2, wu2, wd2, wo = _ffn_in(x2, g1, wg, wu, wd, gm, wq, wr, l,
                                                (ffn2_gate, ffn2_up, ffn2_down, w_out))
        ya = _attention(zq.reshape(b, s, QKV_WIDTH), lam_params, gs, bias, l, lam_init)
        x2, *nxt = _out_ffn(h2, ya, zr, (sc_conv_w, lru_conv_w, lcb, wa, ba, wx, bx, lam),
                            wo, g2, wg2, wu2, wd2, gf, l, last,
                            () if last else (ffn1_gate, ffn1_up, ffn1_down, w_in))
        if not last:
            wg, wu, wd, wq, wr = nxt
    return x2.reshape(b, s, d)
```
